```python
import jax, jax.numpy as jnp
from jax import lax
import numpy as np

D_MODEL = 1024
BATCH = 2
SEQ = 16384
DEPTH = 2
DEC_BATCH = 8
DEC_SEQ = 4096
PAST_LEN = 128

N_EVEN = (DEPTH + 1) // 2
N_ODD = DEPTH // 2
EPS = 1e-6

D_RNN = D_MODEL // 2
RG_HEADS = 8
RG_HEAD_DIM = D_RNN // RG_HEADS
RG_CONV = 4
RG_C = 8.0
MLA_HEADS = 8
Q_LORA = D_MODEL // 4
KV_LORA = D_MODEL // 8
QK_NOPE = 64
QK_ROPE = 32
V_DIM = 64
QK_DIM = QK_NOPE + QK_ROPE
ROPE_THETA = 10000.0
Q_BLOCK = 128
AB_IN = 2 * D_RNN + Q_LORA + KV_LORA + QK_ROPE
AB_OUT = D_RNN + MLA_HEADS * V_DIM
D_CONV = D_MODEL
C_CONV = 3
PEER_HEADS = 8
N_KEYS = 128
N_EXPERTS = N_KEYS * N_KEYS
PEER_TOPK = 16
D_KEY = 256
PEER_CHUNK = 128

kernel_name = 'hybrid_bidir_rglru_mla_shortconv_peer'


def rms_norm(x, g):
    xf = x.astype(jnp.float32)
    y = xf * lax.rsqrt(jnp.mean(xf * xf, axis=-1, keepdims=True) + EPS)
    return (y * g.astype(jnp.float32)).astype(x.dtype)


def depthwise_conv(x, w, pad):
    return lax.conv_general_dilated(x, w[:, None, :].astype(x.dtype), window_strides=(1,), padding=[pad],
                                    dimension_numbers=('NWC', 'WIO', 'NWC'), feature_group_count=x.shape[-1])


def _lin_combine(e1, e2):
    a1, b1 = e1
    a2, b2 = e2
    return a1 * a2, a2 * b1 + b2


def rg_lru_scan(xc, wa, ba, wx, bx, lam):
    b_, s_, _ = xc.shape
    xh = xc.reshape(b_, s_, RG_HEADS, RG_HEAD_DIM)
    r = jax.nn.sigmoid(jnp.einsum('bshi,hij->bshj', xh, wa.astype(jnp.float32)).reshape(b_, s_, D_RNN) + ba.astype(jnp.float32))
    i = jax.nn.sigmoid(jnp.einsum('bshi,hij->bshj', xh, wx.astype(jnp.float32)).reshape(b_, s_, D_RNN) + bx.astype(jnp.float32))
    log_a = -RG_C * r * jax.nn.softplus(-lam.astype(jnp.float32))
    a = jnp.exp(log_a)
    b = jnp.sqrt(-jnp.expm1(2.0 * log_a)) * (i * xc)
    _, h = lax.associative_scan(_lin_combine, (a, b), axis=1)
    return h


def rope_tables(s_):
    inv = 1.0 / (ROPE_THETA ** (jnp.arange(0, QK_ROPE, 2, dtype=jnp.float32) / QK_ROPE))
    ang = jnp.arange(s_, dtype=jnp.float32)[:, None] * inv[None, :]
    return jnp.cos(ang), jnp.sin(ang)


def apply_rope(t, cos, sin):
    half = QK_ROPE // 2
    nope = t[..., :QK_NOPE]
    rot = t[..., QK_NOPE:].astype(jnp.float32)
    r1, r2 = rot[..., :half], rot[..., half:]
    c = cos[None, :, None, :]
    s = sin[None, :, None, :]
    rot = jnp.concatenate([r1 * c - r2 * s, r2 * c + r1 * s], axis=-1)
    return jnp.concatenate([nope, rot.astype(t.dtype)], axis=-1)


def block_attention(q, k, v):
    b_, s_, h_, dq = q.shape
    nb = s_ // Q_BLOCK
    qb = q.reshape(b_, nb, Q_BLOCK, h_, dq).transpose(1, 0, 2, 3, 4)
    scale = QK_DIM ** -0.5

    def one(qblk):
        s = jnp.einsum('bqhd,bkhd->bhqk', qblk, k).astype(jnp.float32) * scale
        p = jax.nn.softmax(s, axis=-1).astype(v.dtype)
        return jnp.einsum('bhqk,bkhd->bqhd', p, v)

    o = lax.map(one, qb)
    return o.transpose(1, 0, 2, 3, 4).reshape(b_, s_, h_ * V_DIM)


def mixer_ab(h, w_in, conv_w, conv_b, rg_wa, rg_ba, rg_wx, rg_bx, rg_lam,
             q_norm, w_uq, kv_norm, w_ukv, qn_q, qn_k, w_out):
    b_, s_, _ = h.shape
    z = h @ w_in
    cuts = [D_RNN, 2 * D_RNN, 2 * D_RNN + Q_LORA, 2 * D_RNN + Q_LORA + KV_LORA]
    xr, yr, q_lat, kv_lat, k_rope = jnp.split(z, cuts, axis=-1)
    xr = depthwise_conv(xr, conv_w, (2, 1)) + conv_b
    xf = xr.astype(jnp.float32)
    h_fw = rg_lru_scan(xf, rg_wa[0], rg_ba[0], rg_wx[0], rg_bx[0], rg_lam[0])
    h_bw = jnp.flip(rg_lru_scan(jnp.flip(xf, axis=1), rg_wa[1], rg_ba[1], rg_wx[1], rg_bx[1], rg_lam[1]), axis=1)
    rg_out = (h_fw + h_bw).astype(h.dtype) * jax.nn.gelu(yr)
    q = (rms_norm(q_lat, q_norm) @ w_uq).reshape(b_, s_, MLA_HEADS, QK_DIM)
    kv = (rms_norm(kv_lat, kv_norm) @ w_ukv).reshape(b_, s_, MLA_HEADS, QK_NOPE + V_DIM)
    k_nope, v = kv[..., :QK_NOPE], kv[..., QK_NOPE:]
    k = jnp.concatenate([k_nope, jnp.broadcast_to(k_rope[:, :, None, :], (b_, s_, MLA_HEADS, QK_ROPE))], axis=-1)
    q = rms_norm(q, qn_q)
    k = rms_norm(k, qn_k)
    cos, sin = rope_tables(s_)
    q = apply_rope(q, cos, sin)
    k = apply_rope(k, cos, sin)
    attn = block_attention(q, k, v)
    return jnp.concatenate([rg_out, attn], axis=-1) @ w_out


def mixer_c(h, w_in, conv_w, w_out):
    z = h @ w_in
    bg, cg, xin = jnp.split(z, 3, axis=-1)
    y = bg * depthwise_conv(cg * xin, conv_w, (1, 1))
    return y @ w_out


def peer(h, wq, k1, k2, u, v):
    b_, s_, d = h.shape
    t = b_ * s_
    hf = h.reshape(t, d)
    q = (hf @ wq).reshape(t, PEER_HEADS, D_KEY).astype(jnp.float32)
    half = D_KEY // 2
    s1 = jnp.einsum('thd,nd->thn', q[..., :half], k1.astype(jnp.float32))
    s2 = jnp.einsum('thd,nd->thn', q[..., half:], k2.astype(jnp.float32))
    v1, i1 = lax.top_k(s1, PEER_TOPK)
    v2, i2 = lax.top_k(s2, PEER_TOPK)
    cand = (v1[..., :, None] + v2[..., None, :]).reshape(t, PEER_HEADS, PEER_TOPK * PEER_TOPK)
    vs, j = lax.top_k(cand, PEER_TOPK)
    e1 = jnp.take_along_axis(i1, j // PEER_TOPK, axis=-1)
    e2 = jnp.take_along_axis(i2, j % PEER_TOPK, axis=-1)
    idx = e1 * N_KEYS + e2
    g = jax.nn.softmax(vs, axis=-1).astype(h.dtype)
    nc = t // PEER_CHUNK

    def chunk(args):
        xc, ic, gc = args
        act = jax.nn.gelu(jnp.einsum('chkd,cd->chk', u[ic], xc)) * gc
        return jnp.einsum('chk,chkd->cd', act, v[ic])

    out = lax.map(chunk, (hf.reshape(nc, PEER_CHUNK, d),
                          idx.reshape(nc, PEER_CHUNK, PEER_HEADS, PEER_TOPK),
                          g.reshape(nc, PEER_CHUNK, PEER_HEADS, PEER_TOPK)))
    return out.reshape(b_, s_, d)


def trunk(x, c, p):
    for i in range(DEPTH):
        mod = jax.nn.silu(c) @ p['ada_w'][i] + p['ada_b'][i]
        sh1, sc1, g1, sh2, sc2, g2 = jnp.split(mod[:, None, :], 6, axis=-1)
        h = rms_norm(x, p['norm1_g'][i]) * (1.0 + sc1) + sh1
        if i % 2 == 0:
            j = i // 2
            m = mixer_ab(h, p['ab_w_in'][j], p['rg_conv_w'][j], p['rg_conv_b'][j], p['rg_wa'][j], p['rg_ba'][j],
                         p['rg_wx'][j], p['rg_bx'][j], p['rg_lambda'][j], p['mla_q_norm'][j], p['mla_w_uq'][j],
                         p['mla_kv_norm'][j], p['mla_w_ukv'][j], p['mla_qn_q'][j], p['mla_qn_k'][j], p['ab_w_out'][j])
        else:
            j = i // 2
            m = mixer_c(h, p['c_w_in'][j], p['c_conv_w'][j], p['c_w_out'][j])
        x = x + g1 * m
        h = rms_norm(x, p['norm2_g'][i]) * (1.0 + sc2) + sh2
        x = x + g2 * peer(h, p['peer_wq'][i], p['peer_k1'][i], p['peer_k2'][i], p['peer_u'][i], p['peer_v'][i])
    return x


def setup_inputs(seed: int = 0) -> dict:
    key = jax.random.key(seed)
    keys = iter(jax.random.split(key, 64))
    f32 = jnp.float32

    def nrm(shape, scale):
        return jax.random.normal(next(keys), shape, f32) * scale

    def gain(shape):
        return 1.0 + nrm(shape, 0.05)

    u_lam = jax.random.uniform(next(keys), (N_EVEN, 2, D_RNN), f32, 0.9, 0.999)
    a0 = u_lam ** (1.0 / RG_C)
    rg_lambda = jnp.log(a0) - jnp.log1p(-a0)
    return {
        'x_prompt': nrm((BATCH, SEQ, D_MODEL), 1.0),
        'x_sample': nrm((DEC_BATCH, DEC_SEQ, D_MODEL), 1.0),
        'c_prompt': nrm((BATCH, D_MODEL), 1.0),
        'c_sample': nrm((DEC_BATCH, D_MODEL), 1.0),
        'ada_w': nrm((DEPTH, D_MODEL, 6 * D_MODEL), 0.02),
        'ada_b': nrm((DEPTH, 6 * D_MODEL), 0.01),
        'norm1_g': gain((DEPTH, D_MODEL)),
        'norm2_g': gain((DEPTH, D_MODEL)),
        'ab_w_in': nrm((N_EVEN, D_MODEL, AB_IN), D_MODEL ** -0.5),
        'rg_conv_w': nrm((N_EVEN, RG_CONV, D_RNN), RG_CONV ** -0.5),
        'rg_conv_b': nrm((N_EVEN, D_RNN), 0.01),
        'rg_wa': nrm((N_EVEN, 2, RG_HEADS, RG_HEAD_DIM, RG_HEAD_DIM), RG_HEAD_DIM ** -0.5),
        'rg_ba': nrm((N_EVEN, 2, D_RNN), 0.01),
        'rg_wx': nrm((N_EVEN, 2, RG_HEADS, RG_HEAD_DIM, RG_HEAD_DIM), RG_HEAD_DIM ** -0.5),
        'rg_bx': nrm((N_EVEN, 2, D_RNN), 0.01),
        'rg_lambda': rg_lambda,
        'mla_q_norm': gain((N_EVEN, Q_LORA)),
        'mla_w_uq': nrm((N_EVEN, Q_LORA, MLA_HEADS * QK_DIM), Q_LORA ** -0.5),
        'mla_kv_norm': gain((N_EVEN, KV_LORA)),
        'mla_w_ukv': nrm((N_EVEN, KV_LORA, MLA_HEADS * (QK_NOPE + V_DIM)), KV_LORA ** -0.5),
        'mla_qn_q': gain((N_EVEN, QK_DIM)),
        'mla_qn_k': gain((N_EVEN, QK_DIM)),
        'ab_w_out': nrm((N_EVEN, AB_OUT, D_MODEL), AB_OUT ** -0.5),
        'c_w_in': nrm((N_ODD, D_MODEL, 3 * D_CONV), D_MODEL ** -0.5),
        'c_conv_w': nrm((N_ODD, C_CONV, D_CONV), C_CONV ** -0.5),
        'c_w_out': nrm((N_ODD, D_CONV, D_MODEL), D_CONV ** -0.5),
        'peer_wq': nrm((DEPTH, D_MODEL, PEER_HEADS * D_KEY), D_MODEL ** -0.5),
        'peer_k1': nrm((DEPTH, N_KEYS, D_KEY // 2), (D_KEY // 2) ** -0.5),
        'peer_k2': nrm((DEPTH, N_KEYS, D_KEY // 2), (D_KEY // 2) ** -0.5),
        'peer_u': nrm((DEPTH, N_EXPERTS, D_MODEL), D_MODEL ** -0.5),
        'peer_v': nrm((DEPTH, N_EXPERTS, D_MODEL), PEER_HEADS ** -0.5),
    }


def reference(x_prompt, x_sample, c_prompt, c_sample, ada_w, ada_b, norm1_g, norm2_g, ab_w_in, rg_conv_w, rg_conv_b,
              rg_wa, rg_ba, rg_wx, rg_bx, rg_lambda, mla_q_norm, mla_w_uq, mla_kv_norm, mla_w_ukv, mla_qn_q, mla_qn_k,
              ab_w_out, c_w_in, c_conv_w, c_w_out, peer_wq, peer_k1, peer_k2, peer_u, peer_v):
    p = dict(ada_w=ada_w, ada_b=ada_b, norm1_g=norm1_g, norm2_g=norm2_g, ab_w_in=ab_w_in, rg_conv_w=rg_conv_w,
             rg_conv_b=rg_conv_b, rg_wa=rg_wa, rg_ba=rg_ba, rg_wx=rg_wx, rg_bx=rg_bx, rg_lambda=rg_lambda,
             mla_q_norm=mla_q_norm, mla_w_uq=mla_w_uq, mla_kv_norm=mla_kv_norm, mla_w_ukv=mla_w_ukv,
             mla_qn_q=mla_qn_q, mla_qn_k=mla_qn_k, ab_w_out=ab_w_out, c_w_in=c_w_in, c_conv_w=c_conv_w,
             c_w_out=c_w_out, peer_wq=peer_wq, peer_k1=peer_k1, peer_k2=peer_k2, peer_u=peer_u, peer_v=peer_v)
    y_prompt = trunk(x_prompt, c_prompt, p)
    y_sample = trunk(x_sample, c_sample, p)
    return (y_prompt, y_sample)
```

```python
import functools

import jax
import jax.numpy as jnp
from jax import lax
from jax.experimental import pallas as pl
from jax.experimental.pallas import tpu as pltpu

F32 = jnp.float32
BF16 = jnp.bfloat16
EPS = 1e-6

RG_HEADS = 8
RG_CONV = 4
RG_C = 8.0
MLA_HEADS = 8
QK_NOPE = 64
QK_ROPE = 32
V_DIM = 64
QK_DIM = QK_NOPE + QK_ROPE
ROPE_THETA = 10000.0
PEER_HEADS = 8
PEER_TOPK = 16
HEAD_SLAB = 128

LANES = 128
SUBLANES = 8
VMEM_LIMIT = 56 * 1024 * 1024

ROW_TILE = 512
SCAN_CHUNK = 1024
ATTN_TQ = 512
ATTN_TK = 1024
ROUTE_TILE = 256
PEER_TB = 16


def _cparams(sem):
    return pltpu.CompilerParams(dimension_semantics=sem, vmem_limit_bytes=VMEM_LIMIT)


def _norm_mod(x, gain, sc, sh):
    ms = jnp.mean(x * x, axis=-1, keepdims=True)
    return x * lax.rsqrt(ms + EPS) * gain * (1.0 + sc) + sh


def _rms(x, gain, n):
    ms = jnp.sum(x * x, axis=-1, keepdims=True) * (1.0 / n)
    return x * lax.rsqrt(ms + EPS) * gain


def _bdot(a, b):
    return jnp.dot(a.astype(BF16), b, preferred_element_type=F32)


def _mod_kernel(c_ref, w_ref, b_ref, o_ref):
    c = c_ref[...]
    s = c * jax.nn.sigmoid(c)
    o_ref[0] = _bdot(s, w_ref[0].astype(BF16)) + b_ref[0]


def _modulation(c_all, ada_w, ada_b):
    depth, d, n = ada_w.shape
    rows = c_all.shape[0]
    tn = 1536
    return pl.pallas_call(
        _mod_kernel,
        grid=(depth, n // tn),
        in_specs=[
            pl.BlockSpec((rows, d), lambda i, j: (0, 0)),
            pl.BlockSpec((1, d, tn), lambda i, j: (i, 0, j)),
            pl.BlockSpec((1, 1, tn), lambda i, j: (i, 0, j)),
        ],
        out_specs=pl.BlockSpec((1, rows, tn), lambda i, j: (i, 0, j)),
        out_shape=jax.ShapeDtypeStruct((depth, rows, n), F32),
        compiler_params=_cparams(("parallel", "parallel")),
        name="modulation",
    )(c_all, ada_w, ada_b.reshape(depth, 1, n))


def _in_proj_kernel(x_ref, g_ref, sc_ref, sh_ref, w_ref, z_ref):
    h = _norm_mod(x_ref[0], g_ref[...], sc_ref[0], sh_ref[0])
    z_ref[0] = _bdot(h, w_ref[...])


def _in_proj(x, gain, sc, sh, w):
    b, s, d = x.shape
    n = w.shape[1]
    tm = min(ROW_TILE, s)
    return pl.pallas_call(
        _in_proj_kernel,
        grid=(b, s // tm),
        in_specs=[
            pl.BlockSpec((1, tm, d), lambda i, j: (i, j, 0)),
            pl.BlockSpec((1, d), lambda i, j: (0, 0)),
            pl.BlockSpec((1, 1, d), lambda i, j: (i, 0, 0)),
            pl.BlockSpec((1, 1, d), lambda i, j: (i, 0, 0)),
            pl.BlockSpec((d, n), lambda i, j: (0, 0)),
        ],
        out_specs=pl.BlockSpec((1, tm, n), lambda i, j: (i, j, 0)),
        out_shape=jax.ShapeDtypeStruct((b, s, n), F32),
        compiler_params=_cparams(("parallel", "parallel")),
        name="in_proj",
    )(x, gain, sc, sh, w)


def _rglru_kernel(xf_ref, xfp_ref, xfn_ref, xb_ref, xbp_ref, xbn_ref, cw_ref, cb_ref, wg_ref, bg_ref, cl_ref,
                  hf_ref, hb_ref, ext_ref, a_ref, b_ref, carry_ref):
    j = pl.program_id(1)
    nc = pl.num_programs(1)
    tc = xf_ref.shape[1]
    dr = xf_ref.shape[2]
    nt = tc // SUBLANES

    @pl.when(j == 0)
    def _():
        carry_ref[...] = jnp.zeros_like(carry_ref)

    def gates(x_ref, xp_ref, xn_ref, first, last, d):
        ext_ref[0:SUBLANES] = jnp.where(first, 0.0, xp_ref[0])
        ext_ref[SUBLANES:SUBLANES + tc] = x_ref[0]
        ext_ref[SUBLANES + tc:2 * SUBLANES + tc] = jnp.where(last, 0.0, xn_ref[0])
        xc = cb_ref[...]
        for k in range(RG_CONV):
            xc = xc + cw_ref[k:k + 1, :] * ext_ref[SUBLANES - 2 + k:SUBLANES - 2 + k + tc]
        g = _bdot(xc, wg_ref[d]) + bg_ref[d]
        r = jax.nn.sigmoid(g[:, :dr])
        i = jax.nn.sigmoid(g[:, dr:])
        log_a = r * cl_ref[d]
        a = jnp.exp(log_a)
        b = jnp.sqrt(-jnp.tanh(log_a) * (1.0 + a * a)) * (i * xc)
        a_ref[d] = a
        b_ref[d] = b

    rows = lax.broadcasted_iota(jnp.int32, (SUBLANES, dr), 0)

    def scan(d, reverse, out_ref):
        def body(it, carry):
            t = (nt - 1 - it) if reverse else it
            off = pl.multiple_of(t * SUBLANES, SUBLANES)
            a = a_ref[d, pl.ds(off, SUBLANES), :]
            b = b_ref[d, pl.ds(off, SUBLANES), :]
            for s in (1, 2, 4):
                if reverse:
                    a_s = pltpu.roll(a, SUBLANES - s, 0)
                    b_s = pltpu.roll(b, SUBLANES - s, 0)
                    m = rows < SUBLANES - s
                else:
                    a_s = pltpu.roll(a, s, 0)
                    b_s = pltpu.roll(b, s, 0)
                    m = rows >= s
                b = jnp.where(m, a * b_s + b, b)
                a = jnp.where(m, a * a_s, a)
            h = b + a * carry
            out_ref[0, pl.ds(off, SUBLANES), :] = h
            edge = h[0:1] if reverse else h[SUBLANES - 1:SUBLANES]
            return jnp.broadcast_to(edge, (SUBLANES, dr))

        carry_ref[d] = lax.fori_loop(0, nt, body, carry_ref[d])

    gates(xf_ref, xfp_ref, xfn_ref, j == 0, j == nc - 1, 0)
    scan(0, False, hf_ref)
    gates(xb_ref, xbp_ref, xbn_ref, j == nc - 1, j == 0, 1)
    scan(1, True, hb_ref)


def _rglru(z, conv_w, conv_b, wg, bg, cl):
    b, s, _ = z.shape
    dr = conv_w.shape[1]
    tc = min(SCAN_CHUNK, s)
    nc = s // tc
    nb8 = s // SUBLANES
    cb8 = tc // SUBLANES

    def main(rev):
        return pl.BlockSpec((1, tc, dr), (lambda i, j: (i, nc - 1 - j, 0)) if rev else (lambda i, j: (i, j, 0)))

    def prev(rev):
        def f(i, j):
            c = (nc - 1 - j) if rev else j
            return (i, jnp.maximum(c * cb8 - 1, 0), 0)
        return pl.BlockSpec((1, SUBLANES, dr), f)

    def nxt(rev):
        def f(i, j):
            c = (nc - 1 - j) if rev else j
            return (i, jnp.minimum((c + 1) * cb8, nb8 - 1), 0)
        return pl.BlockSpec((1, SUBLANES, dr), f)

    def const(shape):
        return pl.BlockSpec(shape, lambda i, j: (0,) * len(shape))

    out_sd = jax.ShapeDtypeStruct((b, s, dr), F32)
    return pl.pallas_call(
        _rglru_kernel,
        grid=(b, nc),
        in_specs=[main(False), prev(False), nxt(False), main(True), prev(True), nxt(True),
                  const((RG_CONV, dr)), const((1, dr)), const((2, dr, 2 * dr)), const((2, 1, 2 * dr)),
                  const((2, 1, dr))],
        out_specs=[pl.BlockSpec((1, tc, dr), lambda i, j: (i, j, 0)),
                   pl.BlockSpec((1, tc, dr), lambda i, j: (i, nc - 1 - j, 0))],
        out_shape=[out_sd, out_sd],
        scratch_shapes=[pltpu.VMEM((tc + 2 * SUBLANES, dr), F32), pltpu.VMEM((2, tc, dr), F32),
                        pltpu.VMEM((2, tc, dr), F32), pltpu.VMEM((2, SUBLANES, dr), F32)],
        compiler_params=_cparams(("parallel", "arbitrary")),
        name="rglru",
    )(z, z, z, z, z, z, conv_w, conv_b, wg, bg, cl)


def _mla_proj_kernel(z_ref, qn_ref, kvn_ref, wuq_ref, wk_ref, wv_ref, gq_ref, gk_ref, rc_ref, r1_ref, r2_ref,
                     q_ref, k_ref, v_ref):
    zz = z_ref[0]
    q_lora = qn_ref.shape[1]
    kv_lora = kvn_ref.shape[1]
    ql = zz[:, :q_lora]
    kvl = zz[:, q_lora:q_lora + kv_lora]
    kr = zz[:, q_lora + kv_lora:]
    q = _bdot(_rms(ql, qn_ref[...], q_lora), wuq_ref[...])
    kvn = _rms(kvl, kvn_ref[...], kv_lora)
    kk = _bdot(kvn, wk_ref[...])
    vv = _bdot(kvn, wv_ref[...])
    rc, r1, r2 = rc_ref[...], r1_ref[...], r2_ref[...]
    half = QK_ROPE // 2
    scale = QK_DIM ** -0.5

    def norm_rope(xh, g):
        xh = _rms(xh, g, QK_DIM)
        return xh * rc + pltpu.roll(xh, HEAD_SLAB - half, 1) * r1 + pltpu.roll(xh, half, 1) * r2

    for h in range(MLA_HEADS):
        sl = slice(h * HEAD_SLAB, (h + 1) * HEAD_SLAB)
        q_ref[0, h] = (norm_rope(q[:, sl], gq_ref[...]) * scale).astype(BF16)
        k_ref[0, h] = norm_rope(kk[:, sl] + kr, gk_ref[...]).astype(BF16)
    for p in range(MLA_HEADS // 2):
        v_ref[0, p] = vv[:, p * LANES:(p + 1) * LANES].astype(BF16)


def _mla_proj(z, col_block, qn, kvn, wuq, wk, wv, gq, gk, rc, r1, r2):
    b, s, _ = z.shape
    tm = min(ROW_TILE, s)
    width = qn.shape[1] + kvn.shape[1] + HEAD_SLAB

    def const(a):
        return pl.BlockSpec(a.shape, lambda i, j: (0,) * a.ndim)

    def rope(a):
        return pl.BlockSpec((tm, HEAD_SLAB), lambda i, j: (j, 0))

    hp = MLA_HEADS // 2
    return pl.pallas_call(
        _mla_proj_kernel,
        grid=(b, s // tm),
        in_specs=[pl.BlockSpec((1, tm, width), lambda i, j: (i, j, col_block)),
                  const(qn), const(kvn), const(wuq), const(wk), const(wv), const(gq), const(gk),
                  rope(rc), rope(r1), rope(r2)],
        out_specs=[pl.BlockSpec((1, MLA_HEADS, tm, HEAD_SLAB), lambda i, j: (i, 0, j, 0)),
                   pl.BlockSpec((1, MLA_HEADS, tm, HEAD_SLAB), lambda i, j: (i, 0, j, 0)),
                   pl.BlockSpec((1, hp, tm, LANES), lambda i, j: (i, 0, j, 0))],
        out_shape=[jax.ShapeDtypeStruct((b, MLA_HEADS, s, HEAD_SLAB), BF16),
                   jax.ShapeDtypeStruct((b, MLA_HEADS, s, HEAD_SLAB), BF16),
                   jax.ShapeDtypeStruct((b, hp, s, LANES), BF16)],
        compiler_params=_cparams(("parallel", "parallel")),
        name="mla_proj",
    )(z, qn, kvn, wuq, wk, wv, gq, gk, rc, r1, r2)


def _flash_kernel(q_ref, k_ref, v_ref, o_ref, m_ref, l_ref, acc_ref):
    ik = pl.program_id(3)

    @pl.when(ik == 0)
    def _():
        m_ref[...] = jnp.full_like(m_ref, -jnp.inf)
        l_ref[...] = jnp.zeros_like(l_ref)
        acc_ref[...] = jnp.zeros_like(acc_ref)

    v = v_ref[0, 0]
    for hh in range(2):
        s = lax.dot_general(q_ref[0, hh], k_ref[0, hh], (((1,), (1,)), ((), ())), preferred_element_type=F32)
        m_prev = m_ref[hh]
        m_new = jnp.maximum(m_prev, jnp.max(s, axis=-1, keepdims=True))
        alpha = jnp.exp(m_prev - m_new)
        p = jnp.exp(s - m_new[:, :1])
        l_ref[hh] = alpha * l_ref[hh] + jnp.sum(p, axis=-1, keepdims=True)
        acc_ref[hh] = alpha * acc_ref[hh] + jnp.dot(p.astype(BF16), v, preferred_element_type=F32)
        m_ref[hh] = m_new

    @pl.when(ik == pl.num_programs(3) - 1)
    def _():
        lane = lax.broadcasted_iota(jnp.int32, acc_ref.shape[1:], 1)
        o_ref[0] = jnp.where(lane < V_DIM, acc_ref[0] / l_ref[0], acc_ref[1] / l_ref[1])


def _attention(q, k, v):
    b, h, s, _ = q.shape
    tq = min(ATTN_TQ, s)
    tk = min(ATTN_TK, s)
    hp = h // 2
    return pl.pallas_call(
        _flash_kernel,
        grid=(b, hp, s // tq, s // tk),
        in_specs=[pl.BlockSpec((1, 2, tq, HEAD_SLAB), lambda i, p, a, c: (i, p, a, 0)),
                  pl.BlockSpec((1, 2, tk, HEAD_SLAB), lambda i, p, a, c: (i, p, c, 0)),
                  pl.BlockSpec((1, 1, tk, LANES), lambda i, p, a, c: (i, p, c, 0))],
        out_specs=pl.BlockSpec((1, tq, LANES), lambda i, p, a, c: (i, a, p)),
        out_shape=jax.ShapeDtypeStruct((b, s, hp * LANES), F32),
        scratch_shapes=[pltpu.VMEM((2, tq, LANES), F32)] * 3,
        compiler_params=_cparams(("parallel", "parallel", "parallel", "arbitrary")),
        name="attention",
    )(q, k, v)


def _ab_out_kernel(x_ref, hf_ref, hb_ref, y_ref, at_ref, w_ref, g_ref, o_ref):
    rg = (hf_ref[0] + hb_ref[0]) * jax.nn.gelu(y_ref[0])
    cat = jnp.concatenate([rg.astype(BF16), at_ref[0].astype(BF16)], axis=-1)
    m = jnp.dot(cat, w_ref[...], preferred_element_type=F32)
    o_ref[0] = x_ref[0] + g_ref[0] * m


def _ab_out(x, hf, hb, z, attn, w, gate):
    b, s, d = x.shape
    dr = hf.shape[2]
    da = attn.shape[2]
    tm = min(ROW_TILE, s)
    return pl.pallas_call(
        _ab_out_kernel,
        grid=(b, s // tm),
        in_specs=[pl.BlockSpec((1, tm, d), lambda i, j: (i, j, 0)),
                  pl.BlockSpec((1, tm, dr), lambda i, j: (i, j, 0)),
                  pl.BlockSpec((1, tm, dr), lambda i, j: (i, j, 0)),
                  pl.BlockSpec((1, tm, dr), lambda i, j: (i, j, 1)),
                  pl.BlockSpec((1, tm, da), lambda i, j: (i, j, 0)),
                  pl.BlockSpec(w.shape, lambda i, j: (0, 0)),
                  pl.BlockSpec((1, 1, d), lambda i, j: (i, 0, 0))],
        out_specs=pl.BlockSpec((1, tm, d), lambda i, j: (i, j, 0)),
        out_shape=jax.ShapeDtypeStruct((b, s, d), F32),
        compiler_params=_cparams(("parallel", "parallel")),
        name="ab_out",
    )(x, hf, hb, z, attn, w, gate)


def _c_in_kernel(x_ref, g_ref, sc_ref, sh_ref, w_ref, bg_ref, cx_ref):
    h = _norm_mod(x_ref[0], g_ref[...], sc_ref[0], sh_ref[0])
    z = _bdot(h, w_ref[...])
    dc = bg_ref.shape[2]
    bg_ref[0] = z[:, :dc]
    cx_ref[0] = z[:, dc:2 * dc] * z[:, 2 * dc:]


def _c_in(x, gain, sc, sh, w):
    b, s, d = x.shape
    dc = w.shape[1] // 3
    tm = min(ROW_TILE, s)
    sd = jax.ShapeDtypeStruct((b, s, dc), F32)
    return pl.pallas_call(
        _c_in_kernel,
        grid=(b, s // tm),
        in_specs=[pl.BlockSpec((1, tm, d), lambda i, j: (i, j, 0)),
                  pl.BlockSpec((1, d), lambda i, j: (0, 0)),
                  pl.BlockSpec((1, 1, d), lambda i, j: (i, 0, 0)),
                  pl.BlockSpec((1, 1, d), lambda i, j: (i, 0, 0)),
                  pl.BlockSpec(w.shape, lambda i, j: (0, 0))],
        out_specs=[pl.BlockSpec((1, tm, dc), lambda i, j: (i, j, 0))] * 2,
        out_shape=[sd, sd],
        compiler_params=_cparams(("parallel", "parallel")),
        name="c_in",
    )(x, gain, sc, sh, w)


def _c_out_kernel(x_ref, bg_ref, cx_ref, cp_ref, cn_ref, cw_ref, w_ref, g_ref, o_ref, ext_ref):
    j = pl.program_id(1)
    tm = cx_ref.shape[1]
    ext_ref[0:SUBLANES] = jnp.where(j == 0, 0.0, cp_ref[0])
    ext_ref[SUBLANES:SUBLANES + tm] = cx_ref[0]
    ext_ref[SUBLANES + tm:2 * SUBLANES + tm] = jnp.where(j == pl.num_programs(1) - 1, 0.0, cn_ref[0])
    conv = cw_ref[0:1, :] * ext_ref[SUBLANES - 1:SUBLANES - 1 + tm]
    conv = conv + cw_ref[1:2, :] * ext_ref[SUBLANES:SUBLANES + tm]
    conv = conv + cw_ref[2:3, :] * ext_ref[SUBLANES + 1:SUBLANES + 1 + tm]
    m = _bdot(bg_ref[0] * conv, w_ref[...])
    o_ref[0] = x_ref[0] + g_ref[0] * m


def _c_out(x, bg, cx, conv_w, w, gate):
    b, s, d = x.shape
    dc = bg.shape[2]
    tm = min(ROW_TILE, s)
    nb8 = s // SUBLANES
    t8 = tm // SUBLANES
    return pl.pallas_call(
        _c_out_kernel,
        grid=(b, s // tm),
        in_specs=[pl.BlockSpec((1, tm, d), lambda i, j: (i, j, 0)),
                  pl.BlockSpec((1, tm, dc), lambda i, j: (i, j, 0)),
                  pl.BlockSpec((1, tm, dc), lambda i, j: (i, j, 0)),
                  pl.BlockSpec((1, SUBLANES, dc), lambda i, j: (i, jnp.maximum(j * t8 - 1, 0), 0)),
                  pl.BlockSpec((1, SUBLANES, dc), lambda i, j: (i, jnp.minimum((j + 1) * t8, nb8 - 1), 0)),
                  pl.BlockSpec(conv_w.shape, lambda i, j: (0, 0)),
                  pl.BlockSpec(w.shape, lambda i, j: (0, 0)),
                  pl.BlockSpec((1, 1, d), lambda i, j: (i, 0, 0))],
        out_specs=pl.BlockSpec((1, tm, d), lambda i, j: (i, j, 0)),
        out_shape=jax.ShapeDtypeStruct((b, s, d), F32),
        scratch_shapes=[pltpu.VMEM((tm + 2 * SUBLANES, dc), F32)],
        compiler_params=_cparams(("parallel", "parallel")),
        name="c_out",
    )(x, bg, cx, cx, cx, conv_w, w, gate)


def _top_rows(s, k, payload=None):
    n = s.shape[0]
    ridx = lax.broadcasted_iota(jnp.int32, s.shape, 0)
    vals, ids = [], []
    for _ in range(k):
        m = jnp.max(s, axis=0, keepdims=True)
        first = jnp.min(jnp.where(s == m, ridx, n), axis=0, keepdims=True)
        sel = ridx == first
        ids.append(first if payload is None else jnp.sum(jnp.where(sel, payload, 0), axis=0, keepdims=True))
        vals.append(m)
        s = jnp.where(sel, -jnp.inf, s)
    return vals, ids


_PAIRS = [(a, b) for a in range(PEER_TOPK) for b in range(PEER_TOPK) if (a + 1) * (b + 1) <= PEER_TOPK]


def _route_kernel(x_ref, g_ref, sc_ref, sh_ref, wq_ref, k1_ref, k2_ref, h_ref, idx_ref, gate_ref, q_scr):
    h = _norm_mod(x_ref[0], g_ref[...], sc_ref[0], sh_ref[0])
    h_ref[0] = h
    q_scr[...] = _bdot(h, wq_ref[...])
    n_keys = k1_ref.shape[0]
    half = k1_ref.shape[1]
    tm = x_ref.shape[1]
    nt = (((1,), (1,)), ((), ()))
    pad = -len(_PAIRS) % SUBLANES

    def head(hd, carry):
        off = pl.multiple_of(hd * 2 * half, 2 * half)
        q1 = q_scr[:, pl.ds(off, half)].astype(BF16)
        q2 = q_scr[:, pl.ds(off + half, half)].astype(BF16)
        s1 = lax.dot_general(k1_ref[...], q1, nt, preferred_element_type=F32)
        s2 = lax.dot_general(k2_ref[...], q2, nt, preferred_element_type=F32)
        v1, i1 = _top_rows(s1, PEER_TOPK)
        v2, i2 = _top_rows(s2, PEER_TOPK)
        cv = [v1[a] + v2[b] for a, b in _PAIRS] + [jnp.full((pad, tm), -jnp.inf, F32)]
        ce = [i1[a] * n_keys + i2[b] for a, b in _PAIRS] + [jnp.zeros((pad, tm), jnp.int32)]
        vs, es = _top_rows(jnp.concatenate(cv, axis=0), PEER_TOPK, jnp.concatenate(ce, axis=0))
        vs = jnp.concatenate(vs, axis=0)
        e = jnp.exp(vs - vs[0:1])
        row = pl.multiple_of(hd * PEER_TOPK, PEER_TOPK)
        gate_ref[0, pl.ds(row, PEER_TOPK), :] = e / jnp.sum(e, axis=0, keepdims=True)
        idx_ref[0, pl.ds(row, PEER_TOPK), :] = jnp.concatenate(es, axis=0)
        return carry

    lax.fori_loop(0, PEER_HEADS, head, 0)


def _route(x, gain, sc, sh, wq, k1, k2):
    b, s, d = x.shape
    tm = min(ROUTE_TILE, s)
    nsel = PEER_HEADS * PEER_TOPK
    return pl.pallas_call(
        _route_kernel,
        grid=(b, s // tm),
        in_specs=[pl.BlockSpec((1, tm, d), lambda i, j: (i, j, 0)),
                  pl.BlockSpec((1, d), lambda i, j: (0, 0)),
                  pl.BlockSpec((1, 1, d), lambda i, j: (i, 0, 0)),
                  pl.BlockSpec((1, 1, d), lambda i, j: (i, 0, 0)),
                  pl.BlockSpec(wq.shape, lambda i, j: (0, 0)),
                  pl.BlockSpec(k1.shape, lambda i, j: (0, 0)),
                  pl.BlockSpec(k2.shape, lambda i, j: (0, 0))],
        out_specs=[pl.BlockSpec((1, tm, d), lambda i, j: (i, j, 0)),
                   pl.BlockSpec((1, nsel, tm), lambda i, j: (i, 0, j)),
                   pl.BlockSpec((1, nsel, tm), lambda i, j: (i, 0, j))],
        out_shape=[jax.ShapeDtypeStruct((b, s, d), F32),
                   jax.ShapeDtypeStruct((b, nsel, s), jnp.int32),
                   jax.ShapeDtypeStruct((b, nsel, s), F32)],
        scratch_shapes=[pltpu.VMEM((tm, wq.shape[1]), F32)],
        compiler_params=_cparams(("parallel", "parallel")),
        name="peer_route",
    )(x, gain, sc, sh, wq, k1, k2)


def _pack_table(t):
    n, d = t.shape
    bits = lax.bitcast_convert_type(t.astype(BF16), jnp.uint16).astype(jnp.uint32)
    words = bits[:, :d // 2] | (bits[:, d // 2:] << 16)
    return lax.bitcast_convert_type(words, jnp.int32).reshape(n * (d // 2 // LANES), LANES)


def _unpack(w):
    lo = lax.bitcast_convert_type(w << 16, F32)
    hi = lax.bitcast_convert_type(w & jnp.int32(-65536), F32)
    return lo, hi


def _peer_u_kernel(idx_ref, x_ref, g_ref, tbl_ref, act_ref, p_ref, r_ref):
    tb, nsel = act_ref.shape
    wr = x_ref.shape[1] // 2

    def tok(t, carry):
        xv = x_ref[t]
        xlo, xhi = xv[:wr], xv[wr:]
        base = pl.multiple_of(t * (nsel * wr), nsel * wr)
        for r in range(nsel):
            lo, hi = _unpack(tbl_ref[pl.ds(pl.multiple_of(idx_ref[t, r], wr), wr), :])
            p_ref[pl.ds(base + r * wr, wr), :] = lo * xlo + hi * xhi
        return carry

    lax.fori_loop(0, tb, tok, 0)
    p4 = p_ref[pl.ds(0, tb * nsel, stride=wr), :]
    for k in range(1, wr):
        p4 = p4 + p_ref[pl.ds(k, tb * nsel, stride=wr), :]
    hi = p4.astype(BF16)
    lo = (p4 - hi.astype(F32)).astype(BF16)
    ones = jnp.ones((LANES, LANES), BF16)
    r_ref[...] = jnp.dot(hi, ones, preferred_element_type=F32) + jnp.dot(lo, ones, preferred_element_type=F32)
    eye = (lax.broadcasted_iota(jnp.int32, (nsel, LANES), 0) == lax.broadcasted_iota(jnp.int32, (nsel, LANES), 1))

    def fin(t, carry):
        s = r_ref[pl.ds(pl.multiple_of(t * nsel, nsel), nsel), :]
        a = jax.nn.gelu(s) * g_ref[pl.ds(t, 1), :]
        act_ref[pl.ds(t, 1), :] = jnp.sum(jnp.where(eye, a, 0.0), axis=0, keepdims=True)
        return carry

    lax.fori_loop(0, tb, fin, 0)


def _peer_u(idx, x, gates, table):
    t, nsel = idx.shape
    rows = x.shape[1]
    tb = min(PEER_TB, t)
    return pl.pallas_call(
        _peer_u_kernel,
        grid=(t // tb,),
        in_specs=[pl.BlockSpec((tb, nsel), lambda i: (i, 0), memory_space=pltpu.SMEM),
                  pl.BlockSpec((tb, rows, LANES), lambda i: (i, 0, 0)),
                  pl.BlockSpec((tb, nsel), lambda i: (i, 0)),
                  pl.BlockSpec(table.shape, lambda i: (0, 0), pipeline_mode=pl.Buffered(1))],
        out_specs=pl.BlockSpec((tb, nsel), lambda i: (i, 0)),
        out_shape=jax.ShapeDtypeStruct((t, nsel), F32),
        scratch_shapes=[pltpu.VMEM((tb * nsel * rows // 2, LANES), F32), pltpu.VMEM((tb * nsel, LANES), F32)],
        compiler_params=_cparams(("parallel",)),
        name="peer_u",
    )(idx, x, gates, table)


def _peer_v_kernel(idx_ref, act_ref, tbl_ref, x_ref, g_ref, o_ref):
    tb, nsel = idx_ref.shape
    wr = x_ref.shape[1] // 2
    nacc = 4

    def tok(t, carry):
        acc_lo = [jnp.zeros((wr, LANES), F32) for _ in range(nacc)]
        acc_hi = [jnp.zeros((wr, LANES), F32) for _ in range(nacc)]
        for r in range(nsel):
            lo, hi = _unpack(tbl_ref[pl.ds(pl.multiple_of(idx_ref[t, r], wr), wr), :])
            a = act_ref[t, r]
            acc_lo[r % nacc] = acc_lo[r % nacc] + a * lo
            acc_hi[r % nacc] = acc_hi[r % nacc] + a * hi
        out = jnp.concatenate([sum(acc_lo[1:], acc_lo[0]), sum(acc_hi[1:], acc_hi[0])], axis=0)
        o_ref[t] = x_ref[t] + g_ref[0] * out
        return carry

    lax.fori_loop(0, tb, tok, 0)


def _peer_v(idx, act, table, x, gate, tokens_per_batch):
    t, nsel = idx.shape
    rows = x.shape[1]
    tb = min(PEER_TB, t)
    return pl.pallas_call(
        _peer_v_kernel,
        grid=(t // tb,),
        in_specs=[pl.BlockSpec((tb, nsel), lambda i: (i, 0), memory_space=pltpu.SMEM),
                  pl.BlockSpec((tb, nsel), lambda i: (i, 0), memory_space=pltpu.SMEM),
                  pl.BlockSpec(table.shape, lambda i: (0, 0), pipeline_mode=pl.Buffered(1)),
                  pl.BlockSpec((tb, rows, LANES), lambda i: (i, 0, 0)),
                  pl.BlockSpec((1, rows, LANES), lambda i: (i * tb // tokens_per_batch, 0, 0))],
        out_specs=pl.BlockSpec((tb, rows, LANES), lambda i: (i, 0, 0)),
        out_shape=jax.ShapeDtypeStruct(x.shape, F32),
        compiler_params=_cparams(("parallel",)),
        name="peer_v",
    )(idx, act, table, x, gate)


def _peer(x, gain, sc, sh, gate, wq, k1, k2, u_tbl, v_tbl):
    b, s, d = x.shape
    rows = d // LANES
    wr = rows // 2
    h, idx, gts = _route(x, gain, sc, sh, wq, k1, k2)
    idx = jnp.transpose(idx, (0, 2, 1)).reshape(b * s, -1) * wr
    gts = jnp.transpose(gts, (0, 2, 1)).reshape(b * s, -1)
    act = _peer_u(idx, h.reshape(b * s, rows, LANES), gts, u_tbl)
    out = _peer_v(idx, act, v_tbl, x.reshape(b * s, rows, LANES), gate.reshape(b, rows, LANES), s)
    return out.reshape(b, s, d)


def _block_diag(w):
    h, i, j = w.shape
    return jnp.einsum('hij,hg->higj', w, jnp.eye(h, dtype=w.dtype)).reshape(h * i, h * j)


def _rope_tables(s):
    half = QK_ROPE // 2
    inv = 1.0 / (ROPE_THETA ** (jnp.arange(0, QK_ROPE, 2, dtype=F32) / QK_ROPE))
    ang = jnp.arange(s, dtype=F32)[:, None] * inv[None, :]
    cos, sin = jnp.cos(ang), jnp.sin(ang)
    z = jnp.zeros((s, QK_NOPE), F32)
    tail = jnp.zeros((s, HEAD_SLAB - QK_DIM), F32)
    zh = jnp.zeros((s, half), F32)
    rc = jnp.concatenate([z + 1.0, cos, cos, tail + 1.0], axis=1)
    r1 = jnp.concatenate([z, -sin, zh, tail], axis=1)
    r2 = jnp.concatenate([z, zh, sin, tail], axis=1)
    return rc, r1, r2


def _pad_last(a, n):
    return jnp.pad(a, [(0, 0)] * (a.ndim - 1) + [(0, n - a.shape[-1])])


def _trunk(x, mod, p):
    b, s, d = x.shape
    depth = p['ada_w'].shape[0]
    d_rnn = p['rg_conv_w'].shape[2]
    q_lora = p['mla_q_norm'].shape[1]
    kv_lora = p['mla_kv_norm'].shape[1]
    for i in range(depth):
        sh1, sc1, g1, sh2, sc2, g2 = [m[:, None, :] for m in jnp.split(mod[i], 6, axis=-1)]
        j = i // 2
        n1 = p['norm1_g'][i][None, :]
        if i % 2 == 0:
            w_in = p['ab_w_in'][j]
            lat = 2 * d_rnn + q_lora + kv_lora
            w0 = jnp.concatenate([w_in[:, :lat], jnp.zeros((d, QK_NOPE), F32), w_in[:, lat:],
                                  jnp.zeros((d, HEAD_SLAB - QK_DIM), F32)], axis=1).astype(BF16)
            z = _in_proj(x, n1, sc1, sh1, w0)
            wg = jnp.stack([jnp.concatenate([_block_diag(p['rg_wa'][j][k]), _block_diag(p['rg_wx'][j][k])], axis=1)
                            for k in range(2)]).astype(BF16)
            bg = jnp.concatenate([p['rg_ba'][j], p['rg_bx'][j]], axis=1)[:, None, :]
            cl = (-RG_C * jax.nn.softplus(-p['rg_lambda'][j]))[:, None, :]
            hf, hb = _rglru(z, p['rg_conv_w'][j], p['rg_conv_b'][j][None, :], wg, bg, cl)
            wuq = _pad_last(p['mla_w_uq'][j].reshape(q_lora, MLA_HEADS, QK_DIM), HEAD_SLAB)
            wuq = wuq.reshape(q_lora, MLA_HEADS * HEAD_SLAB).astype(BF16)
            wkv = p['mla_w_ukv'][j].reshape(kv_lora, MLA_HEADS, QK_NOPE + V_DIM)
            wk = _pad_last(wkv[:, :, :QK_NOPE], HEAD_SLAB).reshape(kv_lora, MLA_HEADS * HEAD_SLAB).astype(BF16)
            wv = wkv[:, :, QK_NOPE:].reshape(kv_lora, MLA_HEADS * V_DIM).astype(BF16)
            gq = _pad_last(p['mla_qn_q'][j][None, :], HEAD_SLAB)
            gk = _pad_last(p['mla_qn_k'][j][None, :], HEAD_SLAB)
            rc, r1, r2 = _rope_tables(s)
            width = q_lora + kv_lora + HEAD_SLAB
            assert (2 * d_rnn) % width == 0
            q, k, v = _mla_proj(z, 2 * d_rnn // width, p['mla_q_norm'][j][None, :], p['mla_kv_norm'][j][None, :],
                                wuq, wk, wv, gq, gk, rc, r1, r2)
            attn = _attention(q, k, v)
            x = _ab_out(x, hf, hb, z, attn, p['ab_w_out'][j].astype(BF16), g1)
        else:
            bgate, cx = _c_in(x, n1, sc1, sh1, p['c_w_in'][j].astype(BF16))
            x = _c_out(x, bgate, cx, p['c_conv_w'][j], p['c_w_out'][j].astype(BF16), g1)
        x = _peer(x, p['norm2_g'][i][None, :], sc2, sh2, g2, p['peer_wq'][i].astype(BF16),
                  p['peer_k1'][i].astype(BF16), p['peer_k2'][i].astype(BF16),
                  _pack_table(p['peer_u'][i]), _pack_table(p['peer_v'][i]))
    return x


def kernel(x_prompt, x_sample, c_prompt, c_sample, ada_w, ada_b, norm1_g, norm2_g, ab_w_in, rg_conv_w, rg_conv_b, rg_wa, rg_ba, rg_wx, rg_bx, rg_lambda, mla_q_norm, mla_w_uq, mla_kv_norm, mla_w_ukv, mla_qn_q, mla_qn_k, ab_w_out, c_w_in, c_conv_w, c_w_out, peer_wq, peer_k1, peer_k2, peer_u, peer_v):
    p = dict(ada_w=ada_w, norm1_g=norm1_g, norm2_g=norm2_g, ab_w_in=ab_w_in, rg_conv_w=rg_conv_w,
             rg_conv_b=rg_conv_b, rg_wa=rg_wa, rg_ba=rg_ba, rg_wx=rg_wx, rg_bx=rg_bx, rg_lambda=rg_lambda,
             mla_q_norm=mla_q_norm, mla_w_uq=mla_w_uq, mla_kv_norm=mla_kv_norm, mla_w_ukv=mla_w_ukv,
             mla_qn_q=mla_qn_q, mla_qn_k=mla_qn_k, ab_w_out=ab_w_out, c_w_in=c_w_in, c_conv_w=c_conv_w,
             c_w_out=c_w_out, peer_wq=peer_wq, peer_k1=peer_k1, peer_k2=peer_k2, peer_u=peer_u, peer_v=peer_v)
    bp, bs = c_prompt.shape[0], c_sample.shape[0]
    rows = -(-(bp + bs) // SUBLANES) * SUBLANES
    c_all = jnp.pad(jnp.concatenate([c_prompt, c_sample], axis=0), ((0, rows - bp - bs), (0, 0)))
    mod = _modulation(c_all, ada_w, ada_b)
    y_prompt = _trunk(x_prompt, mod[:, :bp], p)
    y_sample = _trunk(x_sample, mod[:, bp:bp + bs], p)
    return (y_prompt, y_sample)
```

```python
import functools

import jax
import jax.numpy as jnp
from jax import lax
from jax.experimental import pallas as pl
from jax.experimental.pallas import tpu as pltpu

F32 = jnp.float32
BF16 = jnp.bfloat16
EPS = 1e-6

RG_HEADS = 8
RG_CONV = 4
RG_C = 8.0
MLA_HEADS = 8
QK_NOPE = 64
QK_ROPE = 32
V_DIM = 64
QK_DIM = QK_NOPE + QK_ROPE
ROPE_THETA = 10000.0
PEER_HEADS = 8
PEER_TOPK = 16
HEAD_SLAB = 128

LANES = 128
SUBLANES = 8
VMEM_LIMIT = 56 * 1024 * 1024

ROW_TILE = 512
SCAN_CHUNK = 1024
ATTN_TQ = 512
ATTN_TK = 1024
ROUTE_TILE = 256
PEER_TB = 16


def _cparams(sem):
    return pltpu.CompilerParams(dimension_semantics=sem, vmem_limit_bytes=VMEM_LIMIT)


def _norm_mod(x, gain, sc, sh):
    ms = jnp.mean(x * x, axis=-1, keepdims=True)
    return x * lax.rsqrt(ms + EPS) * gain * (1.0 + sc) + sh


def _rms(x, gain, n):
    ms = jnp.sum(x * x, axis=-1, keepdims=True) * (1.0 / n)
    return x * lax.rsqrt(ms + EPS) * gain


def _bdot(a, b):
    return jnp.dot(a.astype(BF16), b, preferred_element_type=F32)


def _mod_kernel(c_ref, w_ref, b_ref, o_ref):
    c = c_ref[...]
    s = c * jax.nn.sigmoid(c)
    o_ref[0] = _bdot(s, w_ref[0].astype(BF16)) + b_ref[0]


def _modulation(c_all, ada_w, ada_b):
    depth, d, n = ada_w.shape
    rows = c_all.shape[0]
    tn = 1536
    return pl.pallas_call(
        _mod_kernel,
        grid=(depth, n // tn),
        in_specs=[
            pl.BlockSpec((rows, d), lambda i, j: (0, 0)),
            pl.BlockSpec((1, d, tn), lambda i, j: (i, 0, j)),
            pl.BlockSpec((1, 1, tn), lambda i, j: (i, 0, j)),
        ],
        out_specs=pl.BlockSpec((1, rows, tn), lambda i, j: (i, 0, j)),
        out_shape=jax.ShapeDtypeStruct((depth, rows, n), F32),
        compiler_params=_cparams(("parallel", "parallel")),
        name="modulation",
    )(c_all, ada_w, ada_b.reshape(depth, 1, n))


def _in_proj_kernel(x_ref, g_ref, sc_ref, sh_ref, w_ref, z_ref):
    h = _norm_mod(x_ref[0], g_ref[...], sc_ref[0], sh_ref[0])
    z_ref[0] = _bdot(h, w_ref[...])


def _in_proj(x, gain, sc, sh, w):
    b, s, d = x.shape
    n = w.shape[1]
    tm = min(ROW_TILE, s)
    return pl.pallas_call(
        _in_proj_kernel,
        grid=(b, s // tm),
        in_specs=[
            pl.BlockSpec((1, tm, d), lambda i, j: (i, j, 0)),
            pl.BlockSpec((1, d), lambda i, j: (0, 0)),
            pl.BlockSpec((1, 1, d), lambda i, j: (i, 0, 0)),
            pl.BlockSpec((1, 1, d), lambda i, j: (i, 0, 0)),
            pl.BlockSpec((d, n), lambda i, j: (0, 0)),
        ],
        out_specs=pl.BlockSpec((1, tm, n), lambda i, j: (i, j, 0)),
        out_shape=jax.ShapeDtypeStruct((b, s, n), F32),
        compiler_params=_cparams(("parallel", "parallel")),
        name="in_proj",
    )(x, gain, sc, sh, w)


def _rglru_kernel(xf_ref, xfp_ref, xfn_ref, xb_ref, xbp_ref, xbn_ref, cw_ref, cb_ref, wg_ref, bg_ref, cl_ref,
                  hf_ref, hb_ref, ext_ref, a_ref, b_ref, carry_ref):
    j = pl.program_id(1)
    nc = pl.num_programs(1)
    tc = xf_ref.shape[1]
    dr = xf_ref.shape[2]
    nt = tc // SUBLANES

    @pl.when(j == 0)
    def _():
        carry_ref[...] = jnp.zeros_like(carry_ref)

    def gates(x_ref, xp_ref, xn_ref, first, last, d):
        ext_ref[0:SUBLANES] = jnp.where(first, 0.0, xp_ref[0])
        ext_ref[SUBLANES:SUBLANES + tc] = x_ref[0]
        ext_ref[SUBLANES + tc:2 * SUBLANES + tc] = jnp.where(last, 0.0, xn_ref[0])
        xc = cb_ref[...]
        for k in range(RG_CONV):
            xc = xc + cw_ref[k:k + 1, :] * ext_ref[SUBLANES - 2 + k:SUBLANES - 2 + k + tc]
        g = _bdot(xc, wg_ref[d]) + bg_ref[d]
        r = jax.nn.sigmoid(g[:, :dr])
        i = jax.nn.sigmoid(g[:, dr:])
        log_a = r * cl_ref[d]
        a = jnp.exp(log_a)
        b = jnp.sqrt(-jnp.tanh(log_a) * (1.0 + a * a)) * (i * xc)
        a_ref[d] = a
        b_ref[d] = b

    rows = lax.broadcasted_iota(jnp.int32, (SUBLANES, dr), 0)

    def scan(d, reverse, out_ref):
        def body(it, carry):
            t = (nt - 1 - it) if reverse else it
            off = pl.multiple_of(t * SUBLANES, SUBLANES)
            a = a_ref[d, pl.ds(off, SUBLANES), :]
            b = b_ref[d, pl.ds(off, SUBLANES), :]
            for s in (1, 2, 4):
                if reverse:
                    a_s = pltpu.roll(a, SUBLANES - s, 0)
                    b_s = pltpu.roll(b, SUBLANES - s, 0)
                    m = rows < SUBLANES - s
                else:
                    a_s = pltpu.roll(a, s, 0)
                    b_s = pltpu.roll(b, s, 0)
                    m = rows >= s
                b = jnp.where(m, a * b_s + b, b)
                a = jnp.where(m, a * a_s, a)
            h = b + a * carry
            out_ref[0, pl.ds(off, SUBLANES), :] = h
            edge = h[0:1] if reverse else h[SUBLANES - 1:SUBLANES]
            return jnp.broadcast_to(edge, (SUBLANES, dr))

        carry_ref[d] = lax.fori_loop(0, nt, body, carry_ref[d])

    gates(xf_ref, xfp_ref, xfn_ref, j == 0, j == nc - 1, 0)
    scan(0, False, hf_ref)
    gates(xb_ref, xbp_ref, xbn_ref, j == nc - 1, j == 0, 1)
    scan(1, True, hb_ref)


def _rglru(z, conv_w, conv_b, wg, bg, cl):
    b, s, _ = z.shape
    dr = conv_w.shape[1]
    tc = min(SCAN_CHUNK, s)
    nc = s // tc
    nb8 = s // SUBLANES
    cb8 = tc // SUBLANES

    def main(rev):
        return pl.BlockSpec((1, tc, dr), (lambda i, j: (i, nc - 1 - j, 0)) if rev else (lambda i, j: (i, j, 0)))

    def prev(rev):
        def f(i, j):
            c = (nc - 1 - j) if rev else j
            return (i, jnp.maximum(c * cb8 - 1, 0), 0)
        return pl.BlockSpec((1, SUBLANES, dr), f)

    def nxt(rev):
        def f(i, j):
            c = (nc - 1 - j) if rev else j
            return (i, jnp.minimum((c + 1) * cb8, nb8 - 1), 0)
        return pl.BlockSpec((1, SUBLANES, dr), f)

    def const(shape):
        return pl.BlockSpec(shape, lambda i, j: (0,) * len(shape))

    out_sd = jax.ShapeDtypeStruct((b, s, dr), F32)
    return pl.pallas_call(
        _rglru_kernel,
        grid=(b, nc),
        in_specs=[main(False), prev(False), nxt(False), main(True), prev(True), nxt(True),
                  const((RG_CONV, dr)), const((1, dr)), const((2, dr, 2 * dr)), const((2, 1, 2 * dr)),
                  const((2, 1, dr))],
        out_specs=[pl.BlockSpec((1, tc, dr), lambda i, j: (i, j, 0)),
                   pl.BlockSpec((1, tc, dr), lambda i, j: (i, nc - 1 - j, 0))],
        out_shape=[out_sd, out_sd],
        scratch_shapes=[pltpu.VMEM((tc + 2 * SUBLANES, dr), F32), pltpu.VMEM((2, tc, dr), F32),
                        pltpu.VMEM((2, tc, dr), F32), pltpu.VMEM((2, SUBLANES, dr), F32)],
        compiler_params=_cparams(("parallel", "arbitrary")),
        name="rglru",
    )(z, z, z, z, z, z, conv_w, conv_b, wg, bg, cl)


def _mla_proj_kernel(z_ref, qn_ref, kvn_ref, wuq_ref, wk_ref, wv_ref, gq_ref, gk_ref, rc_ref, r1_ref, r2_ref,
                     q_ref, k_ref, v_ref):
    zz = z_ref[0]
    q_lora = qn_ref.shape[1]
    kv_lora = kvn_ref.shape[1]
    ql = zz[:, :q_lora]
    kvl = zz[:, q_lora:q_lora + kv_lora]
    kr = zz[:, q_lora + kv_lora:]
    q = _bdot(_rms(ql, qn_ref[...], q_lora), wuq_ref[...])
    kvn = _rms(kvl, kvn_ref[...], kv_lora)
    kk = _bdot(kvn, wk_ref[...])
    vv = _bdot(kvn, wv_ref[...])
    rc, r1, r2 = rc_ref[...], r1_ref[...], r2_ref[...]
    half = QK_ROPE // 2
    scale = QK_DIM ** -0.5

    def norm_rope(xh, g):
        xh = _rms(xh, g, QK_DIM)
        return xh * rc + pltpu.roll(xh, HEAD_SLAB - half, 1) * r1 + pltpu.roll(xh, half, 1) * r2

    for h in range(MLA_HEADS):
        sl = slice(h * HEAD_SLAB, (h + 1) * HEAD_SLAB)
        q_ref[0, h] = (norm_rope(q[:, sl], gq_ref[...]) * scale).astype(BF16)
        k_ref[0, h] = norm_rope(kk[:, sl] + kr, gk_ref[...]).astype(BF16)
    for p in range(MLA_HEADS // 2):
        v_ref[0, p] = vv[:, p * LANES:(p + 1) * LANES].astype(BF16)


def _mla_proj(z, col_block, qn, kvn, wuq, wk, wv, gq, gk, rc, r1, r2):
    b, s, _ = z.shape
    tm = min(ROW_TILE, s)
    width = qn.shape[1] + kvn.shape[1] + HEAD_SLAB

    def const(a):
        return pl.BlockSpec(a.shape, lambda i, j: (0,) * a.ndim)

    def rope(a):
        return pl.BlockSpec((tm, HEAD_SLAB), lambda i, j: (j, 0))

    hp = MLA_HEADS // 2
    return pl.pallas_call(
        _mla_proj_kernel,
        grid=(b, s // tm),
        in_specs=[pl.BlockSpec((1, tm, width), lambda i, j: (i, j, col_block)),
                  const(qn), const(kvn), const(wuq), const(wk), const(wv), const(gq), const(gk),
                  rope(rc), rope(r1), rope(r2)],
        out_specs=[pl.BlockSpec((1, MLA_HEADS, tm, HEAD_SLAB), lambda i, j: (i, 0, j, 0)),
                   pl.BlockSpec((1, MLA_HEADS, tm, HEAD_SLAB), lambda i, j: (i, 0, j, 0)),
                   pl.BlockSpec((1, hp, tm, LANES), lambda i, j: (i, 0, j, 0))],
        out_shape=[jax.ShapeDtypeStruct((b, MLA_HEADS, s, HEAD_SLAB), BF16),
                   jax.ShapeDtypeStruct((b, MLA_HEADS, s, HEAD_SLAB), BF16),
                   jax.ShapeDtypeStruct((b, hp, s, LANES), BF16)],
        compiler_params=_cparams(("parallel", "parallel")),
        name="mla_proj",
    )(z, qn, kvn, wuq, wk, wv, gq, gk, rc, r1, r2)


def _flash_kernel(q_ref, k_ref, v_ref, o_ref, m_ref, l_ref, acc_ref):
    ik = pl.program_id(3)

    @pl.when(ik == 0)
    def _():
        m_ref[...] = jnp.full_like(m_ref, -jnp.inf)
        l_ref[...] = jnp.zeros_like(l_ref)
        acc_ref[...] = jnp.zeros_like(acc_ref)

    v = v_ref[0, 0]
    for hh in range(2):
        s = lax.dot_general(q_ref[0, hh], k_ref[0, hh], (((1,), (1,)), ((), ())), preferred_element_type=F32)
        m_prev = m_ref[hh]
        m_new = jnp.maximum(m_prev, jnp.max(s, axis=-1, keepdims=True))
        alpha = jnp.exp(m_prev - m_new)
        p = jnp.exp(s - m_new[:, :1])
        l_ref[hh] = alpha * l_ref[hh] + jnp.sum(p, axis=-1, keepdims=True)
        acc_ref[hh] = alpha * acc_ref[hh] + jnp.dot(p.astype(BF16), v, preferred_element_type=F32)
        m_ref[hh] = m_new

    @pl.when(ik == pl.num_programs(3) - 1)
    def _():
        lane = lax.broadcasted_iota(jnp.int32, acc_ref.shape[1:], 1)
        o_ref[0] = jnp.where(lane < V_DIM, acc_ref[0] / l_ref[0], acc_ref[1] / l_ref[1])


def _attention(q, k, v):
    b, h, s, _ = q.shape
    tq = min(ATTN_TQ, s)
    tk = min(ATTN_TK, s)
    hp = h // 2
    return pl.pallas_call(
        _flash_kernel,
        grid=(b, hp, s // tq, s // tk),
        in_specs=[pl.BlockSpec((1, 2, tq, HEAD_SLAB), lambda i, p, a, c: (i, p, a, 0)),
                  pl.BlockSpec((1, 2, tk, HEAD_SLAB), lambda i, p, a, c: (i, p, c, 0)),
                  pl.BlockSpec((1, 1, tk, LANES), lambda i, p, a, c: (i, p, c, 0))],
        out_specs=pl.BlockSpec((1, tq, LANES), lambda i, p, a, c: (i, a, p)),
        out_shape=jax.ShapeDtypeStruct((b, s, hp * LANES), F32),
        scratch_shapes=[pltpu.VMEM((2, tq, LANES), F32)] * 3,
        compiler_params=_cparams(("parallel", "parallel", "parallel", "arbitrary")),
        name="attention",
    )(q, k, v)


def _ab_out_kernel(x_ref, hf_ref, hb_ref, y_ref, at_ref, w_ref, g_ref, o_ref):
    rg = (hf_ref[0] + hb_ref[0]) * jax.nn.gelu(y_ref[0])
    cat = jnp.concatenate([rg.astype(BF16), at_ref[0].astype(BF16)], axis=-1)
    m = jnp.dot(cat, w_ref[...], preferred_element_type=F32)
    o_ref[0] = x_ref[0] + g_ref[0] * m


def _ab_out(x, hf, hb, z, attn, w, gate):
    b, s, d = x.shape
    dr = hf.shape[2]
    da = attn.shape[2]
    tm = min(ROW_TILE, s)
    return pl.pallas_call(
        _ab_out_kernel,
        grid=(b, s // tm),
        in_specs=[pl.BlockSpec((1, tm, d), lambda i, j: (i, j, 0)),
                  pl.BlockSpec((1, tm, dr), lambda i, j: (i, j, 0)),
                  pl.BlockSpec((1, tm, dr), lambda i, j: (i, j, 0)),
                  pl.BlockSpec((1, tm, dr), lambda i, j: (i, j, 1)),
                  pl.BlockSpec((1, tm, da), lambda i, j: (i, j, 0)),
                  pl.BlockSpec(w.shape, lambda i, j: (0, 0)),
                  pl.BlockSpec((1, 1, d), lambda i, j: (i, 0, 0))],
        out_specs=pl.BlockSpec((1, tm, d), lambda i, j: (i, j, 0)),
        out_shape=jax.ShapeDtypeStruct((b, s, d), F32),
        compiler_params=_cparams(("parallel", "parallel")),
        name="ab_out",
    )(x, hf, hb, z, attn, w, gate)


def _c_in_kernel(x_ref, g_ref, sc_ref, sh_ref, w_ref, bg_ref, cx_ref):
    h = _norm_mod(x_ref[0], g_ref[...], sc_ref[0], sh_ref[0])
    z = _bdot(h, w_ref[...])
    dc = bg_ref.shape[2]
    bg_ref[0] = z[:, :dc]
    cx_ref[0] = z[:, dc:2 * dc] * z[:, 2 * dc:]


def _c_in(x, gain, sc, sh, w):
    b, s, d = x.shape
    dc = w.shape[1] // 3
    tm = min(ROW_TILE, s)
    sd = jax.ShapeDtypeStruct((b, s, dc), F32)
    return pl.pallas_call(
        _c_in_kernel,
        grid=(b, s // tm),
        in_specs=[pl.BlockSpec((1, tm, d), lambda i, j: (i, j, 0)),
                  pl.BlockSpec((1, d), lambda i, j: (0, 0)),
                  pl.BlockSpec((1, 1, d), lambda i, j: (i, 0, 0)),
                  pl.BlockSpec((1, 1, d), lambda i, j: (i, 0, 0)),
                  pl.BlockSpec(w.shape, lambda i, j: (0, 0))],
        out_specs=[pl.BlockSpec((1, tm, dc), lambda i, j: (i, j, 0))] * 2,
        out_shape=[sd, sd],
        compiler_params=_cparams(("parallel", "parallel")),
        name="c_in",
    )(x, gain, sc, sh, w)


def _c_out_kernel(x_ref, bg_ref, cx_ref, cp_ref, cn_ref, cw_ref, w_ref, g_ref, o_ref, ext_ref):
    j = pl.program_id(1)
    tm = cx_ref.shape[1]
    ext_ref[0:SUBLANES] = jnp.where(j == 0, 0.0, cp_ref[0])
    ext_ref[SUBLANES:SUBLANES + tm] = cx_ref[0]
    ext_ref[SUBLANES + tm:2 * SUBLANES + tm] = jnp.where(j == pl.num_programs(1) - 1, 0.0, cn_ref[0])
    conv = cw_ref[0:1, :] * ext_ref[SUBLANES - 1:SUBLANES - 1 + tm]
    conv = conv + cw_ref[1:2, :] * ext_ref[SUBLANES:SUBLANES + tm]
    conv = conv + cw_ref[2:3, :] * ext_ref[SUBLANES + 1:SUBLANES + 1 + tm]
    m = _bdot(bg_ref[0] * conv, w_ref[...])
    o_ref[0] = x_ref[0] + g_ref[0] * m


def _c_out(x, bg, cx, conv_w, w, gate):
    b, s, d = x.shape
    dc = bg.shape[2]
    tm = min(ROW_TILE, s)
    nb8 = s // SUBLANES
    t8 = tm // SUBLANES
    return pl.pallas_call(
        _c_out_kernel,
        grid=(b, s // tm),
        in_specs=[pl.BlockSpec((1, tm, d), lambda i, j: (i, j, 0)),
                  pl.BlockSpec((1, tm, dc), lambda i, j: (i, j, 0)),
                  pl.BlockSpec((1, tm, dc), lambda i, j: (i, j, 0)),
                  pl.BlockSpec((1, SUBLANES, dc), lambda i, j: (i, jnp.maximum(j * t8 - 1, 0), 0)),
                  pl.BlockSpec((1, SUBLANES, dc), lambda i, j: (i, jnp.minimum((j + 1) * t8, nb8 - 1), 0)),
                  pl.BlockSpec(conv_w.shape, lambda i, j: (0, 0)),
                  pl.BlockSpec(w.shape, lambda i, j: (0, 0)),
                  pl.BlockSpec((1, 1, d), lambda i, j: (i, 0, 0))],
        out_specs=pl.BlockSpec((1, tm, d), lambda i, j: (i, j, 0)),
        out_shape=jax.ShapeDtypeStruct((b, s, d), F32),
        scratch_shapes=[pltpu.VMEM((tm + 2 * SUBLANES, dc), F32)],
        compiler_params=_cparams(("parallel", "parallel")),
        name="c_out",
    )(x, bg, cx, cx, cx, conv_w, w, gate)


def _top_rows(s, k, payload=None):
    n = s.shape[0]
    ridx = lax.broadcasted_iota(jnp.int32, s.shape, 0)
    vals, ids = [], []
    for _ in range(k):
        m = jnp.max(s, axis=0, keepdims=True)
        first = jnp.min(jnp.where(s == m, ridx, n), axis=0, keepdims=True)
        sel = ridx == first
        ids.append(first if payload is None else jnp.sum(jnp.where(sel, payload, 0), axis=0, keepdims=True))
        vals.append(m)
        s = jnp.where(sel, -jnp.inf, s)
    return vals, ids


_PAIRS = [(a, b) for a in range(PEER_TOPK) for b in range(PEER_TOPK) if (a + 1) * (b + 1) <= PEER_TOPK]


def _route_kernel(x_ref, g_ref, sc_ref, sh_ref, wq_ref, k1_ref, k2_ref, h_ref, idx_ref, gate_ref, q_scr):
    h = _norm_mod(x_ref[0], g_ref[...], sc_ref[0], sh_ref[0])
    h_ref[0] = h
    q_scr[...] = _bdot(h, wq_ref[...])
    n_keys = k1_ref.shape[0]
    half = k1_ref.shape[1]
    tm = x_ref.shape[1]
    nt = (((1,), (1,)), ((), ()))
    pad = -len(_PAIRS) % SUBLANES

    def head(hd, carry):
        off = pl.multiple_of(hd * 2 * half, 2 * half)
        q1 = q_scr[:, pl.ds(off, half)].astype(BF16)
        q2 = q_scr[:, pl.ds(off + half, half)].astype(BF16)
        s1 = lax.dot_general(k1_ref[...], q1, nt, preferred_element_type=F32)
        s2 = lax.dot_general(k2_ref[...], q2, nt, preferred_element_type=F32)
        v1, i1 = _top_rows(s1, PEER_TOPK)
        v2, i2 = _top_rows(s2, PEER_TOPK)
        cv = [v1[a] + v2[b] for a, b in _PAIRS] + [jnp.full((pad, tm), -jnp.inf, F32)]
        ce = [i1[a] * n_keys + i2[b] for a, b in _PAIRS] + [jnp.zeros((pad, tm), jnp.int32)]
        vs, es = _top_rows(jnp.concatenate(cv, axis=0), PEER_TOPK, jnp.concatenate(ce, axis=0))
        vs = jnp.concatenate(vs, axis=0)
        e = jnp.exp(vs - vs[0:1])
        row = pl.multiple_of(hd * PEER_TOPK, PEER_TOPK)
        gate_ref[0, pl.ds(row, PEER_TOPK), :] = e / jnp.sum(e, axis=0, keepdims=True)
        idx_ref[0, pl.ds(row, PEER_TOPK), :] = jnp.concatenate(es, axis=0)
        return carry

    lax.fori_loop(0, PEER_HEADS, head, 0)


def _route(x, gain, sc, sh, wq, k1, k2):
    b, s, d = x.shape
    tm = min(ROUTE_TILE, s)
    nsel = PEER_HEADS * PEER_TOPK
    return pl.pallas_call(
        _route_kernel,
        grid=(b, s // tm),
        in_specs=[pl.BlockSpec((1, tm, d), lambda i, j: (i, j, 0)),
                  pl.BlockSpec((1, d), lambda i, j: (0, 0)),
                  pl.BlockSpec((1, 1, d), lambda i, j: (i, 0, 0)),
                  pl.BlockSpec((1, 1, d), lambda i, j: (i, 0, 0)),
                  pl.BlockSpec(wq.shape, lambda i, j: (0, 0)),
                  pl.BlockSpec(k1.shape, lambda i, j: (0, 0)),
                  pl.BlockSpec(k2.shape, lambda i, j: (0, 0))],
        out_specs=[pl.BlockSpec((1, tm, d), lambda i, j: (i, j, 0)),
                   pl.BlockSpec((1, nsel, tm), lambda i, j: (i, 0, j)),
                   pl.BlockSpec((1, nsel, tm), lambda i, j: (i, 0, j))],
        out_shape=[jax.ShapeDtypeStruct((b, s, d), F32),
                   jax.ShapeDtypeStruct((b, nsel, s), jnp.int32),
                   jax.ShapeDtypeStruct((b, nsel, s), F32)],
        scratch_shapes=[pltpu.VMEM((tm, wq.shape[1]), F32)],
        compiler_params=_cparams(("parallel", "parallel")),
        name="peer_route",
    )(x, gain, sc, sh, wq, k1, k2)


def _pack_table(t):
    n, d = t.shape
    chunks = d // LANES
    tb = t.astype(BF16).reshape(n * chunks // 2, 2, LANES)
    return lax.bitcast_convert_type(jnp.swapaxes(tb, -1, -2), jnp.int32)


def _gather_rows(idx_ref, t, tbl_ref, g_scr, wr):
    for r in range(idx_ref.shape[1]):
        g_scr[pl.ds(r * wr, wr), :] = tbl_ref[pl.ds(pl.multiple_of(idx_ref[t, r], wr), wr), :]
    return pltpu.bitcast(g_scr[...], BF16)


def _split_bf16(a):
    hi = a.astype(BF16).astype(F32)
    return jnp.concatenate([hi, a - hi], axis=0).astype(BF16)


def _chunk_mask(chunks, width):
    lane = lax.broadcasted_iota(jnp.int32, (chunks, width), 1)
    return lane % chunks == lax.broadcasted_iota(jnp.int32, (chunks, width), 0)


def _peer_u_kernel(idx_ref, x_ref, g_ref, tbl_ref, bsum_ref, act_ref, g_scr, y_scr):
    tb = act_ref.shape[0]
    chunks = x_ref.shape[1]
    width = act_ref.shape[1]
    mask = _chunk_mask(chunks, width)
    nt = (((1,), (1,)), ((), ()))

    for t in range(tb):
        gb = _gather_rows(idx_ref, t, tbl_ref, g_scr.at[t % 2], chunks // 2)
        o = lax.dot_general(_split_bf16(x_ref[t]), gb, nt, preferred_element_type=F32)
        o = o[:chunks] + o[chunks:]
        y_scr[t:t + 1, :] = jnp.sum(jnp.where(mask, o, 0.0), axis=0, keepdims=True)
    y = _split_bf16(y_scr[...])
    s = jnp.dot(y, bsum_ref[...], preferred_element_type=F32)
    act_ref[...] = jax.nn.gelu(s[:tb] + s[tb:]) * g_ref[...]


def _peer_u(idx, x, gates, table, bsum):
    t, nsel = idx.shape
    chunks = x.shape[1]
    width = nsel * chunks
    tb = min(PEER_TB, t)
    return pl.pallas_call(
        _peer_u_kernel,
        grid=(t // tb,),
        in_specs=[pl.BlockSpec((tb, nsel), lambda i: (i, 0), memory_space=pltpu.SMEM),
                  pl.BlockSpec((tb, chunks, LANES), lambda i: (i, 0, 0)),
                  pl.BlockSpec((tb, width), lambda i: (i, 0)),
                  pl.BlockSpec(table.shape, lambda i: (0, 0), pipeline_mode=pl.Buffered(1)),
                  pl.BlockSpec(bsum.shape, lambda i: (0, 0), pipeline_mode=pl.Buffered(1))],
        out_specs=pl.BlockSpec((tb, width), lambda i: (i, 0)),
        out_shape=jax.ShapeDtypeStruct((t, width), F32),
        scratch_shapes=[pltpu.VMEM((2, nsel * chunks // 2, LANES), jnp.int32), pltpu.VMEM((tb, width), F32)],
        compiler_params=_cparams(("parallel",)),
        name="peer_u",
    )(idx, x, gates, table, bsum)


def _peer_v_kernel(idx_ref, act_ref, tbl_ref, x_ref, g_ref, o_ref, g_scr):
    tb = idx_ref.shape[0]
    chunks = x_ref.shape[1]
    width = act_ref.shape[1]
    mask = _chunk_mask(chunks, width)

    for t in range(tb):
        gb = _gather_rows(idx_ref, t, tbl_ref, g_scr.at[t % 2], chunks // 2)
        a = jnp.where(mask, jnp.broadcast_to(act_ref[t:t + 1, :], (chunks, width)), 0.0)
        o = jnp.dot(_split_bf16(a), gb, preferred_element_type=F32)
        o_ref[t] = x_ref[t] + g_ref[0] * (o[:chunks] + o[chunks:])


def _peer_v(idx, act, table, x, gate, tokens_per_batch):
    t, nsel = idx.shape
    chunks = x.shape[1]
    width = nsel * chunks
    tb = min(PEER_TB, t)
    return pl.pallas_call(
        _peer_v_kernel,
        grid=(t // tb,),
        in_specs=[pl.BlockSpec((tb, nsel), lambda i: (i, 0), memory_space=pltpu.SMEM),
                  pl.BlockSpec((tb, width), lambda i: (i, 0)),
                  pl.BlockSpec(table.shape, lambda i: (0, 0), pipeline_mode=pl.Buffered(1)),
                  pl.BlockSpec((tb, chunks, LANES), lambda i: (i, 0, 0)),
                  pl.BlockSpec((1, chunks, LANES), lambda i: (i * tb // tokens_per_batch, 0, 0))],
        out_specs=pl.BlockSpec((tb, chunks, LANES), lambda i: (i, 0, 0)),
        out_shape=jax.ShapeDtypeStruct(x.shape, F32),
        scratch_shapes=[pltpu.VMEM((2, nsel * chunks // 2, LANES), jnp.int32)],
        compiler_params=_cparams(("parallel",)),
        name="peer_v",
    )(idx, act, table, x, gate)


def _peer(x, gain, sc, sh, gate, wq, k1, k2, u_tbl, v_tbl):
    b, s, d = x.shape
    chunks = d // LANES
    h, idx, gts = _route(x, gain, sc, sh, wq, k1, k2)
    nsel = idx.shape[1]
    idx = jnp.transpose(idx, (0, 2, 1)).reshape(b * s, nsel) * (chunks // 2)
    gts = jnp.repeat(jnp.transpose(gts, (0, 2, 1)).reshape(b * s, nsel), chunks, axis=1)
    bsum = jnp.kron(jnp.eye(nsel, dtype=F32), jnp.ones((chunks, chunks), F32)).astype(BF16)
    act = _peer_u(idx, h.reshape(b * s, chunks, LANES), gts, u_tbl, bsum)
    out = _peer_v(idx, act, v_tbl, x.reshape(b * s, chunks, LANES), gate.reshape(b, chunks, LANES), s)
    return out.reshape(b, s, d)


def _block_diag(w):
    h, i, j = w.shape
    return jnp.einsum('hij,hg->higj', w, jnp.eye(h, dtype=w.dtype)).reshape(h * i, h * j)


def _rope_tables(s):
    half = QK_ROPE // 2
    inv = 1.0 / (ROPE_THETA ** (jnp.arange(0, QK_ROPE, 2, dtype=F32) / QK_ROPE))
    ang = jnp.arange(s, dtype=F32)[:, None] * inv[None, :]
    cos, sin = jnp.cos(ang), jnp.sin(ang)
    z = jnp.zeros((s, QK_NOPE), F32)
    tail = jnp.zeros((s, HEAD_SLAB - QK_DIM), F32)
    zh = jnp.zeros((s, half), F32)
    rc = jnp.concatenate([z + 1.0, cos, cos, tail + 1.0], axis=1)
    r1 = jnp.concatenate([z, -sin, zh, tail], axis=1)
    r2 = jnp.concatenate([z, zh, sin, tail], axis=1)
    return rc, r1, r2


def _pad_last(a, n):
    return jnp.pad(a, [(0, 0)] * (a.ndim - 1) + [(0, n - a.shape[-1])])


def _trunk(x, mod, p):
    b, s, d = x.shape
    depth = p['ada_w'].shape[0]
    d_rnn = p['rg_conv_w'].shape[2]
    q_lora = p['mla_q_norm'].shape[1]
    kv_lora = p['mla_kv_norm'].shape[1]
    for i in range(depth):
        sh1, sc1, g1, sh2, sc2, g2 = [m[:, None, :] for m in jnp.split(mod[i], 6, axis=-1)]
        j = i // 2
        n1 = p['norm1_g'][i][None, :]
        if i % 2 == 0:
            w_in = p['ab_w_in'][j]
            lat = 2 * d_rnn + q_lora + kv_lora
            w0 = jnp.concatenate([w_in[:, :lat], jnp.zeros((d, QK_NOPE), F32), w_in[:, lat:],
                                  jnp.zeros((d, HEAD_SLAB - QK_DIM), F32)], axis=1).astype(BF16)
            z = _in_proj(x, n1, sc1, sh1, w0)
            wg = jnp.stack([jnp.concatenate([_block_diag(p['rg_wa'][j][k]), _block_diag(p['rg_wx'][j][k])], axis=1)
                            for k in range(2)]).astype(BF16)
            bg = jnp.concatenate([p['rg_ba'][j], p['rg_bx'][j]], axis=1)[:, None, :]
            cl = (-RG_C * jax.nn.softplus(-p['rg_lambda'][j]))[:, None, :]
            hf, hb = _rglru(z, p['rg_conv_w'][j], p['rg_conv_b'][j][None, :], wg, bg, cl)
            wuq = _pad_last(p['mla_w_uq'][j].reshape(q_lora, MLA_HEADS, QK_DIM), HEAD_SLAB)
            wuq = wuq.reshape(q_lora, MLA_HEADS * HEAD_SLAB).astype(BF16)
            wkv = p['mla_w_ukv'][j].reshape(kv_lora, MLA_HEADS, QK_NOPE + V_DIM)
            wk = _pad_last(wkv[:, :, :QK_NOPE], HEAD_SLAB).reshape(kv_lora, MLA_HEADS * HEAD_SLAB).astype(BF16)
            wv = wkv[:, :, QK_NOPE:].reshape(kv_lora, MLA_HEADS * V_DIM).astype(BF16)
            gq = _pad_last(p['mla_qn_q'][j][None, :], HEAD_SLAB)
            gk = _pad_last(p['mla_qn_k'][j][None, :], HEAD_SLAB)
            rc, r1, r2 = _rope_tables(s)
            width = q_lora + kv_lora + HEAD_SLAB
            assert (2 * d_rnn) % width == 0
            q, k, v = _mla_proj(z, 2 * d_rnn // width, p['mla_q_norm'][j][None, :], p['mla_kv_norm'][j][None, :],
                                wuq, wk, wv, gq, gk, rc, r1, r2)
            attn = _attention(q, k, v)
            x = _ab_out(x, hf, hb, z, attn, p['ab_w_out'][j].astype(BF16), g1)
        else:
            bgate, cx = _c_in(x, n1, sc1, sh1, p['c_w_in'][j].astype(BF16))
            x = _c_out(x, bgate, cx, p['c_conv_w'][j], p['c_w_out'][j].astype(BF16), g1)
        x = _peer(x, p['norm2_g'][i][None, :], sc2, sh2, g2, p['peer_wq'][i].astype(BF16),
                  p['peer_k1'][i].astype(BF16), p['peer_k2'][i].astype(BF16),
                  _pack_table(p['peer_u'][i]), _pack_table(p['peer_v'][i]))
    return x


def kernel(x_prompt, x_sample, c_prompt, c_sample, ada_w, ada_b, norm1_g, norm2_g, ab_w_in, rg_conv_w, rg_conv_b, rg_wa, rg_ba, rg_wx, rg_bx, rg_lambda, mla_q_norm, mla_w_uq, mla_kv_norm, mla_w_ukv, mla_qn_q, mla_qn_k, ab_w_out, c_w_in, c_conv_w, c_w_out, peer_wq, peer_k1, peer_k2, peer_u, peer_v):
    p = dict(ada_w=ada_w, norm1_g=norm1_g, norm2_g=norm2_g, ab_w_in=ab_w_in, rg_conv_w=rg_conv_w,
             rg_conv_b=rg_conv_b, rg_wa=rg_wa, rg_ba=rg_ba, rg_wx=rg_wx, rg_bx=rg_bx, rg_lambda=rg_lambda,
             mla_q_norm=mla_q_norm, mla_w_uq=mla_w_uq, mla_kv_norm=mla_kv_norm, mla_w_ukv=mla_w_ukv,
             mla_qn_q=mla_qn_q, mla_qn_k=mla_qn_k, ab_w_out=ab_w_out, c_w_in=c_w_in, c_conv_w=c_conv_w,
             c_w_out=c_w_out, peer_wq=peer_wq, peer_k1=peer_k1, peer_k2=peer_k2, peer_u=peer_u, peer_v=peer_v)
    bp, bs = c_prompt.shape[0], c_sample.shape[0]
    rows = -(-(bp + bs) // SUBLANES) * SUBLANES
    c_all = jnp.pad(jnp.concatenate([c_prompt, c_sample], axis=0), ((0, rows - bp - bs), (0, 0)))
    mod = _modulation(c_all, ada_w, ada_b)
    y_prompt = _trunk(x_prompt, mod[:, :bp], p)
    y_sample = _trunk(x_sample, mod[:, bp:bp + bs], p)
    return (y_prompt, y_sample)
```

```python
import functools

import jax
import jax.numpy as jnp
from jax import lax
from jax.experimental import pallas as pl
from jax.experimental.pallas import tpu as pltpu

F32 = jnp.float32
BF16 = jnp.bfloat16
EPS = 1e-6
LOG2_E = 1.4426950408889634

RG_HEADS = 8
RG_CONV = 4
RG_C = 8.0
MLA_HEADS = 8
QK_NOPE = 64
QK_ROPE = 32
V_DIM = 64
QK_DIM = QK_NOPE + QK_ROPE
ROPE_THETA = 10000.0
PEER_HEADS = 8
PEER_TOPK = 16
HEAD_SLAB = 128
MAX_SOFTMAX_BOUND = 40.0

LANES = 128
SUBLANES = 8
VMEM_LIMIT = 56 * 1024 * 1024

ROW_TILE = 512
SCAN_CHUNK = 1024
ATTN_TQ = 512
ATTN_TK = 1024
ROUTE_TILE = 256
PEER_TB = 16


def _cparams(sem):
    return pltpu.CompilerParams(dimension_semantics=sem, vmem_limit_bytes=VMEM_LIMIT)


def _norm_mod(x, gain, sc, sh):
    ms = jnp.mean(x * x, axis=-1, keepdims=True)
    return x * lax.rsqrt(ms + EPS) * gain * (1.0 + sc) + sh


def _rms(x, gain, n):
    ms = jnp.sum(x * x, axis=-1, keepdims=True) * (1.0 / n)
    return x * lax.rsqrt(ms + EPS) * gain


def _bdot(a, b):
    return jnp.dot(a.astype(BF16), b, preferred_element_type=F32)


def _mod_kernel(c_ref, w_ref, b_ref, o_ref):
    c = c_ref[...]
    s = c * jax.nn.sigmoid(c)
    o_ref[0] = _bdot(s, w_ref[0].astype(BF16)) + b_ref[0]


def _modulation(c_all, ada_w, ada_b):
    depth, d, n = ada_w.shape
    rows = c_all.shape[0]
    tn = 1536
    return pl.pallas_call(
        _mod_kernel,
        grid=(depth, n // tn),
        in_specs=[
            pl.BlockSpec((rows, d), lambda i, j: (0, 0)),
            pl.BlockSpec((1, d, tn), lambda i, j: (i, 0, j)),
            pl.BlockSpec((1, 1, tn), lambda i, j: (i, 0, j)),
        ],
        out_specs=pl.BlockSpec((1, rows, tn), lambda i, j: (i, 0, j)),
        out_shape=jax.ShapeDtypeStruct((depth, rows, n), F32),
        compiler_params=_cparams(("parallel", "parallel")),
        name="modulation",
    )(c_all, ada_w, ada_b.reshape(depth, 1, n))


def _in_proj_kernel(x_ref, g_ref, sc_ref, sh_ref, w_ref, z_ref):
    h = _norm_mod(x_ref[0], g_ref[...], sc_ref[0], sh_ref[0])
    z_ref[0] = _bdot(h, w_ref[...])


def _in_proj(x, gain, sc, sh, w):
    b, s, d = x.shape
    n = w.shape[1]
    tm = min(ROW_TILE, s)
    return pl.pallas_call(
        _in_proj_kernel,
        grid=(b, s // tm),
        in_specs=[
            pl.BlockSpec((1, tm, d), lambda i, j: (i, j, 0)),
            pl.BlockSpec((1, d), lambda i, j: (0, 0)),
            pl.BlockSpec((1, 1, d), lambda i, j: (i, 0, 0)),
            pl.BlockSpec((1, 1, d), lambda i, j: (i, 0, 0)),
            pl.BlockSpec((d, n), lambda i, j: (0, 0)),
        ],
        out_specs=pl.BlockSpec((1, tm, n), lambda i, j: (i, j, 0)),
        out_shape=jax.ShapeDtypeStruct((b, s, n), F32),
        compiler_params=_cparams(("parallel", "parallel")),
        name="in_proj",
    )(x, gain, sc, sh, w)


def _rglru_kernel(xf_ref, xfp_ref, xfn_ref, xb_ref, xbp_ref, xbn_ref, cw_ref, cb_ref, wg_ref, bg_ref, cl_ref,
                  hf_ref, hb_ref, ext_ref, a_ref, b_ref, carry_ref):
    j = pl.program_id(1)
    nc = pl.num_programs(1)
    tc = xf_ref.shape[1]
    dr = xf_ref.shape[2]
    nt = tc // SUBLANES

    @pl.when(j == 0)
    def _():
        carry_ref[...] = jnp.zeros_like(carry_ref)

    def gates(x_ref, xp_ref, xn_ref, first, last, d):
        ext_ref[0:SUBLANES] = jnp.where(first, 0.0, xp_ref[0])
        ext_ref[SUBLANES:SUBLANES + tc] = x_ref[0]
        ext_ref[SUBLANES + tc:2 * SUBLANES + tc] = jnp.where(last, 0.0, xn_ref[0])
        xc = cb_ref[...]
        for k in range(RG_CONV):
            xc = xc + cw_ref[k:k + 1, :] * ext_ref[SUBLANES - 2 + k:SUBLANES - 2 + k + tc]
        g = _bdot(xc, wg_ref[d]) + bg_ref[d]
        r = jax.nn.sigmoid(g[:, :dr])
        i = jax.nn.sigmoid(g[:, dr:])
        log_a = r * cl_ref[d]
        a = jnp.exp(log_a)
        b = jnp.sqrt(-jnp.tanh(log_a) * (1.0 + a * a)) * (i * xc)
        a_ref[d] = a
        b_ref[d] = b

    rows = lax.broadcasted_iota(jnp.int32, (SUBLANES, dr), 0)

    def scan(d, reverse, out_ref):
        def body(it, carry):
            t = (nt - 1 - it) if reverse else it
            off = pl.multiple_of(t * SUBLANES, SUBLANES)
            a = a_ref[d, pl.ds(off, SUBLANES), :]
            b = b_ref[d, pl.ds(off, SUBLANES), :]
            for s in (1, 2, 4):
                if reverse:
                    a_s = pltpu.roll(a, SUBLANES - s, 0)
                    b_s = pltpu.roll(b, SUBLANES - s, 0)
                    m = rows < SUBLANES - s
                else:
                    a_s = pltpu.roll(a, s, 0)
                    b_s = pltpu.roll(b, s, 0)
                    m = rows >= s
                b = jnp.where(m, a * b_s + b, b)
                a = jnp.where(m, a * a_s, a)
            h = b + a * carry
            out_ref[0, pl.ds(off, SUBLANES), :] = h
            edge = h[0:1] if reverse else h[SUBLANES - 1:SUBLANES]
            return jnp.broadcast_to(edge, (SUBLANES, dr))

        carry_ref[d] = lax.fori_loop(0, nt, body, carry_ref[d])

    gates(xf_ref, xfp_ref, xfn_ref, j == 0, j == nc - 1, 0)
    scan(0, False, hf_ref)
    gates(xb_ref, xbp_ref, xbn_ref, j == nc - 1, j == 0, 1)
    scan(1, True, hb_ref)


def _rglru(z, conv_w, conv_b, wg, bg, cl):
    b, s, _ = z.shape
    dr = conv_w.shape[1]
    tc = min(SCAN_CHUNK, s)
    nc = s // tc
    nb8 = s // SUBLANES
    cb8 = tc // SUBLANES

    def main(rev):
        return pl.BlockSpec((1, tc, dr), (lambda i, j: (i, nc - 1 - j, 0)) if rev else (lambda i, j: (i, j, 0)))

    def prev(rev):
        def f(i, j):
            c = (nc - 1 - j) if rev else j
            return (i, jnp.maximum(c * cb8 - 1, 0), 0)
        return pl.BlockSpec((1, SUBLANES, dr), f)

    def nxt(rev):
        def f(i, j):
            c = (nc - 1 - j) if rev else j
            return (i, jnp.minimum((c + 1) * cb8, nb8 - 1), 0)
        return pl.BlockSpec((1, SUBLANES, dr), f)

    def const(shape):
        return pl.BlockSpec(shape, lambda i, j: (0,) * len(shape))

    out_sd = jax.ShapeDtypeStruct((b, s, dr), F32)
    return pl.pallas_call(
        _rglru_kernel,
        grid=(b, nc),
        in_specs=[main(False), prev(False), nxt(False), main(True), prev(True), nxt(True),
                  const((RG_CONV, dr)), const((1, dr)), const((2, dr, 2 * dr)), const((2, 1, 2 * dr)),
                  const((2, 1, dr))],
        out_specs=[pl.BlockSpec((1, tc, dr), lambda i, j: (i, j, 0)),
                   pl.BlockSpec((1, tc, dr), lambda i, j: (i, nc - 1 - j, 0))],
        out_shape=[out_sd, out_sd],
        scratch_shapes=[pltpu.VMEM((tc + 2 * SUBLANES, dr), F32), pltpu.VMEM((2, tc, dr), F32),
                        pltpu.VMEM((2, tc, dr), F32), pltpu.VMEM((2, SUBLANES, dr), F32)],
        compiler_params=_cparams(("parallel", "arbitrary")),
        name="rglru",
    )(z, z, z, z, z, z, conv_w, conv_b, wg, bg, cl)


def _mla_proj_kernel(z_ref, qn_ref, kvn_ref, wuq_ref, wk_ref, wv_ref, gq_ref, gk_ref, qb_ref, kb_ref,
                     rc_ref, r1_ref, r2_ref, q_ref, k_ref, v_ref):
    zz = z_ref[0]
    q_lora = qn_ref.shape[1]
    kv_lora = kvn_ref.shape[1]
    ql = zz[:, :q_lora]
    kvl = zz[:, q_lora:q_lora + kv_lora]
    kr = zz[:, q_lora + kv_lora:]
    q = _bdot(_rms(ql, qn_ref[...], q_lora), wuq_ref[...])
    kvn = _rms(kvl, kvn_ref[...], kv_lora)
    kk = _bdot(kvn, wk_ref[...])
    vv = _bdot(kvn, wv_ref[...])
    rc, r1, r2 = rc_ref[...], r1_ref[...], r2_ref[...]
    half = QK_ROPE // 2
    scale = QK_DIM ** -0.5 * LOG2_E

    def norm_rope(xh, g):
        xh = _rms(xh, g, QK_DIM)
        return xh * rc + pltpu.roll(xh, HEAD_SLAB - half, 1) * r1 + pltpu.roll(xh, half, 1) * r2

    for h in range(MLA_HEADS):
        sl = slice(h * HEAD_SLAB, (h + 1) * HEAD_SLAB)
        q_ref[0, h] = (norm_rope(q[:, sl], gq_ref[...]) * scale + qb_ref[...]).astype(BF16)
        k_ref[0, h] = (norm_rope(kk[:, sl] + kr, gk_ref[...]) + kb_ref[...]).astype(BF16)
    for p in range(MLA_HEADS // 2):
        v_ref[0, p] = vv[:, p * LANES:(p + 1) * LANES].astype(BF16)


def _mla_proj(z, col_block, qn, kvn, wuq, wk, wv, gq, gk, qb, kb, rc, r1, r2):
    b, s, _ = z.shape
    tm = min(ROW_TILE, s)
    width = qn.shape[1] + kvn.shape[1] + HEAD_SLAB

    def const(a):
        return pl.BlockSpec(a.shape, lambda i, j: (0,) * a.ndim)

    def rope(a):
        return pl.BlockSpec((tm, HEAD_SLAB), lambda i, j: (j, 0))

    hp = MLA_HEADS // 2
    return pl.pallas_call(
        _mla_proj_kernel,
        grid=(b, s // tm),
        in_specs=[pl.BlockSpec((1, tm, width), lambda i, j: (i, j, col_block)),
                  const(qn), const(kvn), const(wuq), const(wk), const(wv), const(gq), const(gk), const(qb), const(kb),
                  rope(rc), rope(r1), rope(r2)],
        out_specs=[pl.BlockSpec((1, MLA_HEADS, tm, HEAD_SLAB), lambda i, j: (i, 0, j, 0)),
                   pl.BlockSpec((1, MLA_HEADS, tm, HEAD_SLAB), lambda i, j: (i, 0, j, 0)),
                   pl.BlockSpec((1, hp, tm, LANES), lambda i, j: (i, 0, j, 0))],
        out_shape=[jax.ShapeDtypeStruct((b, MLA_HEADS, s, HEAD_SLAB), BF16),
                   jax.ShapeDtypeStruct((b, MLA_HEADS, s, HEAD_SLAB), BF16),
                   jax.ShapeDtypeStruct((b, hp, s, LANES), BF16)],
        compiler_params=_cparams(("parallel", "parallel")),
        name="mla_proj",
    )(z, qn, kvn, wuq, wk, wv, gq, gk, qb, kb, rc, r1, r2)


def _flash_kernel(q_ref, k_ref, v_ref, o_ref, m_ref, l_ref, acc_ref):
    ik = pl.program_id(3)

    @pl.when(ik == 0)
    def _():
        m_ref[...] = jnp.full_like(m_ref, -jnp.inf)
        l_ref[...] = jnp.zeros_like(l_ref)
        acc_ref[...] = jnp.zeros_like(acc_ref)

    v = v_ref[0, 0]
    for hh in range(2):
        s = lax.dot_general(q_ref[0, hh], k_ref[0, hh], (((1,), (1,)), ((), ())), preferred_element_type=F32)
        m_prev = m_ref[hh]
        m_new = jnp.maximum(m_prev, jnp.max(s, axis=-1, keepdims=True))
        alpha = jnp.exp2(m_prev - m_new)
        p = jnp.exp2(s - m_new[:, :1])
        l_ref[hh] = alpha * l_ref[hh] + jnp.sum(p, axis=-1, keepdims=True)
        acc_ref[hh] = alpha * acc_ref[hh] + jnp.dot(p.astype(BF16), v, preferred_element_type=F32)
        m_ref[hh] = m_new

    @pl.when(ik == pl.num_programs(3) - 1)
    def _():
        lane = lax.broadcasted_iota(jnp.int32, acc_ref.shape[1:], 1)
        o_ref[0] = jnp.where(lane < V_DIM, acc_ref[0] / l_ref[0], acc_ref[1] / l_ref[1])


def _flash_bounded_kernel(q_ref, k_ref, v_ref, o_ref, l_ref, acc_ref):
    ik = pl.program_id(3)

    @pl.when(ik == 0)
    def _():
        l_ref[...] = jnp.zeros_like(l_ref)
        acc_ref[...] = jnp.zeros_like(acc_ref)

    v = v_ref[0, 0]
    for hh in range(2):
        s = lax.dot_general(q_ref[0, hh], k_ref[0, hh], (((1,), (1,)), ((), ())), preferred_element_type=F32)
        p = jnp.exp2(s)
        l_ref[hh] = l_ref[hh] + jnp.sum(p, axis=-1, keepdims=True)
        acc_ref[hh] = acc_ref[hh] + jnp.dot(p.astype(BF16), v, preferred_element_type=F32)

    @pl.when(ik == pl.num_programs(3) - 1)
    def _():
        lane = lax.broadcasted_iota(jnp.int32, acc_ref.shape[1:], 1)
        o_ref[0] = jnp.where(lane < V_DIM, acc_ref[0] / l_ref[0], acc_ref[1] / l_ref[1])


def _attention(q, k, v, bounded):
    b, h, s, _ = q.shape
    tq = min(ATTN_TQ, s)
    tk = min(ATTN_TK, s)
    hp = h // 2
    return pl.pallas_call(
        _flash_bounded_kernel if bounded else _flash_kernel,
        grid=(b, hp, s // tq, s // tk),
        in_specs=[pl.BlockSpec((1, 2, tq, HEAD_SLAB), lambda i, p, a, c: (i, p, a, 0)),
                  pl.BlockSpec((1, 2, tk, HEAD_SLAB), lambda i, p, a, c: (i, p, c, 0)),
                  pl.BlockSpec((1, 1, tk, LANES), lambda i, p, a, c: (i, p, c, 0))],
        out_specs=pl.BlockSpec((1, tq, LANES), lambda i, p, a, c: (i, a, p)),
        out_shape=jax.ShapeDtypeStruct((b, s, hp * LANES), F32),
        scratch_shapes=[pltpu.VMEM((2, tq, LANES), F32)] * (2 if bounded else 3),
        compiler_params=_cparams(("parallel", "parallel", "parallel", "arbitrary")),
        name="attention_bounded" if bounded else "attention",
    )(q, k, v)


def _ab_out_kernel(x_ref, hf_ref, hb_ref, y_ref, at_ref, w_ref, g_ref, o_ref):
    rg = (hf_ref[0] + hb_ref[0]) * jax.nn.gelu(y_ref[0])
    cat = jnp.concatenate([rg.astype(BF16), at_ref[0].astype(BF16)], axis=-1)
    m = jnp.dot(cat, w_ref[...], preferred_element_type=F32)
    o_ref[0] = x_ref[0] + g_ref[0] * m


def _ab_out(x, hf, hb, z, attn, w, gate):
    b, s, d = x.shape
    dr = hf.shape[2]
    da = attn.shape[2]
    tm = min(ROW_TILE, s)
    return pl.pallas_call(
        _ab_out_kernel,
        grid=(b, s // tm),
        in_specs=[pl.BlockSpec((1, tm, d), lambda i, j: (i, j, 0)),
                  pl.BlockSpec((1, tm, dr), lambda i, j: (i, j, 0)),
                  pl.BlockSpec((1, tm, dr), lambda i, j: (i, j, 0)),
                  pl.BlockSpec((1, tm, dr), lambda i, j: (i, j, 1)),
                  pl.BlockSpec((1, tm, da), lambda i, j: (i, j, 0)),
                  pl.BlockSpec(w.shape, lambda i, j: (0, 0)),
                  pl.BlockSpec((1, 1, d), lambda i, j: (i, 0, 0))],
        out_specs=pl.BlockSpec((1, tm, d), lambda i, j: (i, j, 0)),
        out_shape=jax.ShapeDtypeStruct((b, s, d), F32),
        compiler_params=_cparams(("parallel", "parallel")),
        name="ab_out",
    )(x, hf, hb, z, attn, w, gate)


def _c_in_kernel(x_ref, g_ref, sc_ref, sh_ref, w_ref, bg_ref, cx_ref):
    h = _norm_mod(x_ref[0], g_ref[...], sc_ref[0], sh_ref[0])
    z = _bdot(h, w_ref[...])
    dc = bg_ref.shape[2]
    bg_ref[0] = z[:, :dc]
    cx_ref[0] = z[:, dc:2 * dc] * z[:, 2 * dc:]


def _c_in(x, gain, sc, sh, w):
    b, s, d = x.shape
    dc = w.shape[1] // 3
    tm = min(ROW_TILE, s)
    sd = jax.ShapeDtypeStruct((b, s, dc), F32)
    return pl.pallas_call(
        _c_in_kernel,
        grid=(b, s // tm),
        in_specs=[pl.BlockSpec((1, tm, d), lambda i, j: (i, j, 0)),
                  pl.BlockSpec((1, d), lambda i, j: (0, 0)),
                  pl.BlockSpec((1, 1, d), lambda i, j: (i, 0, 0)),
                  pl.BlockSpec((1, 1, d), lambda i, j: (i, 0, 0)),
                  pl.BlockSpec(w.shape, lambda i, j: (0, 0))],
        out_specs=[pl.BlockSpec((1, tm, dc), lambda i, j: (i, j, 0))] * 2,
        out_shape=[sd, sd],
        compiler_params=_cparams(("parallel", "parallel")),
        name="c_in",
    )(x, gain, sc, sh, w)


def _c_out_kernel(x_ref, bg_ref, cx_ref, cp_ref, cn_ref, cw_ref, w_ref, g_ref, o_ref, ext_ref):
    j = pl.program_id(1)
    tm = cx_ref.shape[1]
    ext_ref[0:SUBLANES] = jnp.where(j == 0, 0.0, cp_ref[0])
    ext_ref[SUBLANES:SUBLANES + tm] = cx_ref[0]
    ext_ref[SUBLANES + tm:2 * SUBLANES + tm] = jnp.where(j == pl.num_programs(1) - 1, 0.0, cn_ref[0])
    conv = cw_ref[0:1, :] * ext_ref[SUBLANES - 1:SUBLANES - 1 + tm]
    conv = conv + cw_ref[1:2, :] * ext_ref[SUBLANES:SUBLANES + tm]
    conv = conv + cw_ref[2:3, :] * ext_ref[SUBLANES + 1:SUBLANES + 1 + tm]
    m = _bdot(bg_ref[0] * conv, w_ref[...])
    o_ref[0] = x_ref[0] + g_ref[0] * m


def _c_out(x, bg, cx, conv_w, w, gate):
    b, s, d = x.shape
    dc = bg.shape[2]
    tm = min(ROW_TILE, s)
    nb8 = s // SUBLANES
    t8 = tm // SUBLANES
    return pl.pallas_call(
        _c_out_kernel,
        grid=(b, s // tm),
        in_specs=[pl.BlockSpec((1, tm, d), lambda i, j: (i, j, 0)),
                  pl.BlockSpec((1, tm, dc), lambda i, j: (i, j, 0)),
                  pl.BlockSpec((1, tm, dc), lambda i, j: (i, j, 0)),
                  pl.BlockSpec((1, SUBLANES, dc), lambda i, j: (i, jnp.maximum(j * t8 - 1, 0), 0)),
                  pl.BlockSpec((1, SUBLANES, dc), lambda i, j: (i, jnp.minimum((j + 1) * t8, nb8 - 1), 0)),
                  pl.BlockSpec(conv_w.shape, lambda i, j: (0, 0)),
                  pl.BlockSpec(w.shape, lambda i, j: (0, 0)),
                  pl.BlockSpec((1, 1, d), lambda i, j: (i, 0, 0))],
        out_specs=pl.BlockSpec((1, tm, d), lambda i, j: (i, j, 0)),
        out_shape=jax.ShapeDtypeStruct((b, s, d), F32),
        scratch_shapes=[pltpu.VMEM((tm + 2 * SUBLANES, dc), F32)],
        compiler_params=_cparams(("parallel", "parallel")),
        name="c_out",
    )(x, bg, cx, cx, cx, conv_w, w, gate)


def _top_rows(s, k, payload=None):
    n = s.shape[0]
    ridx = lax.broadcasted_iota(jnp.int32, s.shape, 0)
    vals, ids = [], []
    for _ in range(k):
        m = jnp.max(s, axis=0, keepdims=True)
        first = jnp.min(jnp.where(s == m, ridx, n), axis=0, keepdims=True)
        sel = ridx == first
        ids.append(first if payload is None else jnp.sum(jnp.where(sel, payload, 0), axis=0, keepdims=True))
        vals.append(m)
        s = jnp.where(sel, -jnp.inf, s)
    return vals, ids


_PAIRS = [(a, b) for a in range(PEER_TOPK) for b in range(PEER_TOPK) if (a + 1) * (b + 1) <= PEER_TOPK]


def _route_kernel(x_ref, g_ref, sc_ref, sh_ref, wq_ref, k1_ref, k2_ref, h_ref, idx_ref, gate_ref, q_scr):
    h = _norm_mod(x_ref[0], g_ref[...], sc_ref[0], sh_ref[0])
    h_ref[0] = h
    q_scr[...] = _bdot(h, wq_ref[...])
    n_keys = k1_ref.shape[0]
    half = k1_ref.shape[1]
    tm = x_ref.shape[1]
    nt = (((1,), (1,)), ((), ()))
    pad = -len(_PAIRS) % SUBLANES

    def head(hd, carry):
        off = pl.multiple_of(hd * 2 * half, 2 * half)
        q1 = q_scr[:, pl.ds(off, half)].astype(BF16)
        q2 = q_scr[:, pl.ds(off + half, half)].astype(BF16)
        s1 = lax.dot_general(k1_ref[...], q1, nt, preferred_element_type=F32)
        s2 = lax.dot_general(k2_ref[...], q2, nt, preferred_element_type=F32)
        v1, i1 = _top_rows(s1, PEER_TOPK)
        v2, i2 = _top_rows(s2, PEER_TOPK)
        cv = [v1[a] + v2[b] for a, b in _PAIRS] + [jnp.full((pad, tm), -jnp.inf, F32)]
        ce = [i1[a] * n_keys + i2[b] for a, b in _PAIRS] + [jnp.zeros((pad, tm), jnp.int32)]
        vs, es = _top_rows(jnp.concatenate(cv, axis=0), PEER_TOPK, jnp.concatenate(ce, axis=0))
        vs = jnp.concatenate(vs, axis=0)
        e = jnp.exp(vs - vs[0:1])
        row = pl.multiple_of(hd * PEER_TOPK, PEER_TOPK)
        gate_ref[0, pl.ds(row, PEER_TOPK), :] = e / jnp.sum(e, axis=0, keepdims=True)
        idx_ref[0, pl.ds(row, PEER_TOPK), :] = jnp.concatenate(es, axis=0)
        return carry

    lax.fori_loop(0, PEER_HEADS, head, 0)


def _route(x, gain, sc, sh, wq, k1, k2):
    b, s, d = x.shape
    tm = min(ROUTE_TILE, s)
    nsel = PEER_HEADS * PEER_TOPK
    return pl.pallas_call(
        _route_kernel,
        grid=(b, s // tm),
        in_specs=[pl.BlockSpec((1, tm, d), lambda i, j: (i, j, 0)),
                  pl.BlockSpec((1, d), lambda i, j: (0, 0)),
                  pl.BlockSpec((1, 1, d), lambda i, j: (i, 0, 0)),
                  pl.BlockSpec((1, 1, d), lambda i, j: (i, 0, 0)),
                  pl.BlockSpec(wq.shape, lambda i, j: (0, 0)),
                  pl.BlockSpec(k1.shape, lambda i, j: (0, 0)),
                  pl.BlockSpec(k2.shape, lambda i, j: (0, 0))],
        out_specs=[pl.BlockSpec((1, tm, d), lambda i, j: (i, j, 0)),
                   pl.BlockSpec((1, nsel, tm), lambda i, j: (i, 0, j)),
                   pl.BlockSpec((1, nsel, tm), lambda i, j: (i, 0, j))],
        out_shape=[jax.ShapeDtypeStruct((b, s, d), F32),
                   jax.ShapeDtypeStruct((b, nsel, s), jnp.int32),
                   jax.ShapeDtypeStruct((b, nsel, s), F32)],
        scratch_shapes=[pltpu.VMEM((tm, wq.shape[1]), F32)],
        compiler_params=_cparams(("parallel", "parallel")),
        name="peer_route",
    )(x, gain, sc, sh, wq, k1, k2)


def _pack_table(t):
    n, d = t.shape
    chunks = d // LANES
    tb = t.astype(BF16).reshape(n * chunks // 2, 2, LANES)
    return lax.bitcast_convert_type(jnp.swapaxes(tb, -1, -2), jnp.int32)


def _gather_rows(idx_ref, t, tbl_ref, wr):
    rows = [tbl_ref[pl.ds(pl.multiple_of(idx_ref[t, r], wr), wr), :] for r in range(idx_ref.shape[1])]
    return pltpu.bitcast(jnp.concatenate(rows, axis=0), BF16)


def _idx_copy(idx_hbm, half_block, smem, sem):
    sub = smem.shape[0]
    return pltpu.make_async_copy(idx_hbm.at[pl.ds(half_block * sub, sub)], smem, sem)


def _staged_halves(idx_hbm, smem_a, smem_b, sems, process):
    i = pl.program_id(0)
    sub = smem_a.shape[0]

    @pl.when(i == 0)
    def _():
        _idx_copy(idx_hbm, 0, smem_a, sems.at[0]).start()

    _idx_copy(idx_hbm, 2 * i, smem_a, sems.at[0]).wait()
    _idx_copy(idx_hbm, 2 * i + 1, smem_b, sems.at[1]).start()
    process(smem_a, 0)
    _idx_copy(idx_hbm, 2 * i + 1, smem_b, sems.at[1]).wait()

    @pl.when(i + 1 < pl.num_programs(0))
    def _():
        _idx_copy(idx_hbm, 2 * i + 2, smem_a, sems.at[0]).start()

    process(smem_b, sub)


def _split_bf16(a):
    hi = a.astype(BF16).astype(F32)
    return jnp.concatenate([hi, a - hi], axis=0).astype(BF16)


def _chunk_mask(chunks, width):
    lane = lax.broadcasted_iota(jnp.int32, (chunks, width), 1)
    return lane % chunks == lax.broadcasted_iota(jnp.int32, (chunks, width), 0)


def _peer_u_kernel(idx_hbm, x_ref, g_ref, tbl_ref, bsum_ref, act_ref, smem_a, smem_b, sems, y_scr):
    tb = act_ref.shape[0]
    chunks = x_ref.shape[1]
    width = act_ref.shape[1]
    mask = _chunk_mask(chunks, width)
    nt = (((1,), (1,)), ((), ()))

    def process(idx_ref, first):
        for t in range(idx_ref.shape[0]):
            gb = _gather_rows(idx_ref, t, tbl_ref, chunks // 2)
            o = lax.dot_general(_split_bf16(x_ref[first + t]), gb, nt, preferred_element_type=F32)
            o = o[:chunks] + o[chunks:]
            y_scr[first + t:first + t + 1, :] = jnp.sum(jnp.where(mask, o, 0.0), axis=0, keepdims=True)

    _staged_halves(idx_hbm, smem_a, smem_b, sems, process)
    y = _split_bf16(y_scr[...])
    s = jnp.dot(y, bsum_ref[...], preferred_element_type=F32)
    act_ref[...] = jax.nn.gelu(s[:tb] + s[tb:]) * g_ref[...]


def _peer_blocks(t):
    sub = min(PEER_TB, t // 2)
    assert t % (2 * sub) == 0
    return sub, 2 * sub


def _idx_scratch(sub, nsel):
    return [pltpu.SMEM((sub, nsel), jnp.int32), pltpu.SMEM((sub, nsel), jnp.int32), pltpu.SemaphoreType.DMA((2,))]


def _peer_u(idx, x, gates, table, bsum):
    t, nsel = idx.shape
    chunks = x.shape[1]
    width = nsel * chunks
    sub, tb = _peer_blocks(t)
    return pl.pallas_call(
        _peer_u_kernel,
        grid=(t // tb,),
        in_specs=[pl.BlockSpec(memory_space=pl.ANY),
                  pl.BlockSpec((tb, chunks, LANES), lambda i: (i, 0, 0)),
                  pl.BlockSpec((tb, width), lambda i: (i, 0)),
                  pl.BlockSpec(table.shape, lambda i: (0, 0), pipeline_mode=pl.Buffered(1)),
                  pl.BlockSpec(bsum.shape, lambda i: (0, 0), pipeline_mode=pl.Buffered(1))],
        out_specs=pl.BlockSpec((tb, width), lambda i: (i, 0)),
        out_shape=jax.ShapeDtypeStruct((t, width), F32),
        scratch_shapes=_idx_scratch(sub, nsel) + [pltpu.VMEM((tb, width), F32)],
        compiler_params=_cparams(("arbitrary",)),
        name="peer_u",
    )(idx, x, gates, table, bsum)


def _peer_v_kernel(idx_hbm, act_ref, tbl_ref, x_ref, g_ref, o_ref, smem_a, smem_b, sems):
    chunks = x_ref.shape[1]
    width = act_ref.shape[1]
    mask = _chunk_mask(chunks, width)

    def process(idx_ref, first):
        for t in range(idx_ref.shape[0]):
            gb = _gather_rows(idx_ref, t, tbl_ref, chunks // 2)
            a = jnp.where(mask, jnp.broadcast_to(act_ref[first + t:first + t + 1, :], (chunks, width)), 0.0)
            o = jnp.dot(_split_bf16(a), gb, preferred_element_type=F32)
            o_ref[first + t] = x_ref[first + t] + g_ref[0] * (o[:chunks] + o[chunks:])

    _staged_halves(idx_hbm, smem_a, smem_b, sems, process)


def _peer_v(idx, act, table, x, gate, tokens_per_batch):
    t, nsel = idx.shape
    chunks = x.shape[1]
    width = nsel * chunks
    sub, tb = _peer_blocks(t)
    assert tokens_per_batch % tb == 0
    return pl.pallas_call(
        _peer_v_kernel,
        grid=(t // tb,),
        in_specs=[pl.BlockSpec(memory_space=pl.ANY),
                  pl.BlockSpec((tb, width), lambda i: (i, 0)),
                  pl.BlockSpec(table.shape, lambda i: (0, 0), pipeline_mode=pl.Buffered(1)),
                  pl.BlockSpec((tb, chunks, LANES), lambda i: (i, 0, 0)),
                  pl.BlockSpec((1, chunks, LANES), lambda i: (i * tb // tokens_per_batch, 0, 0))],
        out_specs=pl.BlockSpec((tb, chunks, LANES), lambda i: (i, 0, 0)),
        out_shape=jax.ShapeDtypeStruct(x.shape, F32),
        scratch_shapes=_idx_scratch(sub, nsel),
        compiler_params=_cparams(("arbitrary",)),
        name="peer_v",
    )(idx, act, table, x, gate)


def _peer(x, gain, sc, sh, gate, wq, k1, k2, u_tbl, v_tbl):
    b, s, d = x.shape
    chunks = d // LANES
    h, idx, gts = _route(x, gain, sc, sh, wq, k1, k2)
    nsel = idx.shape[1]
    idx = jnp.transpose(idx, (0, 2, 1)).reshape(b * s, nsel) * (chunks // 2)
    gts = jnp.repeat(jnp.transpose(gts, (0, 2, 1)).reshape(b * s, nsel), chunks, axis=1)
    bsum = jnp.kron(jnp.eye(nsel, dtype=F32), jnp.ones((chunks, chunks), F32)).astype(BF16)
    act = _peer_u(idx, h.reshape(b * s, chunks, LANES), gts, u_tbl, bsum)
    out = _peer_v(idx, act, v_tbl, x.reshape(b * s, chunks, LANES), gate.reshape(b, chunks, LANES), s)
    return out.reshape(b, s, d)


def _block_diag(w):
    h, i, j = w.shape
    return jnp.einsum('hij,hg->higj', w, jnp.eye(h, dtype=w.dtype)).reshape(h * i, h * j)


def _rope_tables(s):
    half = QK_ROPE // 2
    inv = 1.0 / (ROPE_THETA ** (jnp.arange(0, QK_ROPE, 2, dtype=F32) / QK_ROPE))
    ang = jnp.arange(s, dtype=F32)[:, None] * inv[None, :]
    cos, sin = jnp.cos(ang), jnp.sin(ang)
    z = jnp.zeros((s, QK_NOPE), F32)
    tail = jnp.zeros((s, HEAD_SLAB - QK_DIM), F32)
    zh = jnp.zeros((s, half), F32)
    rc = jnp.concatenate([z + 1.0, cos, cos, tail + 1.0], axis=1)
    r1 = jnp.concatenate([z, -sin, zh, tail], axis=1)
    r2 = jnp.concatenate([z, zh, sin, tail], axis=1)
    return rc, r1, r2


def _pad_last(a, n):
    return jnp.pad(a, [(0, 0)] * (a.ndim - 1) + [(0, n - a.shape[-1])])


def _trunk(x, mod, p):
    b, s, d = x.shape
    depth = p['ada_w'].shape[0]
    d_rnn = p['rg_conv_w'].shape[2]
    q_lora = p['mla_q_norm'].shape[1]
    kv_lora = p['mla_kv_norm'].shape[1]
    for i in range(depth):
        sh1, sc1, g1, sh2, sc2, g2 = [m[:, None, :] for m in jnp.split(mod[i], 6, axis=-1)]
        j = i // 2
        n1 = p['norm1_g'][i][None, :]
        if i % 2 == 0:
            w_in = p['ab_w_in'][j]
            lat = 2 * d_rnn + q_lora + kv_lora
            w0 = jnp.concatenate([w_in[:, :lat], jnp.zeros((d, QK_NOPE), F32), w_in[:, lat:],
                                  jnp.zeros((d, HEAD_SLAB - QK_DIM), F32)], axis=1).astype(BF16)
            z = _in_proj(x, n1, sc1, sh1, w0)
            wg = jnp.stack([jnp.concatenate([_block_diag(p['rg_wa'][j][k]), _block_diag(p['rg_wx'][j][k])], axis=1)
                            for k in range(2)]).astype(BF16)
            bg = jnp.concatenate([p['rg_ba'][j], p['rg_bx'][j]], axis=1)[:, None, :]
            cl = (-RG_C * jax.nn.softplus(-p['rg_lambda'][j]))[:, None, :]
            hf, hb = _rglru(z, p['rg_conv_w'][j], p['rg_conv_b'][j][None, :], wg, bg, cl)
            wuq = _pad_last(p['mla_w_uq'][j].reshape(q_lora, MLA_HEADS, QK_DIM), HEAD_SLAB)
            wuq = wuq.reshape(q_lora, MLA_HEADS * HEAD_SLAB).astype(BF16)
            wkv = p['mla_w_ukv'][j].reshape(kv_lora, MLA_HEADS, QK_NOPE + V_DIM)
            wk = _pad_last(wkv[:, :, :QK_NOPE], HEAD_SLAB).reshape(kv_lora, MLA_HEADS * HEAD_SLAB).astype(BF16)
            wv = wkv[:, :, QK_NOPE:].reshape(kv_lora, MLA_HEADS * V_DIM).astype(BF16)
            gq = _pad_last(p['mla_qn_q'][j][None, :], HEAD_SLAB)
            gk = _pad_last(p['mla_qn_k'][j][None, :], HEAD_SLAB)
            rc, r1, r2 = _rope_tables(s)
            width = q_lora + kv_lora + HEAD_SLAB
            assert (2 * d_rnn) % width == 0
            bound = 1.02 * QK_DIM ** 0.5 * jnp.max(jnp.abs(gq)) * jnp.max(jnp.abs(gk))
            spare = jnp.arange(HEAD_SLAB)[None, :] == HEAD_SLAB - 1
            qb = jnp.where(spare, 1.0, 0.0).astype(F32)
            kb = jnp.where(spare, -bound * LOG2_E, 0.0).astype(F32)
            q, k, v = _mla_proj(z, 2 * d_rnn // width, p['mla_q_norm'][j][None, :], p['mla_kv_norm'][j][None, :],
                                wuq, wk, wv, gq, gk, qb, kb, rc, r1, r2)
            attn = lax.cond(bound < MAX_SOFTMAX_BOUND, functools.partial(_attention, bounded=True),
                            functools.partial(_attention, bounded=False), q, k, v)
            x = _ab_out(x, hf, hb, z, attn, p['ab_w_out'][j].astype(BF16), g1)
        else:
            bgate, cx = _c_in(x, n1, sc1, sh1, p['c_w_in'][j].astype(BF16))
            x = _c_out(x, bgate, cx, p['c_conv_w'][j], p['c_w_out'][j].astype(BF16), g1)
        x = _peer(x, p['norm2_g'][i][None, :], sc2, sh2, g2, p['peer_wq'][i].astype(BF16),
                  p['peer_k1'][i].astype(BF16), p['peer_k2'][i].astype(BF16),
                  _pack_table(p['peer_u'][i]), _pack_table(p['peer_v'][i]))
    return x


def kernel(x_prompt, x_sample, c_prompt, c_sample, ada_w, ada_b, norm1_g, norm2_g, ab_w_in, rg_conv_w, rg_conv_b, rg_wa, rg_ba, rg_wx, rg_bx, rg_lambda, mla_q_norm, mla_w_uq, mla_kv_norm, mla_w_ukv, mla_qn_q, mla_qn_k, ab_w_out, c_w_in, c_conv_w, c_w_out, peer_wq, peer_k1, peer_k2, peer_u, peer_v):
    p = dict(ada_w=ada_w, norm1_g=norm1_g, norm2_g=norm2_g, ab_w_in=ab_w_in, rg_conv_w=rg_conv_w,
             rg_conv_b=rg_conv_b, rg_wa=rg_wa, rg_ba=rg_ba, rg_wx=rg_wx, rg_bx=rg_bx, rg_lambda=rg_lambda,
             mla_q_norm=mla_q_norm, mla_w_uq=mla_w_uq, mla_kv_norm=mla_kv_norm, mla_w_ukv=mla_w_ukv,
             mla_qn_q=mla_qn_q, mla_qn_k=mla_qn_k, ab_w_out=ab_w_out, c_w_in=c_w_in, c_conv_w=c_conv_w,
             c_w_out=c_w_out, peer_wq=peer_wq, peer_k1=peer_k1, peer_k2=peer_k2, peer_u=peer_u, peer_v=peer_v)
    bp, bs = c_prompt.shape[0], c_sample.shape[0]
    rows = -(-(bp + bs) // SUBLANES) * SUBLANES
    c_all = jnp.pad(jnp.concatenate([c_prompt, c_sample], axis=0), ((0, rows - bp - bs), (0, 0)))
    mod = _modulation(c_all, ada_w, ada_b)
    y_prompt = _trunk(x_prompt, mod[:, :bp], p)
    y_sample = _trunk(x_sample, mod[:, bp:bp + bs], p)
    return (y_prompt, y_sample)
```

```python
import functools

import jax
import jax.numpy as jnp
from jax import lax
from jax.experimental import pallas as pl
from jax.experimental.pallas import tpu as pltpu

F32 = jnp.float32
BF16 = jnp.bfloat16
EPS = 1e-6
LOG2_E = 1.4426950408889634

RG_HEADS = 8
RG_CONV = 4
RG_C = 8.0
MLA_HEADS = 8
QK_NOPE = 64
QK_ROPE = 32
V_DIM = 64
QK_DIM = QK_NOPE + QK_ROPE
ROPE_THETA = 10000.0
PEER_HEADS = 8
PEER_TOPK = 16
HEAD_SLAB = 128
MAX_SOFTMAX_BOUND = 40.0

LANES = 128
SUBLANES = 8
VMEM_LIMIT = 56 * 1024 * 1024

ROW_TILE = 512
SCAN_CHUNK = 1024
ATTN_TQ = 1024
ATTN_TK = 1024
ROUTE_TILE = 256
PEER_TB = 16


def _cparams(sem):
    return pltpu.CompilerParams(dimension_semantics=sem, vmem_limit_bytes=VMEM_LIMIT)


def _norm_mod(x, gain, sc, sh):
    ms = jnp.mean(x * x, axis=-1, keepdims=True)
    return x * lax.rsqrt(ms + EPS) * gain * (1.0 + sc) + sh


def _rms(x, gain, n):
    ms = jnp.sum(x * x, axis=-1, keepdims=True) * (1.0 / n)
    return x * lax.rsqrt(ms + EPS) * gain


def _bdot(a, b):
    return jnp.dot(a.astype(BF16), b, preferred_element_type=F32)


def _mod_kernel(c_ref, w_ref, b_ref, o_ref):
    c = c_ref[...]
    s = c * jax.nn.sigmoid(c)
    o_ref[0] = _bdot(s, w_ref[0].astype(BF16)) + b_ref[0]


def _modulation(c_all, ada_w, ada_b):
    depth, d, n = ada_w.shape
    rows = c_all.shape[0]
    tn = 1536
    return pl.pallas_call(
        _mod_kernel,
        grid=(depth, n // tn),
        in_specs=[
            pl.BlockSpec((rows, d), lambda i, j: (0, 0)),
            pl.BlockSpec((1, d, tn), lambda i, j: (i, 0, j)),
            pl.BlockSpec((1, 1, tn), lambda i, j: (i, 0, j)),
        ],
        out_specs=pl.BlockSpec((1, rows, tn), lambda i, j: (i, 0, j)),
        out_shape=jax.ShapeDtypeStruct((depth, rows, n), F32),
        compiler_params=_cparams(("parallel", "parallel")),
        name="modulation",
    )(c_all, ada_w, ada_b.reshape(depth, 1, n))


def _in_proj_kernel(x_ref, g_ref, sc_ref, sh_ref, w_ref, z_ref):
    h = _norm_mod(x_ref[0], g_ref[...], sc_ref[0], sh_ref[0])
    z_ref[0] = _bdot(h, w_ref[...])


def _in_proj(x, gain, sc, sh, w):
    b, s, d = x.shape
    n = w.shape[1]
    tm = min(ROW_TILE, s)
    return pl.pallas_call(
        _in_proj_kernel,
        grid=(b, s // tm),
        in_specs=[
            pl.BlockSpec((1, tm, d), lambda i, j: (i, j, 0)),
            pl.BlockSpec((1, d), lambda i, j: (0, 0)),
            pl.BlockSpec((1, 1, d), lambda i, j: (i, 0, 0)),
            pl.BlockSpec((1, 1, d), lambda i, j: (i, 0, 0)),
            pl.BlockSpec((d, n), lambda i, j: (0, 0)),
        ],
        out_specs=pl.BlockSpec((1, tm, n), lambda i, j: (i, j, 0)),
        out_shape=jax.ShapeDtypeStruct((b, s, n), F32),
        compiler_params=_cparams(("parallel", "parallel")),
        name="in_proj",
    )(x, gain, sc, sh, w)


def _rglru_kernel(xf_ref, xfp_ref, xfn_ref, xb_ref, xbp_ref, xbn_ref, cw_ref, cb_ref, wg_ref, bg_ref, cl_ref,
                  hf_ref, hb_ref, ext_ref, a_ref, b_ref, carry_ref):
    j = pl.program_id(1)
    nc = pl.num_programs(1)
    tc = xf_ref.shape[1]
    dr = xf_ref.shape[2]
    nt = tc // SUBLANES

    @pl.when(j == 0)
    def _():
        carry_ref[...] = jnp.zeros_like(carry_ref)

    def gates(x_ref, xp_ref, xn_ref, first, last, d):
        ext_ref[0:SUBLANES] = jnp.where(first, 0.0, xp_ref[0])
        ext_ref[SUBLANES:SUBLANES + tc] = x_ref[0]
        ext_ref[SUBLANES + tc:2 * SUBLANES + tc] = jnp.where(last, 0.0, xn_ref[0])
        xc = cb_ref[...]
        for k in range(RG_CONV):
            xc = xc + cw_ref[k:k + 1, :] * ext_ref[SUBLANES - 2 + k:SUBLANES - 2 + k + tc]
        g = _bdot(xc, wg_ref[d]) + bg_ref[d]
        r = jax.nn.sigmoid(g[:, :dr])
        i = jax.nn.sigmoid(g[:, dr:])
        log_a = r * cl_ref[d]
        a = jnp.exp(log_a)
        b = jnp.sqrt(-jnp.tanh(log_a) * (1.0 + a * a)) * (i * xc)
        a_ref[d] = a
        b_ref[d] = b

    rows = lax.broadcasted_iota(jnp.int32, (SUBLANES, dr), 0)

    def scan(d, reverse, out_ref):
        def body(it, carry):
            t = (nt - 1 - it) if reverse else it
            off = pl.multiple_of(t * SUBLANES, SUBLANES)
            a = a_ref[d, pl.ds(off, SUBLANES), :]
            b = b_ref[d, pl.ds(off, SUBLANES), :]
            for s in (1, 2, 4):
                if reverse:
                    a_s = pltpu.roll(a, SUBLANES - s, 0)
                    b_s = pltpu.roll(b, SUBLANES - s, 0)
                    m = rows < SUBLANES - s
                else:
                    a_s = pltpu.roll(a, s, 0)
                    b_s = pltpu.roll(b, s, 0)
                    m = rows >= s
                b = jnp.where(m, a * b_s + b, b)
                a = jnp.where(m, a * a_s, a)
            h = b + a * carry
            out_ref[0, pl.ds(off, SUBLANES), :] = h
            edge = h[0:1] if reverse else h[SUBLANES - 1:SUBLANES]
            return jnp.broadcast_to(edge, (SUBLANES, dr))

        carry_ref[d] = lax.fori_loop(0, nt, body, carry_ref[d])

    gates(xf_ref, xfp_ref, xfn_ref, j == 0, j == nc - 1, 0)
    scan(0, False, hf_ref)
    gates(xb_ref, xbp_ref, xbn_ref, j == nc - 1, j == 0, 1)
    scan(1, True, hb_ref)


def _rglru(z, conv_w, conv_b, wg, bg, cl):
    b, s, _ = z.shape
    dr = conv_w.shape[1]
    tc = min(SCAN_CHUNK, s)
    nc = s // tc
    nb8 = s // SUBLANES
    cb8 = tc // SUBLANES

    def main(rev):
        return pl.BlockSpec((1, tc, dr), (lambda i, j: (i, nc - 1 - j, 0)) if rev else (lambda i, j: (i, j, 0)))

    def prev(rev):
        def f(i, j):
            c = (nc - 1 - j) if rev else j
            return (i, jnp.maximum(c * cb8 - 1, 0), 0)
        return pl.BlockSpec((1, SUBLANES, dr), f)

    def nxt(rev):
        def f(i, j):
            c = (nc - 1 - j) if rev else j
            return (i, jnp.minimum((c + 1) * cb8, nb8 - 1), 0)
        return pl.BlockSpec((1, SUBLANES, dr), f)

    def const(shape):
        return pl.BlockSpec(shape, lambda i, j: (0,) * len(shape))

    out_sd = jax.ShapeDtypeStruct((b, s, dr), F32)
    return pl.pallas_call(
        _rglru_kernel,
        grid=(b, nc),
        in_specs=[main(False), prev(False), nxt(False), main(True), prev(True), nxt(True),
                  const((RG_CONV, dr)), const((1, dr)), const((2, dr, 2 * dr)), const((2, 1, 2 * dr)),
                  const((2, 1, dr))],
        out_specs=[pl.BlockSpec((1, tc, dr), lambda i, j: (i, j, 0)),
                   pl.BlockSpec((1, tc, dr), lambda i, j: (i, nc - 1 - j, 0))],
        out_shape=[out_sd, out_sd],
        scratch_shapes=[pltpu.VMEM((tc + 2 * SUBLANES, dr), F32), pltpu.VMEM((2, tc, dr), F32),
                        pltpu.VMEM((2, tc, dr), F32), pltpu.VMEM((2, SUBLANES, dr), F32)],
        compiler_params=_cparams(("parallel", "arbitrary")),
        name="rglru",
    )(z, z, z, z, z, z, conv_w, conv_b, wg, bg, cl)


def _mla_proj_kernel(z_ref, qn_ref, kvn_ref, wuq_ref, wk_ref, wv_ref, gq_ref, gk_ref, qb_ref, kb_ref,
                     rc_ref, r1_ref, r2_ref, q_ref, k_ref, v_ref):
    zz = z_ref[0]
    q_lora = qn_ref.shape[1]
    kv_lora = kvn_ref.shape[1]
    ql = zz[:, :q_lora]
    kvl = zz[:, q_lora:q_lora + kv_lora]
    kr = zz[:, q_lora + kv_lora:]
    q = _bdot(_rms(ql, qn_ref[...], q_lora), wuq_ref[...])
    kvn = _rms(kvl, kvn_ref[...], kv_lora)
    kk = _bdot(kvn, wk_ref[...])
    vv = _bdot(kvn, wv_ref[...])
    rc, r1, r2 = rc_ref[...], r1_ref[...], r2_ref[...]
    half = QK_ROPE // 2
    scale = QK_DIM ** -0.5 * LOG2_E

    def norm_rope(xh, g):
        xh = _rms(xh, g, QK_DIM)
        return xh * rc + pltpu.roll(xh, HEAD_SLAB - half, 1) * r1 + pltpu.roll(xh, half, 1) * r2

    for h in range(MLA_HEADS):
        sl = slice(h * HEAD_SLAB, (h + 1) * HEAD_SLAB)
        q_ref[0, h] = (norm_rope(q[:, sl], gq_ref[...]) * scale + qb_ref[...]).astype(BF16)
        k_ref[0, h] = (norm_rope(kk[:, sl] + kr, gk_ref[...]) + kb_ref[...]).astype(BF16)
    for p in range(MLA_HEADS // 2):
        v_ref[0, p] = vv[:, p * LANES:(p + 1) * LANES].astype(BF16)


def _mla_proj(z, col_block, qn, kvn, wuq, wk, wv, gq, gk, qb, kb, rc, r1, r2):
    b, s, _ = z.shape
    tm = min(ROW_TILE, s)
    width = qn.shape[1] + kvn.shape[1] + HEAD_SLAB

    def const(a):
        return pl.BlockSpec(a.shape, lambda i, j: (0,) * a.ndim)

    def rope(a):
        return pl.BlockSpec((tm, HEAD_SLAB), lambda i, j: (j, 0))

    hp = MLA_HEADS // 2
    return pl.pallas_call(
        _mla_proj_kernel,
        grid=(b, s // tm),
        in_specs=[pl.BlockSpec((1, tm, width), lambda i, j: (i, j, col_block)),
                  const(qn), const(kvn), const(wuq), const(wk), const(wv), const(gq), const(gk), const(qb), const(kb),
                  rope(rc), rope(r1), rope(r2)],
        out_specs=[pl.BlockSpec((1, MLA_HEADS, tm, HEAD_SLAB), lambda i, j: (i, 0, j, 0)),
                   pl.BlockSpec((1, MLA_HEADS, tm, HEAD_SLAB), lambda i, j: (i, 0, j, 0)),
                   pl.BlockSpec((1, hp, tm, LANES), lambda i, j: (i, 0, j, 0))],
        out_shape=[jax.ShapeDtypeStruct((b, MLA_HEADS, s, HEAD_SLAB), BF16),
                   jax.ShapeDtypeStruct((b, MLA_HEADS, s, HEAD_SLAB), BF16),
                   jax.ShapeDtypeStruct((b, hp, s, LANES), BF16)],
        compiler_params=_cparams(("parallel", "parallel")),
        name="mla_proj",
    )(z, qn, kvn, wuq, wk, wv, gq, gk, qb, kb, rc, r1, r2)


def _flash_kernel(q_ref, k_ref, v_ref, o_ref, m_ref, l_ref, acc_ref):
    ik = pl.program_id(3)

    @pl.when(ik == 0)
    def _():
        m_ref[...] = jnp.full_like(m_ref, -jnp.inf)
        l_ref[...] = jnp.zeros_like(l_ref)
        acc_ref[...] = jnp.zeros_like(acc_ref)

    v = v_ref[0, 0]
    for hh in range(2):
        s = lax.dot_general(q_ref[0, hh], k_ref[0, hh], (((1,), (1,)), ((), ())), preferred_element_type=F32)
        m_prev = m_ref[hh]
        m_new = jnp.maximum(m_prev, jnp.max(s, axis=-1, keepdims=True))
        alpha = jnp.exp2(m_prev - m_new)
        p = jnp.exp2(s - m_new[:, :1])
        l_ref[hh] = alpha * l_ref[hh] + jnp.sum(p, axis=-1, keepdims=True)
        acc_ref[hh] = alpha * acc_ref[hh] + jnp.dot(p.astype(BF16), v, preferred_element_type=F32)
        m_ref[hh] = m_new

    @pl.when(ik == pl.num_programs(3) - 1)
    def _():
        lane = lax.broadcasted_iota(jnp.int32, acc_ref.shape[1:], 1)
        o_ref[0] = jnp.where(lane < V_DIM, acc_ref[0] / l_ref[0], acc_ref[1] / l_ref[1])


def _flash_bounded_kernel(q_ref, k_ref, v_ref, o_ref, l_ref, acc_ref):
    ik = pl.program_id(3)

    @pl.when(ik == 0)
    def _():
        l_ref[...] = jnp.zeros_like(l_ref)
        acc_ref[...] = jnp.zeros_like(acc_ref)

    v = v_ref[0, 0]
    for hh in range(2):
        s = lax.dot_general(q_ref[0, hh], k_ref[0, hh], (((1,), (1,)), ((), ())), preferred_element_type=F32)
        p = jnp.exp2(s)
        l_ref[hh] = l_ref[hh] + jnp.sum(p, axis=-1, keepdims=True)
        acc_ref[hh] = acc_ref[hh] + jnp.dot(p.astype(BF16), v, preferred_element_type=F32)

    @pl.when(ik == pl.num_programs(3) - 1)
    def _():
        lane = lax.broadcasted_iota(jnp.int32, acc_ref.shape[1:], 1)
        o_ref[0] = jnp.where(lane < V_DIM, acc_ref[0] / l_ref[0], acc_ref[1] / l_ref[1])


def _attention(q, k, v, bounded):
    b, h, s, _ = q.shape
    tq = min(ATTN_TQ, s)
    tk = min(ATTN_TK, s)
    hp = h // 2
    return pl.pallas_call(
        _flash_bounded_kernel if bounded else _flash_kernel,
        grid=(b, hp, s // tq, s // tk),
        in_specs=[pl.BlockSpec((1, 2, tq, HEAD_SLAB), lambda i, p, a, c: (i, p, a, 0)),
                  pl.BlockSpec((1, 2, tk, HEAD_SLAB), lambda i, p, a, c: (i, p, c, 0)),
                  pl.BlockSpec((1, 1, tk, LANES), lambda i, p, a, c: (i, p, c, 0))],
        out_specs=pl.BlockSpec((1, tq, LANES), lambda i, p, a, c: (i, a, p)),
        out_shape=jax.ShapeDtypeStruct((b, s, hp * LANES), F32),
        scratch_shapes=[pltpu.VMEM((2, tq, LANES), F32)] * (2 if bounded else 3),
        compiler_params=_cparams(("parallel", "parallel", "parallel", "arbitrary")),
        name="attention_bounded" if bounded else "attention",
    )(q, k, v)


def _ab_out_kernel(x_ref, hf_ref, hb_ref, y_ref, at_ref, w_ref, g_ref, o_ref):
    rg = (hf_ref[0] + hb_ref[0]) * jax.nn.gelu(y_ref[0])
    cat = jnp.concatenate([rg.astype(BF16), at_ref[0].astype(BF16)], axis=-1)
    m = jnp.dot(cat, w_ref[...], preferred_element_type=F32)
    o_ref[0] = x_ref[0] + g_ref[0] * m


def _ab_out(x, hf, hb, z, attn, w, gate):
    b, s, d = x.shape
    dr = hf.shape[2]
    da = attn.shape[2]
    tm = min(ROW_TILE, s)
    return pl.pallas_call(
        _ab_out_kernel,
        grid=(b, s // tm),
        in_specs=[pl.BlockSpec((1, tm, d), lambda i, j: (i, j, 0)),
                  pl.BlockSpec((1, tm, dr), lambda i, j: (i, j, 0)),
                  pl.BlockSpec((1, tm, dr), lambda i, j: (i, j, 0)),
                  pl.BlockSpec((1, tm, dr), lambda i, j: (i, j, 1)),
                  pl.BlockSpec((1, tm, da), lambda i, j: (i, j, 0)),
                  pl.BlockSpec(w.shape, lambda i, j: (0, 0)),
                  pl.BlockSpec((1, 1, d), lambda i, j: (i, 0, 0))],
        out_specs=pl.BlockSpec((1, tm, d), lambda i, j: (i, j, 0)),
        out_shape=jax.ShapeDtypeStruct((b, s, d), F32),
        compiler_params=_cparams(("parallel", "parallel")),
        name="ab_out",
    )(x, hf, hb, z, attn, w, gate)


def _c_in_kernel(x_ref, g_ref, sc_ref, sh_ref, w_ref, bg_ref, cx_ref):
    h = _norm_mod(x_ref[0], g_ref[...], sc_ref[0], sh_ref[0])
    z = _bdot(h, w_ref[...])
    dc = bg_ref.shape[2]
    bg_ref[0] = z[:, :dc]
    cx_ref[0] = z[:, dc:2 * dc] * z[:, 2 * dc:]


def _c_in(x, gain, sc, sh, w):
    b, s, d = x.shape
    dc = w.shape[1] // 3
    tm = min(ROW_TILE, s)
    sd = jax.ShapeDtypeStruct((b, s, dc), F32)
    return pl.pallas_call(
        _c_in_kernel,
        grid=(b, s // tm),
        in_specs=[pl.BlockSpec((1, tm, d), lambda i, j: (i, j, 0)),
                  pl.BlockSpec((1, d), lambda i, j: (0, 0)),
                  pl.BlockSpec((1, 1, d), lambda i, j: (i, 0, 0)),
                  pl.BlockSpec((1, 1, d), lambda i, j: (i, 0, 0)),
                  pl.BlockSpec(w.shape, lambda i, j: (0, 0))],
        out_specs=[pl.BlockSpec((1, tm, dc), lambda i, j: (i, j, 0))] * 2,
        out_shape=[sd, sd],
        compiler_params=_cparams(("parallel", "parallel")),
        name="c_in",
    )(x, gain, sc, sh, w)


def _c_out_kernel(x_ref, bg_ref, cx_ref, cp_ref, cn_ref, cw_ref, w_ref, g_ref, o_ref, ext_ref):
    j = pl.program_id(1)
    tm = cx_ref.shape[1]
    ext_ref[0:SUBLANES] = jnp.where(j == 0, 0.0, cp_ref[0])
    ext_ref[SUBLANES:SUBLANES + tm] = cx_ref[0]
    ext_ref[SUBLANES + tm:2 * SUBLANES + tm] = jnp.where(j == pl.num_programs(1) - 1, 0.0, cn_ref[0])
    conv = cw_ref[0:1, :] * ext_ref[SUBLANES - 1:SUBLANES - 1 + tm]
    conv = conv + cw_ref[1:2, :] * ext_ref[SUBLANES:SUBLANES + tm]
    conv = conv + cw_ref[2:3, :] * ext_ref[SUBLANES + 1:SUBLANES + 1 + tm]
    m = _bdot(bg_ref[0] * conv, w_ref[...])
    o_ref[0] = x_ref[0] + g_ref[0] * m


def _c_out(x, bg, cx, conv_w, w, gate):
    b, s, d = x.shape
    dc = bg.shape[2]
    tm = min(ROW_TILE, s)
    nb8 = s // SUBLANES
    t8 = tm // SUBLANES
    return pl.pallas_call(
        _c_out_kernel,
        grid=(b, s // tm),
        in_specs=[pl.BlockSpec((1, tm, d), lambda i, j: (i, j, 0)),
                  pl.BlockSpec((1, tm, dc), lambda i, j: (i, j, 0)),
                  pl.BlockSpec((1, tm, dc), lambda i, j: (i, j, 0)),
                  pl.BlockSpec((1, SUBLANES, dc), lambda i, j: (i, jnp.maximum(j * t8 - 1, 0), 0)),
                  pl.BlockSpec((1, SUBLANES, dc), lambda i, j: (i, jnp.minimum((j + 1) * t8, nb8 - 1), 0)),
                  pl.BlockSpec(conv_w.shape, lambda i, j: (0, 0)),
                  pl.BlockSpec(w.shape, lambda i, j: (0, 0)),
                  pl.BlockSpec((1, 1, d), lambda i, j: (i, 0, 0))],
        out_specs=pl.BlockSpec((1, tm, d), lambda i, j: (i, j, 0)),
        out_shape=jax.ShapeDtypeStruct((b, s, d), F32),
        scratch_shapes=[pltpu.VMEM((tm + 2 * SUBLANES, dc), F32)],
        compiler_params=_cparams(("parallel", "parallel")),
        name="c_out",
    )(x, bg, cx, cx, cx, conv_w, w, gate)


def _top_rows(s, k, payload=None):
    n = s.shape[0]
    ridx = lax.broadcasted_iota(jnp.int32, s.shape, 0).astype(F32)
    vals, ids = [], []
    for _ in range(k):
        m = jnp.max(s, axis=0, keepdims=True)
        first = jnp.min(jnp.where(s == m, ridx, float(n)), axis=0, keepdims=True)
        sel = ridx == first
        ids.append(first if payload is None else jnp.sum(jnp.where(sel, payload, 0.0), axis=0, keepdims=True))
        vals.append(m)
        s = jnp.where(sel, -jnp.inf, s)
    return vals, ids


_PAIRS = [(a, b) for a in range(PEER_TOPK) for b in range(PEER_TOPK) if (a + 1) * (b + 1) <= PEER_TOPK]


def _route_kernel(x_ref, g_ref, sc_ref, sh_ref, wq_ref, k1_ref, k2_ref, h_ref, idx_ref, gate_ref, q_scr):
    h = _norm_mod(x_ref[0], g_ref[...], sc_ref[0], sh_ref[0])
    h_ref[0] = h
    q_scr[...] = _bdot(h, wq_ref[...])
    n_keys = k1_ref.shape[0]
    half = k1_ref.shape[1]
    tm = x_ref.shape[1]
    nt = (((1,), (1,)), ((), ()))
    pad = -len(_PAIRS) % SUBLANES

    def head(hd, carry):
        off = pl.multiple_of(hd * 2 * half, 2 * half)
        q1 = q_scr[:, pl.ds(off, half)].astype(BF16)
        q2 = q_scr[:, pl.ds(off + half, half)].astype(BF16)
        s1 = lax.dot_general(k1_ref[...], q1, nt, preferred_element_type=F32)
        s2 = lax.dot_general(k2_ref[...], q2, nt, preferred_element_type=F32)
        v1, i1 = _top_rows(s1, PEER_TOPK)
        v2, i2 = _top_rows(s2, PEER_TOPK)
        cv = [v1[a] + v2[b] for a, b in _PAIRS] + [jnp.full((pad, tm), -jnp.inf, F32)]
        ce = [i1[a] * float(n_keys) + i2[b] for a, b in _PAIRS] + [jnp.zeros((pad, tm), F32)]
        vs, es = _top_rows(jnp.concatenate(cv, axis=0), PEER_TOPK, jnp.concatenate(ce, axis=0))
        vs = jnp.concatenate(vs, axis=0)
        e = jnp.exp(vs - vs[0:1])
        row = pl.multiple_of(hd * PEER_TOPK, PEER_TOPK)
        gate_ref[0, pl.ds(row, PEER_TOPK), :] = e / jnp.sum(e, axis=0, keepdims=True)
        idx_ref[0, pl.ds(row, PEER_TOPK), :] = jnp.concatenate(es, axis=0).astype(jnp.int32)
        return carry

    lax.fori_loop(0, PEER_HEADS, head, 0)


def _route(x, gain, sc, sh, wq, k1, k2):
    b, s, d = x.shape
    tm = min(ROUTE_TILE, s)
    nsel = PEER_HEADS * PEER_TOPK
    return pl.pallas_call(
        _route_kernel,
        grid=(b, s // tm),
        in_specs=[pl.BlockSpec((1, tm, d), lambda i, j: (i, j, 0)),
                  pl.BlockSpec((1, d), lambda i, j: (0, 0)),
                  pl.BlockSpec((1, 1, d), lambda i, j: (i, 0, 0)),
                  pl.BlockSpec((1, 1, d), lambda i, j: (i, 0, 0)),
                  pl.BlockSpec(wq.shape, lambda i, j: (0, 0)),
                  pl.BlockSpec(k1.shape, lambda i, j: (0, 0)),
                  pl.BlockSpec(k2.shape, lambda i, j: (0, 0))],
        out_specs=[pl.BlockSpec((1, tm, d), lambda i, j: (i, j, 0)),
                   pl.BlockSpec((1, nsel, tm), lambda i, j: (i, 0, j)),
                   pl.BlockSpec((1, nsel, tm), lambda i, j: (i, 0, j))],
        out_shape=[jax.ShapeDtypeStruct((b, s, d), F32),
                   jax.ShapeDtypeStruct((b, nsel, s), jnp.int32),
                   jax.ShapeDtypeStruct((b, nsel, s), F32)],
        scratch_shapes=[pltpu.VMEM((tm, wq.shape[1]), F32)],
        compiler_params=_cparams(("parallel", "parallel")),
        name="peer_route",
    )(x, gain, sc, sh, wq, k1, k2)


def _pack_table(t):
    n, d = t.shape
    chunks = d // LANES
    bits = lax.bitcast_convert_type(t.astype(BF16), jnp.uint16).astype(jnp.uint32)
    bits = bits.reshape(n * chunks // 2, 2, LANES)
    return lax.bitcast_convert_type(bits[:, 0, :] | (bits[:, 1, :] << 16), jnp.int32)


def _gather_rows(idx_ref, t, tbl_ref, wr):
    rows = [tbl_ref[pl.ds(pl.multiple_of(idx_ref[t, r], wr), wr), :] for r in range(idx_ref.shape[1])]
    return pltpu.bitcast(jnp.concatenate(rows, axis=0), BF16)


def _idx_copy(idx_hbm, half_block, smem, sem):
    sub = smem.shape[0]
    return pltpu.make_async_copy(idx_hbm.at[pl.ds(half_block * sub, sub)], smem, sem)


def _staged_halves(idx_hbm, smem_a, smem_b, sems, process):
    i = pl.program_id(0)
    sub = smem_a.shape[0]

    @pl.when(i == 0)
    def _():
        _idx_copy(idx_hbm, 0, smem_a, sems.at[0]).start()

    _idx_copy(idx_hbm, 2 * i, smem_a, sems.at[0]).wait()
    _idx_copy(idx_hbm, 2 * i + 1, smem_b, sems.at[1]).start()
    process(smem_a, 0)
    _idx_copy(idx_hbm, 2 * i + 1, smem_b, sems.at[1]).wait()

    @pl.when(i + 1 < pl.num_programs(0))
    def _():
        _idx_copy(idx_hbm, 2 * i + 2, smem_a, sems.at[0]).start()

    process(smem_b, sub)


def _split_bf16(a):
    hi = a.astype(BF16).astype(F32)
    return jnp.concatenate([hi, a - hi], axis=0).astype(BF16)


def _chunk_mask(chunks, width):
    lane = lax.broadcasted_iota(jnp.int32, (chunks, width), 1)
    return lane % chunks == lax.broadcasted_iota(jnp.int32, (chunks, width), 0)


def _peer_u_kernel(idx_hbm, x_ref, g_ref, tbl_ref, bsum_ref, act_ref, smem_a, smem_b, sems, y_scr, x_scr):
    tb = act_ref.shape[0]
    chunks = x_ref.shape[1] // LANES
    width = act_ref.shape[1]
    mask = _chunk_mask(chunks, width)
    nt = (((1,), (1,)), ((), ()))
    for c in range(chunks):
        x_scr[pl.ds(c, tb, stride=chunks), :] = x_ref[:, c * LANES:(c + 1) * LANES]

    def process(idx_ref, first):
        for t in range(idx_ref.shape[0]):
            gb = _gather_rows(idx_ref, t, tbl_ref, chunks // 2)
            xt = x_scr[(first + t) * chunks:(first + t + 1) * chunks, :]
            o = lax.dot_general(_split_bf16(xt), gb, nt, preferred_element_type=F32)
            o = o[:chunks] + o[chunks:]
            y_scr[first + t:first + t + 1, :] = jnp.sum(jnp.where(mask, o, 0.0), axis=0, keepdims=True)

    _staged_halves(idx_hbm, smem_a, smem_b, sems, process)
    y = _split_bf16(y_scr[...])
    s = jnp.dot(y, bsum_ref[...], preferred_element_type=F32)
    act_ref[...] = jax.nn.gelu(s[:tb] + s[tb:]) * g_ref[...]


def _peer_blocks(t):
    sub = min(PEER_TB, t // 2)
    assert t % (2 * sub) == 0
    return sub, 2 * sub


def _idx_scratch(sub, nsel):
    return [pltpu.SMEM((sub, nsel), jnp.int32), pltpu.SMEM((sub, nsel), jnp.int32), pltpu.SemaphoreType.DMA((2,))]


def _peer_u(idx, x, gates, table, bsum):
    t, nsel = idx.shape
    d = x.shape[1]
    chunks = d // LANES
    width = nsel * chunks
    sub, tb = _peer_blocks(t)
    return pl.pallas_call(
        _peer_u_kernel,
        grid=(t // tb,),
        in_specs=[pl.BlockSpec(memory_space=pl.ANY),
                  pl.BlockSpec((tb, d), lambda i: (i, 0)),
                  pl.BlockSpec((tb, width), lambda i: (i, 0)),
                  pl.BlockSpec(table.shape, lambda i: (0, 0), pipeline_mode=pl.Buffered(1)),
                  pl.BlockSpec(bsum.shape, lambda i: (0, 0), pipeline_mode=pl.Buffered(1))],
        out_specs=pl.BlockSpec((tb, width), lambda i: (i, 0)),
        out_shape=jax.ShapeDtypeStruct((t, width), F32),
        scratch_shapes=_idx_scratch(sub, nsel) + [pltpu.VMEM((tb, width), F32), pltpu.VMEM((tb * chunks, LANES), F32)],
        compiler_params=_cparams(("arbitrary",)),
        name="peer_u",
    )(idx, x, gates, table, bsum)


def _peer_v_kernel(idx_hbm, act_ref, tbl_ref, x_ref, g_ref, o_ref, smem_a, smem_b, sems, o_scr):
    tb = x_ref.shape[0]
    chunks = x_ref.shape[1] // LANES
    width = act_ref.shape[1]
    mask = _chunk_mask(chunks, width)

    def process(idx_ref, first):
        for t in range(idx_ref.shape[0]):
            gb = _gather_rows(idx_ref, t, tbl_ref, chunks // 2)
            a = jnp.where(mask, jnp.broadcast_to(act_ref[first + t:first + t + 1, :], (chunks, width)), 0.0)
            o = jnp.dot(_split_bf16(a), gb, preferred_element_type=F32)
            o_scr[(first + t) * chunks:(first + t + 1) * chunks, :] = o[:chunks] + o[chunks:]

    _staged_halves(idx_hbm, smem_a, smem_b, sems, process)
    for c in range(chunks):
        sl = slice(c * LANES, (c + 1) * LANES)
        o_ref[:, sl] = x_ref[:, sl] + g_ref[0][:, sl] * o_scr[pl.ds(c, tb, stride=chunks), :]


def _peer_v(idx, act, table, x, gate, tokens_per_batch):
    t, nsel = idx.shape
    d = x.shape[1]
    chunks = d // LANES
    width = nsel * chunks
    sub, tb = _peer_blocks(t)
    assert tokens_per_batch % tb == 0
    return pl.pallas_call(
        _peer_v_kernel,
        grid=(t // tb,),
        in_specs=[pl.BlockSpec(memory_space=pl.ANY),
                  pl.BlockSpec((tb, width), lambda i: (i, 0)),
                  pl.BlockSpec(table.shape, lambda i: (0, 0), pipeline_mode=pl.Buffered(1)),
                  pl.BlockSpec((tb, d), lambda i: (i, 0)),
                  pl.BlockSpec((1, 1, d), lambda i: (i * tb // tokens_per_batch, 0, 0))],
        out_specs=pl.BlockSpec((tb, d), lambda i: (i, 0)),
        out_shape=jax.ShapeDtypeStruct(x.shape, F32),
        scratch_shapes=_idx_scratch(sub, nsel) + [pltpu.VMEM((tb * chunks, LANES), F32)],
        compiler_params=_cparams(("arbitrary",)),
        name="peer_v",
    )(idx, act, table, x, gate)


def _peer(x, gain, sc, sh, gate, wq, k1, k2, u_tbl, v_tbl):
    b, s, d = x.shape
    chunks = d // LANES
    h, idx, gts = _route(x, gain, sc, sh, wq, k1, k2)
    nsel = idx.shape[1]
    idx = jnp.transpose(idx, (0, 2, 1)).reshape(b * s, nsel) * (chunks // 2)
    gts = jnp.repeat(jnp.transpose(gts, (0, 2, 1)).reshape(b * s, nsel), chunks, axis=1)
    bsum = jnp.kron(jnp.eye(nsel, dtype=F32), jnp.ones((chunks, chunks), F32)).astype(BF16)
    act = _peer_u(idx, h.reshape(b * s, d), gts, u_tbl, bsum)
    out = _peer_v(idx, act, v_tbl, x.reshape(b * s, d), gate, s)
    return out.reshape(b, s, d)


def _block_diag(w):
    h, i, j = w.shape
    return jnp.einsum('hij,hg->higj', w, jnp.eye(h, dtype=w.dtype)).reshape(h * i, h * j)


def _rope_tables(s):
    half = QK_ROPE // 2
    inv = 1.0 / (ROPE_THETA ** (jnp.arange(0, QK_ROPE, 2, dtype=F32) / QK_ROPE))
    ang = jnp.arange(s, dtype=F32)[:, None] * inv[None, :]
    cos, sin = jnp.cos(ang), jnp.sin(ang)
    z = jnp.zeros((s, QK_NOPE), F32)
    tail = jnp.zeros((s, HEAD_SLAB - QK_DIM), F32)
    zh = jnp.zeros((s, half), F32)
    rc = jnp.concatenate([z + 1.0, cos, cos, tail + 1.0], axis=1)
    r1 = jnp.concatenate([z, -sin, zh, tail], axis=1)
    r2 = jnp.concatenate([z, zh, sin, tail], axis=1)
    return rc, r1, r2


def _pad_last(a, n):
    return jnp.pad(a, [(0, 0)] * (a.ndim - 1) + [(0, n - a.shape[-1])])


def _trunk(x, mod, p):
    b, s, d = x.shape
    depth = p['ada_w'].shape[0]
    d_rnn = p['rg_conv_w'].shape[2]
    q_lora = p['mla_q_norm'].shape[1]
    kv_lora = p['mla_kv_norm'].shape[1]
    for i in range(depth):
        sh1, sc1, g1, sh2, sc2, g2 = [m[:, None, :] for m in jnp.split(mod[i], 6, axis=-1)]
        j = i // 2
        n1 = p['norm1_g'][i][None, :]
        if i % 2 == 0:
            w_in = p['ab_w_in'][j]
            lat = 2 * d_rnn + q_lora + kv_lora
            w0 = jnp.concatenate([w_in[:, :lat], jnp.zeros((d, QK_NOPE), F32), w_in[:, lat:],
                                  jnp.zeros((d, HEAD_SLAB - QK_DIM), F32)], axis=1).astype(BF16)
            z = _in_proj(x, n1, sc1, sh1, w0)
            wg = jnp.stack([jnp.concatenate([_block_diag(p['rg_wa'][j][k]), _block_diag(p['rg_wx'][j][k])], axis=1)
                            for k in range(2)]).astype(BF16)
            bg = jnp.concatenate([p['rg_ba'][j], p['rg_bx'][j]], axis=1)[:, None, :]
            cl = (-RG_C * jax.nn.softplus(-p['rg_lambda'][j]))[:, None, :]
            hf, hb = _rglru(z, p['rg_conv_w'][j], p['rg_conv_b'][j][None, :], wg, bg, cl)
            wuq = _pad_last(p['mla_w_uq'][j].reshape(q_lora, MLA_HEADS, QK_DIM), HEAD_SLAB)
            wuq = wuq.reshape(q_lora, MLA_HEADS * HEAD_SLAB).astype(BF16)
            wkv = p['mla_w_ukv'][j].reshape(kv_lora, MLA_HEADS, QK_NOPE + V_DIM)
            wk = _pad_last(wkv[:, :, :QK_NOPE], HEAD_SLAB).reshape(kv_lora, MLA_HEADS * HEAD_SLAB).astype(BF16)
            wv = wkv[:, :, QK_NOPE:].reshape(kv_lora, MLA_HEADS * V_DIM).astype(BF16)
            gq = _pad_last(p['mla_qn_q'][j][None, :], HEAD_SLAB)
            gk = _pad_last(p['mla_qn_k'][j][None, :], HEAD_SLAB)
            rc, r1, r2 = _rope_tables(s)
            width = q_lora + kv_lora + HEAD_SLAB
            assert (2 * d_rnn) % width == 0
            bound = 1.02 * QK_DIM ** 0.5 * jnp.max(jnp.abs(gq)) * jnp.max(jnp.abs(gk))
            spare = jnp.arange(HEAD_SLAB)[None, :] == HEAD_SLAB - 1
            qb = jnp.where(spare, 1.0, 0.0).astype(F32)
            kb = jnp.where(spare, -bound * LOG2_E, 0.0).astype(F32)
            q, k, v = _mla_proj(z, 2 * d_rnn // width, p['mla_q_norm'][j][None, :], p['mla_kv_norm'][j][None, :],
                                wuq, wk, wv, gq, gk, qb, kb, rc, r1, r2)
            attn = lax.cond(bound < MAX_SOFTMAX_BOUND, functools.partial(_attention, bounded=True),
                            functools.partial(_attention, bounded=False), q, k, v)
            x = _ab_out(x, hf, hb, z, attn, p['ab_w_out'][j].astype(BF16), g1)
        else:
            bgate, cx = _c_in(x, n1, sc1, sh1, p['c_w_in'][j].astype(BF16))
            x = _c_out(x, bgate, cx, p['c_conv_w'][j], p['c_w_out'][j].astype(BF16), g1)
        x = _peer(x, p['norm2_g'][i][None, :], sc2, sh2, g2, p['peer_wq'][i].astype(BF16),
                  p['peer_k1'][i].astype(BF16), p['peer_k2'][i].astype(BF16),
                  _pack_table(p['peer_u'][i]), _pack_table(p['peer_v'][i]))
    return x


def kernel(x_prompt, x_sample, c_prompt, c_sample, ada_w, ada_b, norm1_g, norm2_g, ab_w_in, rg_conv_w, rg_conv_b, rg_wa, rg_ba, rg_wx, rg_bx, rg_lambda, mla_q_norm, mla_w_uq, mla_kv_norm, mla_w_ukv, mla_qn_q, mla_qn_k, ab_w_out, c_w_in, c_conv_w, c_w_out, peer_wq, peer_k1, peer_k2, peer_u, peer_v):
    p = dict(ada_w=ada_w, norm1_g=norm1_g, norm2_g=norm2_g, ab_w_in=ab_w_in, rg_conv_w=rg_conv_w,
             rg_conv_b=rg_conv_b, rg_wa=rg_wa, rg_ba=rg_ba, rg_wx=rg_wx, rg_bx=rg_bx, rg_lambda=rg_lambda,
             mla_q_norm=mla_q_norm, mla_w_uq=mla_w_uq, mla_kv_norm=mla_kv_norm, mla_w_ukv=mla_w_ukv,
             mla_qn_q=mla_qn_q, mla_qn_k=mla_qn_k, ab_w_out=ab_w_out, c_w_in=c_w_in, c_conv_w=c_conv_w,
             c_w_out=c_w_out, peer_wq=peer_wq, peer_k1=peer_k1, peer_k2=peer_k2, peer_u=peer_u, peer_v=peer_v)
    bp, bs = c_prompt.shape[0], c_sample.shape[0]
    rows = -(-(bp + bs) // SUBLANES) * SUBLANES
    c_all = jnp.pad(jnp.concatenate([c_prompt, c_sample], axis=0), ((0, rows - bp - bs), (0, 0)))
    mod = _modulation(c_all, ada_w, ada_b)
    y_prompt = _trunk(x_prompt, mod[:, :bp], p)
    y_sample = _trunk(x_sample, mod[:, bp:bp + bs], p)
    return (y_prompt, y_sample)
```

```python
import functools

import jax
import jax.numpy as jnp
from jax import lax
from jax.experimental import pallas as pl
from jax.experimental.pallas import tpu as pltpu
from jax.experimental.pallas import tpu_sc as plsc

F32 = jnp.float32
BF16 = jnp.bfloat16
EPS = 1e-6
LOG2_E = 1.4426950408889634

RG_HEADS = 8
RG_CONV = 4
RG_C = 8.0
MLA_HEADS = 8
QK_NOPE = 64
QK_ROPE = 32
V_DIM = 64
QK_DIM = QK_NOPE + QK_ROPE
ROPE_THETA = 10000.0
PEER_HEADS = 8
PEER_TOPK = 16
HEAD_SLAB = 128
MAX_SOFTMAX_BOUND = 40.0

LANES = 128
SUBLANES = 8
VMEM_LIMIT = 56 * 1024 * 1024

ROW_TILE = 512
SCAN_CHUNK = 1024
ATTN_TQ = 1024
ATTN_TK = 1024
ROUTE_TILE = 256
PEER_TB = 16
SC_TOKENS = 8


def _cparams(sem):
    return pltpu.CompilerParams(dimension_semantics=sem, vmem_limit_bytes=VMEM_LIMIT)


def _norm_mod(x, gain, sc, sh):
    ms = jnp.mean(x * x, axis=-1, keepdims=True)
    return x * lax.rsqrt(ms + EPS) * gain * (1.0 + sc) + sh


def _rms(x, gain, n):
    ms = jnp.sum(x * x, axis=-1, keepdims=True) * (1.0 / n)
    return x * lax.rsqrt(ms + EPS) * gain


def _bdot(a, b):
    return jnp.dot(a.astype(BF16), b, preferred_element_type=F32)


def _mod_kernel(c_ref, w_ref, b_ref, o_ref):
    c = c_ref[...]
    s = c * jax.nn.sigmoid(c)
    o_ref[0] = _bdot(s, w_ref[0].astype(BF16)) + b_ref[0]


def _modulation(c_all, ada_w, ada_b):
    depth, d, n = ada_w.shape
    rows = c_all.shape[0]
    tn = 1536
    return pl.pallas_call(
        _mod_kernel,
        grid=(depth, n // tn),
        in_specs=[
            pl.BlockSpec((rows, d), lambda i, j: (0, 0)),
            pl.BlockSpec((1, d, tn), lambda i, j: (i, 0, j)),
            pl.BlockSpec((1, 1, tn), lambda i, j: (i, 0, j)),
        ],
        out_specs=pl.BlockSpec((1, rows, tn), lambda i, j: (i, 0, j)),
        out_shape=jax.ShapeDtypeStruct((depth, rows, n), F32),
        compiler_params=_cparams(("parallel", "parallel")),
        name="modulation",
    )(c_all, ada_w, ada_b.reshape(depth, 1, n))


def _in_proj_kernel(x_ref, g_ref, sc_ref, sh_ref, w_ref, z_ref):
    h = _norm_mod(x_ref[0], g_ref[...], sc_ref[0], sh_ref[0])
    z_ref[0] = _bdot(h, w_ref[...])


def _in_proj(x, gain, sc, sh, w):
    b, s, d = x.shape
    n = w.shape[1]
    tm = min(ROW_TILE, s)
    return pl.pallas_call(
        _in_proj_kernel,
        grid=(b, s // tm),
        in_specs=[
            pl.BlockSpec((1, tm, d), lambda i, j: (i, j, 0)),
            pl.BlockSpec((1, d), lambda i, j: (0, 0)),
            pl.BlockSpec((1, 1, d), lambda i, j: (i, 0, 0)),
            pl.BlockSpec((1, 1, d), lambda i, j: (i, 0, 0)),
            pl.BlockSpec((d, n), lambda i, j: (0, 0)),
        ],
        out_specs=pl.BlockSpec((1, tm, n), lambda i, j: (i, j, 0)),
        out_shape=jax.ShapeDtypeStruct((b, s, n), F32),
        compiler_params=_cparams(("parallel", "parallel")),
        name="in_proj",
    )(x, gain, sc, sh, w)


def _rglru_kernel(xf_ref, xfp_ref, xfn_ref, xb_ref, xbp_ref, xbn_ref, cw_ref, cb_ref, wg_ref, bg_ref, cl_ref,
                  hf_ref, hb_ref, ext_ref, a_ref, b_ref, carry_ref):
    j = pl.program_id(1)
    nc = pl.num_programs(1)
    tc = xf_ref.shape[1]
    dr = xf_ref.shape[2]
    nt = tc // SUBLANES

    @pl.when(j == 0)
    def _():
        carry_ref[...] = jnp.zeros_like(carry_ref)

    def gates(x_ref, xp_ref, xn_ref, first, last, d):
        ext_ref[0:SUBLANES] = jnp.where(first, 0.0, xp_ref[0])
        ext_ref[SUBLANES:SUBLANES + tc] = x_ref[0]
        ext_ref[SUBLANES + tc:2 * SUBLANES + tc] = jnp.where(last, 0.0, xn_ref[0])
        xc = cb_ref[...]
        for k in range(RG_CONV):
            xc = xc + cw_ref[k:k + 1, :] * ext_ref[SUBLANES - 2 + k:SUBLANES - 2 + k + tc]
        g = _bdot(xc, wg_ref[d]) + bg_ref[d]
        r = jax.nn.sigmoid(g[:, :dr])
        i = jax.nn.sigmoid(g[:, dr:])
        log_a = r * cl_ref[d]
        a = jnp.exp(log_a)
        b = jnp.sqrt(-jnp.tanh(log_a) * (1.0 + a * a)) * (i * xc)
        a_ref[d] = a
        b_ref[d] = b

    rows = lax.broadcasted_iota(jnp.int32, (SUBLANES, dr), 0)

    def scan(d, reverse, out_ref):
        def body(it, carry):
            t = (nt - 1 - it) if reverse else it
            off = pl.multiple_of(t * SUBLANES, SUBLANES)
            a = a_ref[d, pl.ds(off, SUBLANES), :]
            b = b_ref[d, pl.ds(off, SUBLANES), :]
            for s in (1, 2, 4):
                if reverse:
                    a_s = pltpu.roll(a, SUBLANES - s, 0)
                    b_s = pltpu.roll(b, SUBLANES - s, 0)
                    m = rows < SUBLANES - s
                else:
                    a_s = pltpu.roll(a, s, 0)
                    b_s = pltpu.roll(b, s, 0)
                    m = rows >= s
                b = jnp.where(m, a * b_s + b, b)
                a = jnp.where(m, a * a_s, a)
            h = b + a * carry
            out_ref[0, pl.ds(off, SUBLANES), :] = h
            edge = h[0:1] if reverse else h[SUBLANES - 1:SUBLANES]
            return jnp.broadcast_to(edge, (SUBLANES, dr))

        carry_ref[d] = lax.fori_loop(0, nt, body, carry_ref[d])

    gates(xf_ref, xfp_ref, xfn_ref, j == 0, j == nc - 1, 0)
    scan(0, False, hf_ref)
    gates(xb_ref, xbp_ref, xbn_ref, j == nc - 1, j == 0, 1)
    scan(1, True, hb_ref)


def _rglru(z, conv_w, conv_b, wg, bg, cl):
    b, s, _ = z.shape
    dr = conv_w.shape[1]
    tc = min(SCAN_CHUNK, s)
    nc = s // tc
    nb8 = s // SUBLANES
    cb8 = tc // SUBLANES

    def main(rev):
        return pl.BlockSpec((1, tc, dr), (lambda i, j: (i, nc - 1 - j, 0)) if rev else (lambda i, j: (i, j, 0)))

    def prev(rev):
        def f(i, j):
            c = (nc - 1 - j) if rev else j
            return (i, jnp.maximum(c * cb8 - 1, 0), 0)
        return pl.BlockSpec((1, SUBLANES, dr), f)

    def nxt(rev):
        def f(i, j):
            c = (nc - 1 - j) if rev else j
            return (i, jnp.minimum((c + 1) * cb8, nb8 - 1), 0)
        return pl.BlockSpec((1, SUBLANES, dr), f)

    def const(shape):
        return pl.BlockSpec(shape, lambda i, j: (0,) * len(shape))

    out_sd = jax.ShapeDtypeStruct((b, s, dr), F32)
    return pl.pallas_call(
        _rglru_kernel,
        grid=(b, nc),
        in_specs=[main(False), prev(False), nxt(False), main(True), prev(True), nxt(True),
                  const((RG_CONV, dr)), const((1, dr)), const((2, dr, 2 * dr)), const((2, 1, 2 * dr)),
                  const((2, 1, dr))],
        out_specs=[pl.BlockSpec((1, tc, dr), lambda i, j: (i, j, 0)),
                   pl.BlockSpec((1, tc, dr), lambda i, j: (i, nc - 1 - j, 0))],
        out_shape=[out_sd, out_sd],
        scratch_shapes=[pltpu.VMEM((tc + 2 * SUBLANES, dr), F32), pltpu.VMEM((2, tc, dr), F32),
                        pltpu.VMEM((2, tc, dr), F32), pltpu.VMEM((2, SUBLANES, dr), F32)],
        compiler_params=_cparams(("parallel", "arbitrary")),
        name="rglru",
    )(z, z, z, z, z, z, conv_w, conv_b, wg, bg, cl)


def _mla_proj_kernel(z_ref, qn_ref, kvn_ref, wuq_ref, wk_ref, wv_ref, gq_ref, gk_ref, qb_ref, kb_ref,
                     rc_ref, r1_ref, r2_ref, q_ref, k_ref, v_ref):
    zz = z_ref[0]
    q_lora = qn_ref.shape[1]
    kv_lora = kvn_ref.shape[1]
    ql = zz[:, :q_lora]
    kvl = zz[:, q_lora:q_lora + kv_lora]
    kr = zz[:, q_lora + kv_lora:]
    q = _bdot(_rms(ql, qn_ref[...], q_lora), wuq_ref[...])
    kvn = _rms(kvl, kvn_ref[...], kv_lora)
    kk = _bdot(kvn, wk_ref[...])
    vv = _bdot(kvn, wv_ref[...])
    rc, r1, r2 = rc_ref[...], r1_ref[...], r2_ref[...]
    half = QK_ROPE // 2
    scale = QK_DIM ** -0.5 * LOG2_E

    def norm_rope(xh, g):
        xh = _rms(xh, g, QK_DIM)
        return xh * rc + pltpu.roll(xh, HEAD_SLAB - half, 1) * r1 + pltpu.roll(xh, half, 1) * r2

    for h in range(MLA_HEADS):
        sl = slice(h * HEAD_SLAB, (h + 1) * HEAD_SLAB)
        q_ref[0, h] = (norm_rope(q[:, sl], gq_ref[...]) * scale + qb_ref[...]).astype(BF16)
        k_ref[0, h] = (norm_rope(kk[:, sl] + kr, gk_ref[...]) + kb_ref[...]).astype(BF16)
    for p in range(MLA_HEADS // 2):
        v_ref[0, p] = vv[:, p * LANES:(p + 1) * LANES].astype(BF16)


def _mla_proj(z, col_block, qn, kvn, wuq, wk, wv, gq, gk, qb, kb, rc, r1, r2):
    b, s, _ = z.shape
    tm = min(ROW_TILE, s)
    width = qn.shape[1] + kvn.shape[1] + HEAD_SLAB

    def const(a):
        return pl.BlockSpec(a.shape, lambda i, j: (0,) * a.ndim)

    def rope(a):
        return pl.BlockSpec((tm, HEAD_SLAB), lambda i, j: (j, 0))

    hp = MLA_HEADS // 2
    return pl.pallas_call(
        _mla_proj_kernel,
        grid=(b, s // tm),
        in_specs=[pl.BlockSpec((1, tm, width), lambda i, j: (i, j, col_block)),
                  const(qn), const(kvn), const(wuq), const(wk), const(wv), const(gq), const(gk), const(qb), const(kb),
                  rope(rc), rope(r1), rope(r2)],
        out_specs=[pl.BlockSpec((1, MLA_HEADS, tm, HEAD_SLAB), lambda i, j: (i, 0, j, 0)),
                   pl.BlockSpec((1, MLA_HEADS, tm, HEAD_SLAB), lambda i, j: (i, 0, j, 0)),
                   pl.BlockSpec((1, hp, tm, LANES), lambda i, j: (i, 0, j, 0))],
        out_shape=[jax.ShapeDtypeStruct((b, MLA_HEADS, s, HEAD_SLAB), BF16),
                   jax.ShapeDtypeStruct((b, MLA_HEADS, s, HEAD_SLAB), BF16),
                   jax.ShapeDtypeStruct((b, hp, s, LANES), BF16)],
        compiler_params=_cparams(("parallel", "parallel")),
        name="mla_proj",
    )(z, qn, kvn, wuq, wk, wv, gq, gk, qb, kb, rc, r1, r2)


def _flash_kernel(q_ref, k_ref, v_ref, o_ref, m_ref, l_ref, acc_ref):
    ik = pl.program_id(3)

    @pl.when(ik == 0)
    def _():
        m_ref[...] = jnp.full_like(m_ref, -jnp.inf)
        l_ref[...] = jnp.zeros_like(l_ref)
        acc_ref[...] = jnp.zeros_like(acc_ref)

    v = v_ref[0, 0]
    for hh in range(2):
        s = lax.dot_general(q_ref[0, hh], k_ref[0, hh], (((1,), (1,)), ((), ())), preferred_element_type=F32)
        m_prev = m_ref[hh]
        m_new = jnp.maximum(m_prev, jnp.max(s, axis=-1, keepdims=True))
        alpha = jnp.exp2(m_prev - m_new)
        p = jnp.exp2(s - m_new[:, :1])
        l_ref[hh] = alpha * l_ref[hh] + jnp.sum(p, axis=-1, keepdims=True)
        acc_ref[hh] = alpha * acc_ref[hh] + jnp.dot(p.astype(BF16), v, preferred_element_type=F32)
        m_ref[hh] = m_new

    @pl.when(ik == pl.num_programs(3) - 1)
    def _():
        lane = lax.broadcasted_iota(jnp.int32, acc_ref.shape[1:], 1)
        o_ref[0] = jnp.where(lane < V_DIM, acc_ref[0] / l_ref[0], acc_ref[1] / l_ref[1])


def _flash_bounded_kernel(q_ref, k_ref, v_ref, o_ref, l_ref, acc_ref):
    ik = pl.program_id(3)

    @pl.when(ik == 0)
    def _():
        l_ref[...] = jnp.zeros_like(l_ref)
        acc_ref[...] = jnp.zeros_like(acc_ref)

    v = v_ref[0, 0]
    for hh in range(2):
        s = lax.dot_general(q_ref[0, hh], k_ref[0, hh], (((1,), (1,)), ((), ())), preferred_element_type=F32)
        p = jnp.exp2(s)
        l_ref[hh] = l_ref[hh] + jnp.sum(p, axis=-1, keepdims=True)
        acc_ref[hh] = acc_ref[hh] + jnp.dot(p.astype(BF16), v, preferred_element_type=F32)

    @pl.when(ik == pl.num_programs(3) - 1)
    def _():
        lane = lax.broadcasted_iota(jnp.int32, acc_ref.shape[1:], 1)
        o_ref[0] = jnp.where(lane < V_DIM, acc_ref[0] / l_ref[0], acc_ref[1] / l_ref[1])


def _attention(q, k, v, bounded):
    b, h, s, _ = q.shape
    tq = min(ATTN_TQ, s)
    tk = min(ATTN_TK, s)
    hp = h // 2
    return pl.pallas_call(
        _flash_bounded_kernel if bounded else _flash_kernel,
        grid=(b, hp, s // tq, s // tk),
        in_specs=[pl.BlockSpec((1, 2, tq, HEAD_SLAB), lambda i, p, a, c: (i, p, a, 0)),
                  pl.BlockSpec((1, 2, tk, HEAD_SLAB), lambda i, p, a, c: (i, p, c, 0)),
                  pl.BlockSpec((1, 1, tk, LANES), lambda i, p, a, c: (i, p, c, 0))],
        out_specs=pl.BlockSpec((1, tq, LANES), lambda i, p, a, c: (i, a, p)),
        out_shape=jax.ShapeDtypeStruct((b, s, hp * LANES), F32),
        scratch_shapes=[pltpu.VMEM((2, tq, LANES), F32)] * (2 if bounded else 3),
        compiler_params=_cparams(("parallel", "parallel", "parallel", "arbitrary")),
        name="attention_bounded" if bounded else "attention",
    )(q, k, v)


def _ab_out_kernel(x_ref, hf_ref, hb_ref, y_ref, at_ref, w_ref, g_ref, o_ref):
    rg = (hf_ref[0] + hb_ref[0]) * jax.nn.gelu(y_ref[0])
    cat = jnp.concatenate([rg.astype(BF16), at_ref[0].astype(BF16)], axis=-1)
    m = jnp.dot(cat, w_ref[...], preferred_element_type=F32)
    o_ref[0] = x_ref[0] + g_ref[0] * m


def _ab_out(x, hf, hb, z, attn, w, gate):
    b, s, d = x.shape
    dr = hf.shape[2]
    da = attn.shape[2]
    tm = min(ROW_TILE, s)
    return pl.pallas_call(
        _ab_out_kernel,
        grid=(b, s // tm),
        in_specs=[pl.BlockSpec((1, tm, d), lambda i, j: (i, j, 0)),
                  pl.BlockSpec((1, tm, dr), lambda i, j: (i, j, 0)),
                  pl.BlockSpec((1, tm, dr), lambda i, j: (i, j, 0)),
                  pl.BlockSpec((1, tm, dr), lambda i, j: (i, j, 1)),
                  pl.BlockSpec((1, tm, da), lambda i, j: (i, j, 0)),
                  pl.BlockSpec(w.shape, lambda i, j: (0, 0)),
                  pl.BlockSpec((1, 1, d), lambda i, j: (i, 0, 0))],
        out_specs=pl.BlockSpec((1, tm, d), lambda i, j: (i, j, 0)),
        out_shape=jax.ShapeDtypeStruct((b, s, d), F32),
        compiler_params=_cparams(("parallel", "parallel")),
        name="ab_out",
    )(x, hf, hb, z, attn, w, gate)


def _c_in_kernel(x_ref, g_ref, sc_ref, sh_ref, w_ref, bg_ref, cx_ref):
    h = _norm_mod(x_ref[0], g_ref[...], sc_ref[0], sh_ref[0])
    z = _bdot(h, w_ref[...])
    dc = bg_ref.shape[2]
    bg_ref[0] = z[:, :dc]
    cx_ref[0] = z[:, dc:2 * dc] * z[:, 2 * dc:]


def _c_in(x, gain, sc, sh, w):
    b, s, d = x.shape
    dc = w.shape[1] // 3
    tm = min(ROW_TILE, s)
    sd = jax.ShapeDtypeStruct((b, s, dc), F32)
    return pl.pallas_call(
        _c_in_kernel,
        grid=(b, s // tm),
        in_specs=[pl.BlockSpec((1, tm, d), lambda i, j: (i, j, 0)),
                  pl.BlockSpec((1, d), lambda i, j: (0, 0)),
                  pl.BlockSpec((1, 1, d), lambda i, j: (i, 0, 0)),
                  pl.BlockSpec((1, 1, d), lambda i, j: (i, 0, 0)),
                  pl.BlockSpec(w.shape, lambda i, j: (0, 0))],
        out_specs=[pl.BlockSpec((1, tm, dc), lambda i, j: (i, j, 0))] * 2,
        out_shape=[sd, sd],
        compiler_params=_cparams(("parallel", "parallel")),
        name="c_in",
    )(x, gain, sc, sh, w)


def _c_out_kernel(x_ref, bg_ref, cx_ref, cp_ref, cn_ref, cw_ref, w_ref, g_ref, o_ref, ext_ref):
    j = pl.program_id(1)
    tm = cx_ref.shape[1]
    ext_ref[0:SUBLANES] = jnp.where(j == 0, 0.0, cp_ref[0])
    ext_ref[SUBLANES:SUBLANES + tm] = cx_ref[0]
    ext_ref[SUBLANES + tm:2 * SUBLANES + tm] = jnp.where(j == pl.num_programs(1) - 1, 0.0, cn_ref[0])
    conv = cw_ref[0:1, :] * ext_ref[SUBLANES - 1:SUBLANES - 1 + tm]
    conv = conv + cw_ref[1:2, :] * ext_ref[SUBLANES:SUBLANES + tm]
    conv = conv + cw_ref[2:3, :] * ext_ref[SUBLANES + 1:SUBLANES + 1 + tm]
    m = _bdot(bg_ref[0] * conv, w_ref[...])
    o_ref[0] = x_ref[0] + g_ref[0] * m


def _c_out(x, bg, cx, conv_w, w, gate):
    b, s, d = x.shape
    dc = bg.shape[2]
    tm = min(ROW_TILE, s)
    nb8 = s // SUBLANES
    t8 = tm // SUBLANES
    return pl.pallas_call(
        _c_out_kernel,
        grid=(b, s // tm),
        in_specs=[pl.BlockSpec((1, tm, d), lambda i, j: (i, j, 0)),
                  pl.BlockSpec((1, tm, dc), lambda i, j: (i, j, 0)),
                  pl.BlockSpec((1, tm, dc), lambda i, j: (i, j, 0)),
                  pl.BlockSpec((1, SUBLANES, dc), lambda i, j: (i, jnp.maximum(j * t8 - 1, 0), 0)),
                  pl.BlockSpec((1, SUBLANES, dc), lambda i, j: (i, jnp.minimum((j + 1) * t8, nb8 - 1), 0)),
                  pl.BlockSpec(conv_w.shape, lambda i, j: (0, 0)),
                  pl.BlockSpec(w.shape, lambda i, j: (0, 0)),
                  pl.BlockSpec((1, 1, d), lambda i, j: (i, 0, 0))],
        out_specs=pl.BlockSpec((1, tm, d), lambda i, j: (i, j, 0)),
        out_shape=jax.ShapeDtypeStruct((b, s, d), F32),
        scratch_shapes=[pltpu.VMEM((tm + 2 * SUBLANES, dc), F32)],
        compiler_params=_cparams(("parallel", "parallel")),
        name="c_out",
    )(x, bg, cx, cx, cx, conv_w, w, gate)


def _top_rows(s, k, payload=None):
    n = s.shape[0]
    ridx = lax.broadcasted_iota(jnp.int32, s.shape, 0).astype(F32)
    vals, ids = [], []
    for _ in range(k):
        m = jnp.max(s, axis=0, keepdims=True)
        first = jnp.min(jnp.where(s == m, ridx, float(n)), axis=0, keepdims=True)
        sel = ridx == first
        ids.append(first if payload is None else jnp.sum(jnp.where(sel, payload, 0.0), axis=0, keepdims=True))
        vals.append(m)
        s = jnp.where(sel, -jnp.inf, s)
    return vals, ids


_PAIRS = [(a, b) for a in range(PEER_TOPK) for b in range(PEER_TOPK) if (a + 1) * (b + 1) <= PEER_TOPK]


def _route_kernel(x_ref, g_ref, sc_ref, sh_ref, wq_ref, k1_ref, k2_ref, h_ref, idx_ref, gate_ref, q_scr):
    h = _norm_mod(x_ref[0], g_ref[...], sc_ref[0], sh_ref[0])
    h_ref[0] = h
    q_scr[...] = _bdot(h, wq_ref[...])
    n_keys = k1_ref.shape[0]
    half = k1_ref.shape[1]
    tm = x_ref.shape[1]
    nt = (((1,), (1,)), ((), ()))
    pad = -len(_PAIRS) % SUBLANES

    def head(hd, carry):
        off = pl.multiple_of(hd * 2 * half, 2 * half)
        q1 = q_scr[:, pl.ds(off, half)].astype(BF16)
        q2 = q_scr[:, pl.ds(off + half, half)].astype(BF16)
        s1 = lax.dot_general(k1_ref[...], q1, nt, preferred_element_type=F32)
        s2 = lax.dot_general(k2_ref[...], q2, nt, preferred_element_type=F32)
        v1, i1 = _top_rows(s1, PEER_TOPK)
        v2, i2 = _top_rows(s2, PEER_TOPK)
        cv = [v1[a] + v2[b] for a, b in _PAIRS] + [jnp.full((pad, tm), -jnp.inf, F32)]
        ce = [i1[a] * float(n_keys) + i2[b] for a, b in _PAIRS] + [jnp.zeros((pad, tm), F32)]
        vs, es = _top_rows(jnp.concatenate(cv, axis=0), PEER_TOPK, jnp.concatenate(ce, axis=0))
        vs = jnp.concatenate(vs, axis=0)
        e = jnp.exp(vs - vs[0:1])
        row = pl.multiple_of(hd * PEER_TOPK, PEER_TOPK)
        gate_ref[0, pl.ds(row, PEER_TOPK), :] = e / jnp.sum(e, axis=0, keepdims=True)
        idx_ref[0, pl.ds(row, PEER_TOPK), :] = jnp.concatenate(es, axis=0).astype(jnp.int32)
        return carry

    lax.fori_loop(0, PEER_HEADS, head, 0)


def _route(x, gain, sc, sh, wq, k1, k2):
    b, s, d = x.shape
    tm = min(ROUTE_TILE, s)
    nsel = PEER_HEADS * PEER_TOPK
    return pl.pallas_call(
        _route_kernel,
        grid=(b, s // tm),
        in_specs=[pl.BlockSpec((1, tm, d), lambda i, j: (i, j, 0)),
                  pl.BlockSpec((1, d), lambda i, j: (0, 0)),
                  pl.BlockSpec((1, 1, d), lambda i, j: (i, 0, 0)),
                  pl.BlockSpec((1, 1, d), lambda i, j: (i, 0, 0)),
                  pl.BlockSpec(wq.shape, lambda i, j: (0, 0)),
                  pl.BlockSpec(k1.shape, lambda i, j: (0, 0)),
                  pl.BlockSpec(k2.shape, lambda i, j: (0, 0))],
        out_specs=[pl.BlockSpec((1, tm, d), lambda i, j: (i, j, 0)),
                   pl.BlockSpec((1, nsel, tm), lambda i, j: (i, 0, j)),
                   pl.BlockSpec((1, nsel, tm), lambda i, j: (i, 0, j))],
        out_shape=[jax.ShapeDtypeStruct((b, s, d), F32),
                   jax.ShapeDtypeStruct((b, nsel, s), jnp.int32),
                   jax.ShapeDtypeStruct((b, nsel, s), F32)],
        scratch_shapes=[pltpu.VMEM((tm, wq.shape[1]), F32)],
        compiler_params=_cparams(("parallel", "parallel")),
        name="peer_route",
    )(x, gain, sc, sh, wq, k1, k2)


def _pack_table(t):
    n, d = t.shape
    chunks = d // LANES
    bits = lax.bitcast_convert_type(t.astype(BF16), jnp.uint16).astype(jnp.uint32)
    bits = bits.reshape(n * chunks // 2, 2, LANES)
    return lax.bitcast_convert_type(bits[:, 0, :] | (bits[:, 1, :] << 16), jnp.int32)


def _gather_rows(idx_ref, t, tbl_ref, wr):
    rows = [tbl_ref[pl.ds(pl.multiple_of(idx_ref[t, r], wr), wr), :] for r in range(idx_ref.shape[1])]
    return pltpu.bitcast(jnp.concatenate(rows, axis=0), BF16)


def _idx_copy(idx_hbm, half_block, smem, sem):
    sub = smem.shape[0]
    return pltpu.make_async_copy(idx_hbm.at[pl.ds(half_block * sub, sub)], smem, sem)


def _staged_halves(idx_hbm, smem_a, smem_b, sems, process):
    i = pl.program_id(0)
    sub = smem_a.shape[0]

    @pl.when(i == 0)
    def _():
        _idx_copy(idx_hbm, 0, smem_a, sems.at[0]).start()

    _idx_copy(idx_hbm, 2 * i, smem_a, sems.at[0]).wait()
    _idx_copy(idx_hbm, 2 * i + 1, smem_b, sems.at[1]).start()
    process(smem_a, 0)
    _idx_copy(idx_hbm, 2 * i + 1, smem_b, sems.at[1]).wait()

    @pl.when(i + 1 < pl.num_programs(0))
    def _():
        _idx_copy(idx_hbm, 2 * i + 2, smem_a, sems.at[0]).start()

    process(smem_b, sub)


def _split_bf16(a):
    hi = a.astype(BF16).astype(F32)
    return jnp.concatenate([hi, a - hi], axis=0).astype(BF16)


def _chunk_mask(chunks, width):
    lane = lax.broadcasted_iota(jnp.int32, (chunks, width), 1)
    return lane % chunks == lax.broadcasted_iota(jnp.int32, (chunks, width), 0)


def _peer_blocks(t):
    sub = min(PEER_TB, t // 2)
    assert t % (2 * sub) == 0
    return sub, 2 * sub


def _idx_scratch(sub, nsel):
    return [pltpu.SMEM((sub, nsel), jnp.int32), pltpu.SMEM((sub, nsel), jnp.int32), pltpu.SemaphoreType.DMA((2,))]


def _pack_pairs(t):
    d = t.shape[1]
    bits = lax.bitcast_convert_type(t.astype(BF16), jnp.uint16).astype(jnp.uint32)
    return lax.bitcast_convert_type(bits[:, :d // 2] | (bits[:, d // 2:] << 16), jnp.int32)


def _sc_scores(tbl, idx, x):
    t, nsel = idx.shape
    w = tbl.shape[1]
    info = plsc.get_sparse_core_info()
    lanes = info.num_lanes
    workers = info.num_cores * info.num_subcores
    per = t // workers
    half = nsel // 2
    assert t % (workers * SC_TOKENS) == 0 and half % lanes == 0 and w % lanes == 0
    mesh = plsc.VectorSubcoreMesh(core_axis_name="c", subcore_axis_name="s")

    @functools.partial(
        pl.kernel, mesh=mesh, out_type=jax.ShapeDtypeStruct((t, nsel), F32),
        scratch_types=[pltpu.VMEM((SC_TOKENS, nsel), jnp.int32), pltpu.VMEM((SC_TOKENS, 2 * w), F32),
                       pltpu.VMEM((2, half, w), jnp.int32), pltpu.VMEM((SC_TOKENS, nsel), F32),
                       pltpu.SemaphoreType.DMA((2,))],
        compiler_params=pltpu.CompilerParams(needs_layout_passes=False),
        name="peer_scores")
    def scores(tbl_hbm, idx_hbm, x_hbm, out_hbm, idx_v, x_v, rows_v, s_v, sems):
        wid = lax.axis_index("s") * info.num_cores + lax.axis_index("c")
        lane = lax.iota(jnp.int32, lanes)

        def gather(tt, hb):
            return pltpu.make_async_copy(tbl_hbm.at[idx_v.at[tt, pl.ds(hb * half, half)]], rows_v.at[hb], sems.at[hb])

        def compute(tt, hb):
            @pl.loop(0, half // lanes)
            def _(rg):
                def chunk(c, accs):
                    xl = x_v[tt, pl.ds(c * lanes, lanes)]
                    xh = x_v[tt, pl.ds(w + c * lanes, lanes)]
                    out = []
                    for r in range(lanes):
                        wv = rows_v[hb, rg * lanes + r, pl.ds(c * lanes, lanes)]
                        lo = lax.bitcast_convert_type(wv << 16, F32)
                        hi = lax.bitcast_convert_type(wv & jnp.int32(-65536), F32)
                        out.append(accs[r] + lo * xl + hi * xh)
                    return tuple(out)

                accs = lax.fori_loop(0, w // lanes, chunk, tuple(jnp.zeros((lanes,), F32) for _ in range(lanes)))
                res = jnp.zeros((lanes,), F32)
                for r in range(lanes):
                    res = jnp.where(lane == r, jnp.sum(accs[r]), res)
                s_v[tt, pl.ds(hb * half + rg * lanes, lanes)] = res

        @pl.loop(0, per // SC_TOKENS)
        def _(bi):
            t0 = wid * per + bi * SC_TOKENS
            pltpu.sync_copy(idx_hbm.at[pl.ds(t0, SC_TOKENS)], idx_v)
            pltpu.sync_copy(x_hbm.at[pl.ds(t0, SC_TOKENS)], x_v)
            gather(0, 0).start()

            @pl.loop(0, SC_TOKENS)
            def _(tt):
                gather(tt, 1).start()
                gather(tt, 0).wait()
                compute(tt, 0)

                @pl.when(tt + 1 < SC_TOKENS)
                def _():
                    gather(tt + 1, 0).start()

                gather(tt, 1).wait()
                compute(tt, 1)

            pltpu.sync_copy(s_v, out_hbm.at[pl.ds(t0, SC_TOKENS)])

    return scores(tbl, idx, x)


def _peer_v_kernel(idx_hbm, s_ref, gts_ref, spread_ref, tbl_ref, x_ref, g_ref, o_ref, smem_a, smem_b, sems, o_scr,
                   act_ref):
    tb = x_ref.shape[0]
    chunks = x_ref.shape[1] // LANES
    width = act_ref.shape[1]
    mask = _chunk_mask(chunks, width)
    a = _split_bf16(jax.nn.gelu(s_ref[...]) * gts_ref[...])
    a = jnp.dot(a, spread_ref[...], preferred_element_type=F32)
    act_ref[...] = a[:tb] + a[tb:]

    def process(idx_ref, first):
        for t in range(idx_ref.shape[0]):
            gb = _gather_rows(idx_ref, t, tbl_ref, chunks // 2)
            a = jnp.where(mask, jnp.broadcast_to(act_ref[first + t:first + t + 1, :], (chunks, width)), 0.0)
            o = jnp.dot(_split_bf16(a), gb, preferred_element_type=F32)
            o_scr[(first + t) * chunks:(first + t + 1) * chunks, :] = o[:chunks] + o[chunks:]

    _staged_halves(idx_hbm, smem_a, smem_b, sems, process)
    for c in range(chunks):
        sl = slice(c * LANES, (c + 1) * LANES)
        o_ref[:, sl] = x_ref[:, sl] + g_ref[0][:, sl] * o_scr[pl.ds(c, tb, stride=chunks), :]


def _peer_v(idx, scores, gts, table, x, gate, tokens_per_batch):
    t, nsel = idx.shape
    d = x.shape[1]
    chunks = d // LANES
    width = nsel * chunks
    sub, tb = _peer_blocks(t)
    assert tokens_per_batch % tb == 0
    spread = jnp.kron(jnp.eye(nsel, dtype=F32), jnp.ones((1, chunks), F32)).astype(BF16)
    return pl.pallas_call(
        _peer_v_kernel,
        grid=(t // tb,),
        in_specs=[pl.BlockSpec(memory_space=pl.ANY),
                  pl.BlockSpec((tb, nsel), lambda i: (i, 0)),
                  pl.BlockSpec((tb, nsel), lambda i: (i, 0)),
                  pl.BlockSpec(spread.shape, lambda i: (0, 0), pipeline_mode=pl.Buffered(1)),
                  pl.BlockSpec(table.shape, lambda i: (0, 0), pipeline_mode=pl.Buffered(1)),
                  pl.BlockSpec((tb, d), lambda i: (i, 0)),
                  pl.BlockSpec((1, 1, d), lambda i: (i * tb // tokens_per_batch, 0, 0))],
        out_specs=pl.BlockSpec((tb, d), lambda i: (i, 0)),
        out_shape=jax.ShapeDtypeStruct(x.shape, F32),
        scratch_shapes=_idx_scratch(sub, nsel) + [pltpu.VMEM((tb * chunks, LANES), F32), pltpu.VMEM((tb, width), F32)],
        compiler_params=_cparams(("arbitrary",)),
        name="peer_v",
    )(idx, scores, gts, spread, table, x, gate)


def _peer(x, gain, sc, sh, gate, wq, k1, k2, u_tbl, v_tbl):
    b, s, d = x.shape
    chunks = d // LANES
    h, idx, gts = _route(x, gain, sc, sh, wq, k1, k2)
    nsel = idx.shape[1]
    idx = jnp.transpose(idx, (0, 2, 1)).reshape(b * s, nsel)
    gts = jnp.transpose(gts, (0, 2, 1)).reshape(b * s, nsel)
    scores = _sc_scores(u_tbl, idx, h.reshape(b * s, d))
    out = _peer_v(idx * (chunks // 2), scores, gts, v_tbl, x.reshape(b * s, d), gate, s)
    return out.reshape(b, s, d)


def _block_diag(w):
    h, i, j = w.shape
    return jnp.einsum('hij,hg->higj', w, jnp.eye(h, dtype=w.dtype)).reshape(h * i, h * j)


def _rope_tables(s):
    half = QK_ROPE // 2
    inv = 1.0 / (ROPE_THETA ** (jnp.arange(0, QK_ROPE, 2, dtype=F32) / QK_ROPE))
    ang = jnp.arange(s, dtype=F32)[:, None] * inv[None, :]
    cos, sin = jnp.cos(ang), jnp.sin(ang)
    z = jnp.zeros((s, QK_NOPE), F32)
    tail = jnp.zeros((s, HEAD_SLAB - QK_DIM), F32)
    zh = jnp.zeros((s, half), F32)
    rc = jnp.concatenate([z + 1.0, cos, cos, tail + 1.0], axis=1)
    r1 = jnp.concatenate([z, -sin, zh, tail], axis=1)
    r2 = jnp.concatenate([z, zh, sin, tail], axis=1)
    return rc, r1, r2


def _pad_last(a, n):
    return jnp.pad(a, [(0, 0)] * (a.ndim - 1) + [(0, n - a.shape[-1])])


def _trunk(x, mod, p):
    b, s, d = x.shape
    depth = p['ada_w'].shape[0]
    d_rnn = p['rg_conv_w'].shape[2]
    q_lora = p['mla_q_norm'].shape[1]
    kv_lora = p['mla_kv_norm'].shape[1]
    for i in range(depth):
        sh1, sc1, g1, sh2, sc2, g2 = [m[:, None, :] for m in jnp.split(mod[i], 6, axis=-1)]
        j = i // 2
        n1 = p['norm1_g'][i][None, :]
        if i % 2 == 0:
            w_in = p['ab_w_in'][j]
            lat = 2 * d_rnn + q_lora + kv_lora
            w0 = jnp.concatenate([w_in[:, :lat], jnp.zeros((d, QK_NOPE), F32), w_in[:, lat:],
                                  jnp.zeros((d, HEAD_SLAB - QK_DIM), F32)], axis=1).astype(BF16)
            z = _in_proj(x, n1, sc1, sh1, w0)
            wg = jnp.stack([jnp.concatenate([_block_diag(p['rg_wa'][j][k]), _block_diag(p['rg_wx'][j][k])], axis=1)
                            for k in range(2)]).astype(BF16)
            bg = jnp.concatenate([p['rg_ba'][j], p['rg_bx'][j]], axis=1)[:, None, :]
            cl = (-RG_C * jax.nn.softplus(-p['rg_lambda'][j]))[:, None, :]
            hf, hb = _rglru(z, p['rg_conv_w'][j], p['rg_conv_b'][j][None, :], wg, bg, cl)
            wuq = _pad_last(p['mla_w_uq'][j].reshape(q_lora, MLA_HEADS, QK_DIM), HEAD_SLAB)
            wuq = wuq.reshape(q_lora, MLA_HEADS * HEAD_SLAB).astype(BF16)
            wkv = p['mla_w_ukv'][j].reshape(kv_lora, MLA_HEADS, QK_NOPE + V_DIM)
            wk = _pad_last(wkv[:, :, :QK_NOPE], HEAD_SLAB).reshape(kv_lora, MLA_HEADS * HEAD_SLAB).astype(BF16)
            wv = wkv[:, :, QK_NOPE:].reshape(kv_lora, MLA_HEADS * V_DIM).astype(BF16)
            gq = _pad_last(p['mla_qn_q'][j][None, :], HEAD_SLAB)
            gk = _pad_last(p['mla_qn_k'][j][None, :], HEAD_SLAB)
            rc, r1, r2 = _rope_tables(s)
            width = q_lora + kv_lora + HEAD_SLAB
            assert (2 * d_rnn) % width == 0
            bound = 1.02 * QK_DIM ** 0.5 * jnp.max(jnp.abs(gq)) * jnp.max(jnp.abs(gk))
            spare = jnp.arange(HEAD_SLAB)[None, :] == HEAD_SLAB - 1
            qb = jnp.where(spare, 1.0, 0.0).astype(F32)
            kb = jnp.where(spare, -bound * LOG2_E, 0.0).astype(F32)
            q, k, v = _mla_proj(z, 2 * d_rnn // width, p['mla_q_norm'][j][None, :], p['mla_kv_norm'][j][None, :],
                                wuq, wk, wv, gq, gk, qb, kb, rc, r1, r2)
            attn = lax.cond(bound < MAX_SOFTMAX_BOUND, functools.partial(_attention, bounded=True),
                            functools.partial(_attention, bounded=False), q, k, v)
            x = _ab_out(x, hf, hb, z, attn, p['ab_w_out'][j].astype(BF16), g1)
        else:
            bgate, cx = _c_in(x, n1, sc1, sh1, p['c_w_in'][j].astype(BF16))
            x = _c_out(x, bgate, cx, p['c_conv_w'][j], p['c_w_out'][j].astype(BF16), g1)
        x = _peer(x, p['norm2_g'][i][None, :], sc2, sh2, g2, p['peer_wq'][i].astype(BF16),
                  p['peer_k1'][i].astype(BF16), p['peer_k2'][i].astype(BF16),
                  _pack_pairs(p['peer_u'][i]), _pack_table(p['peer_v'][i]))
    return x


def kernel(x_prompt, x_sample, c_prompt, c_sample, ada_w, ada_b, norm1_g, norm2_g, ab_w_in, rg_conv_w, rg_conv_b, rg_wa, rg_ba, rg_wx, rg_bx, rg_lambda, mla_q_norm, mla_w_uq, mla_kv_norm, mla_w_ukv, mla_qn_q, mla_qn_k, ab_w_out, c_w_in, c_conv_w, c_w_out, peer_wq, peer_k1, peer_k2, peer_u, peer_v):
    p = dict(ada_w=ada_w, norm1_g=norm1_g, norm2_g=norm2_g, ab_w_in=ab_w_in, rg_conv_w=rg_conv_w,
             rg_conv_b=rg_conv_b, rg_wa=rg_wa, rg_ba=rg_ba, rg_wx=rg_wx, rg_bx=rg_bx, rg_lambda=rg_lambda,
             mla_q_norm=mla_q_norm, mla_w_uq=mla_w_uq, mla_kv_norm=mla_kv_norm, mla_w_ukv=mla_w_ukv,
             mla_qn_q=mla_qn_q, mla_qn_k=mla_qn_k, ab_w_out=ab_w_out, c_w_in=c_w_in, c_conv_w=c_conv_w,
             c_w_out=c_w_out, peer_wq=peer_wq, peer_k1=peer_k1, peer_k2=peer_k2, peer_u=peer_u, peer_v=peer_v)
    bp, bs = c_prompt.shape[0], c_sample.shape[0]
    rows = -(-(bp + bs) // SUBLANES) * SUBLANES
    c_all = jnp.pad(jnp.concatenate([c_prompt, c_sample], axis=0), ((0, rows - bp - bs), (0, 0)))
    mod = _modulation(c_all, ada_w, ada_b)
    y_prompt = _trunk(x_prompt, mod[:, :bp], p)
    y_sample = _trunk(x_sample, mod[:, bp:bp + bs], p)
    return (y_prompt, y_sample)
```

```python
import functools

import jax
import jax.numpy as jnp
from jax import lax
from jax.experimental import pallas as pl
from jax.experimental.pallas import tpu as pltpu
from jax.experimental.pallas import tpu_sc as plsc

F32 = jnp.float32
BF16 = jnp.bfloat16
EPS = 1e-6
LOG2_E = 1.4426950408889634

RG_HEADS = 8
RG_CONV = 4
RG_C = 8.0
MLA_HEADS = 8
QK_NOPE = 64
QK_ROPE = 32
V_DIM = 64
QK_DIM = QK_NOPE + QK_ROPE
ROPE_THETA = 10000.0
PEER_HEADS = 8
PEER_TOPK = 16
HEAD_SLAB = 128
MAX_SOFTMAX_BOUND = 40.0

LANES = 128
SUBLANES = 8
VMEM_LIMIT = 56 * 1024 * 1024

ROW_TILE = 512
SCAN_CHUNK = 1024
ATTN_TQ = 1024
ATTN_TK = 1024
ROUTE_TILE = 256
PEER_TB = 16
SC_TOKENS = 8


def _cparams(sem):
    return pltpu.CompilerParams(dimension_semantics=sem, vmem_limit_bytes=VMEM_LIMIT)


def _norm_mod(x, gain, sc, sh):
    ms = jnp.mean(x * x, axis=-1, keepdims=True)
    return x * lax.rsqrt(ms + EPS) * gain * (1.0 + sc) + sh


def _rms(x, gain, n):
    ms = jnp.sum(x * x, axis=-1, keepdims=True) * (1.0 / n)
    return x * lax.rsqrt(ms + EPS) * gain


def _bdot(a, b):
    return jnp.dot(a.astype(BF16), b, preferred_element_type=F32)


def _mod_kernel(c_ref, w_ref, b_ref, o_ref):
    c = c_ref[...]
    s = c * jax.nn.sigmoid(c)
    o_ref[0] = _bdot(s, w_ref[0].astype(BF16)) + b_ref[0]


def _modulation(c_all, ada_w, ada_b):
    depth, d, n = ada_w.shape
    rows = c_all.shape[0]
    tn = 1536
    return pl.pallas_call(
        _mod_kernel,
        grid=(depth, n // tn),
        in_specs=[
            pl.BlockSpec((rows, d), lambda i, j: (0, 0)),
            pl.BlockSpec((1, d, tn), lambda i, j: (i, 0, j)),
            pl.BlockSpec((1, 1, tn), lambda i, j: (i, 0, j)),
        ],
        out_specs=pl.BlockSpec((1, rows, tn), lambda i, j: (i, 0, j)),
        out_shape=jax.ShapeDtypeStruct((depth, rows, n), F32),
        compiler_params=_cparams(("parallel", "parallel")),
        name="modulation",
    )(c_all, ada_w, ada_b.reshape(depth, 1, n))


def _in_proj_kernel(x_ref, g_ref, sc_ref, sh_ref, w_ref, z_ref):
    h = _norm_mod(x_ref[0], g_ref[...], sc_ref[0], sh_ref[0])
    z_ref[0] = _bdot(h, w_ref[...])


def _in_proj(x, gain, sc, sh, w):
    b, s, d = x.shape
    n = w.shape[1]
    tm = min(ROW_TILE, s)
    return pl.pallas_call(
        _in_proj_kernel,
        grid=(b, s // tm),
        in_specs=[
            pl.BlockSpec((1, tm, d), lambda i, j: (i, j, 0)),
            pl.BlockSpec((1, d), lambda i, j: (0, 0)),
            pl.BlockSpec((1, 1, d), lambda i, j: (i, 0, 0)),
            pl.BlockSpec((1, 1, d), lambda i, j: (i, 0, 0)),
            pl.BlockSpec((d, n), lambda i, j: (0, 0)),
        ],
        out_specs=pl.BlockSpec((1, tm, n), lambda i, j: (i, j, 0)),
        out_shape=jax.ShapeDtypeStruct((b, s, n), F32),
        compiler_params=_cparams(("parallel", "parallel")),
        name="in_proj",
    )(x, gain, sc, sh, w)


def _rglru_kernel(xf_ref, xfp_ref, xfn_ref, xb_ref, xbp_ref, xbn_ref, cw_ref, cb_ref, wg_ref, bg_ref, cl_ref,
                  hf_ref, hb_ref, ext_ref, a_ref, b_ref, carry_ref):
    j = pl.program_id(1)
    nc = pl.num_programs(1)
    tc = xf_ref.shape[1]
    dr = xf_ref.shape[2]
    nt = tc // SUBLANES

    @pl.when(j == 0)
    def _():
        carry_ref[...] = jnp.zeros_like(carry_ref)

    def gates(x_ref, xp_ref, xn_ref, first, last, d):
        ext_ref[0:SUBLANES] = jnp.where(first, 0.0, xp_ref[0])
        ext_ref[SUBLANES:SUBLANES + tc] = x_ref[0]
        ext_ref[SUBLANES + tc:2 * SUBLANES + tc] = jnp.where(last, 0.0, xn_ref[0])
        xc = cb_ref[...]
        for k in range(RG_CONV):
            xc = xc + cw_ref[k:k + 1, :] * ext_ref[SUBLANES - 2 + k:SUBLANES - 2 + k + tc]
        g = _bdot(xc, wg_ref[d]) + bg_ref[d]
        r = jax.nn.sigmoid(g[:, :dr])
        i = jax.nn.sigmoid(g[:, dr:])
        log_a = r * cl_ref[d]
        a = jnp.exp(log_a)
        b = jnp.sqrt(-jnp.tanh(log_a) * (1.0 + a * a)) * (i * xc)
        a_ref[d] = a
        b_ref[d] = b

    rows = lax.broadcasted_iota(jnp.int32, (SUBLANES, dr), 0)

    def scan(d, reverse, out_ref):
        def body(it, carry):
            t = (nt - 1 - it) if reverse else it
            off = pl.multiple_of(t * SUBLANES, SUBLANES)
            a = a_ref[d, pl.ds(off, SUBLANES), :]
            b = b_ref[d, pl.ds(off, SUBLANES), :]
            for s in (1, 2, 4):
                if reverse:
                    a_s = pltpu.roll(a, SUBLANES - s, 0)
                    b_s = pltpu.roll(b, SUBLANES - s, 0)
                    m = rows < SUBLANES - s
                else:
                    a_s = pltpu.roll(a, s, 0)
                    b_s = pltpu.roll(b, s, 0)
                    m = rows >= s
                b = jnp.where(m, a * b_s + b, b)
                a = jnp.where(m, a * a_s, a)
            h = b + a * carry
            out_ref[0, pl.ds(off, SUBLANES), :] = h
            edge = h[0:1] if reverse else h[SUBLANES - 1:SUBLANES]
            return jnp.broadcast_to(edge, (SUBLANES, dr))

        carry_ref[d] = lax.fori_loop(0, nt, body, carry_ref[d])

    gates(xf_ref, xfp_ref, xfn_ref, j == 0, j == nc - 1, 0)
    scan(0, False, hf_ref)
    gates(xb_ref, xbp_ref, xbn_ref, j == nc - 1, j == 0, 1)
    scan(1, True, hb_ref)


def _rglru(z, conv_w, conv_b, wg, bg, cl):
    b, s, _ = z.shape
    dr = conv_w.shape[1]
    tc = min(SCAN_CHUNK, s)
    nc = s // tc
    nb8 = s // SUBLANES
    cb8 = tc // SUBLANES

    def main(rev):
        return pl.BlockSpec((1, tc, dr), (lambda i, j: (i, nc - 1 - j, 0)) if rev else (lambda i, j: (i, j, 0)))

    def prev(rev):
        def f(i, j):
            c = (nc - 1 - j) if rev else j
            return (i, jnp.maximum(c * cb8 - 1, 0), 0)
        return pl.BlockSpec((1, SUBLANES, dr), f)

    def nxt(rev):
        def f(i, j):
            c = (nc - 1 - j) if rev else j
            return (i, jnp.minimum((c + 1) * cb8, nb8 - 1), 0)
        return pl.BlockSpec((1, SUBLANES, dr), f)

    def const(shape):
        return pl.BlockSpec(shape, lambda i, j: (0,) * len(shape))

    out_sd = jax.ShapeDtypeStruct((b, s, dr), F32)
    return pl.pallas_call(
        _rglru_kernel,
        grid=(b, nc),
        in_specs=[main(False), prev(False), nxt(False), main(True), prev(True), nxt(True),
                  const((RG_CONV, dr)), const((1, dr)), const((2, dr, 2 * dr)), const((2, 1, 2 * dr)),
                  const((2, 1, dr))],
        out_specs=[pl.BlockSpec((1, tc, dr), lambda i, j: (i, j, 0)),
                   pl.BlockSpec((1, tc, dr), lambda i, j: (i, nc - 1 - j, 0))],
        out_shape=[out_sd, out_sd],
        scratch_shapes=[pltpu.VMEM((tc + 2 * SUBLANES, dr), F32), pltpu.VMEM((2, tc, dr), F32),
                        pltpu.VMEM((2, tc, dr), F32), pltpu.VMEM((2, SUBLANES, dr), F32)],
        compiler_params=_cparams(("parallel", "arbitrary")),
        name="rglru",
    )(z, z, z, z, z, z, conv_w, conv_b, wg, bg, cl)


def _mla_proj_kernel(z_ref, qn_ref, kvn_ref, wuq_ref, wk_ref, wv_ref, gq_ref, gk_ref, qb_ref, kb_ref,
                     rc_ref, r1_ref, r2_ref, q_ref, k_ref, v_ref):
    zz = z_ref[0]
    q_lora = qn_ref.shape[1]
    kv_lora = kvn_ref.shape[1]
    ql = zz[:, :q_lora]
    kvl = zz[:, q_lora:q_lora + kv_lora]
    kr = zz[:, q_lora + kv_lora:]
    q = _bdot(_rms(ql, qn_ref[...], q_lora), wuq_ref[...])
    kvn = _rms(kvl, kvn_ref[...], kv_lora)
    kk = _bdot(kvn, wk_ref[...])
    vv = _bdot(kvn, wv_ref[...])
    rc, r1, r2 = rc_ref[...], r1_ref[...], r2_ref[...]
    half = QK_ROPE // 2
    scale = QK_DIM ** -0.5 * LOG2_E

    def norm_rope(xh, g):
        xh = _rms(xh, g, QK_DIM)
        return xh * rc + pltpu.roll(xh, HEAD_SLAB - half, 1) * r1 + pltpu.roll(xh, half, 1) * r2

    for h in range(MLA_HEADS):
        sl = slice(h * HEAD_SLAB, (h + 1) * HEAD_SLAB)
        q_ref[0, h] = (norm_rope(q[:, sl], gq_ref[...]) * scale + qb_ref[...]).astype(BF16)
        k_ref[0, h] = (norm_rope(kk[:, sl] + kr, gk_ref[...]) + kb_ref[...]).astype(BF16)
    for p in range(MLA_HEADS // 2):
        v_ref[0, p] = vv[:, p * LANES:(p + 1) * LANES].astype(BF16)


def _mla_proj(z, col_block, qn, kvn, wuq, wk, wv, gq, gk, qb, kb, rc, r1, r2):
    b, s, _ = z.shape
    tm = min(ROW_TILE, s)
    width = qn.shape[1] + kvn.shape[1] + HEAD_SLAB

    def const(a):
        return pl.BlockSpec(a.shape, lambda i, j: (0,) * a.ndim)

    def rope(a):
        return pl.BlockSpec((tm, HEAD_SLAB), lambda i, j: (j, 0))

    hp = MLA_HEADS // 2
    return pl.pallas_call(
        _mla_proj_kernel,
        grid=(b, s // tm),
        in_specs=[pl.BlockSpec((1, tm, width), lambda i, j: (i, j, col_block)),
                  const(qn), const(kvn), const(wuq), const(wk), const(wv), const(gq), const(gk), const(qb), const(kb),
                  rope(rc), rope(r1), rope(r2)],
        out_specs=[pl.BlockSpec((1, MLA_HEADS, tm, HEAD_SLAB), lambda i, j: (i, 0, j, 0)),
                   pl.BlockSpec((1, MLA_HEADS, tm, HEAD_SLAB), lambda i, j: (i, 0, j, 0)),
                   pl.BlockSpec((1, hp, tm, LANES), lambda i, j: (i, 0, j, 0))],
        out_shape=[jax.ShapeDtypeStruct((b, MLA_HEADS, s, HEAD_SLAB), BF16),
                   jax.ShapeDtypeStruct((b, MLA_HEADS, s, HEAD_SLAB), BF16),
                   jax.ShapeDtypeStruct((b, hp, s, LANES), BF16)],
        compiler_params=_cparams(("parallel", "parallel")),
        name="mla_proj",
    )(z, qn, kvn, wuq, wk, wv, gq, gk, qb, kb, rc, r1, r2)


def _flash_kernel(q_ref, k_ref, v_ref, o_ref, m_ref, l_ref, acc_ref):
    ik = pl.program_id(3)

    @pl.when(ik == 0)
    def _():
        m_ref[...] = jnp.full_like(m_ref, -jnp.inf)
        l_ref[...] = jnp.zeros_like(l_ref)
        acc_ref[...] = jnp.zeros_like(acc_ref)

    v = v_ref[0, 0]
    for hh in range(2):
        s = lax.dot_general(q_ref[0, hh], k_ref[0, hh], (((1,), (1,)), ((), ())), preferred_element_type=F32)
        m_prev = m_ref[hh]
        m_new = jnp.maximum(m_prev, jnp.max(s, axis=-1, keepdims=True))
        alpha = jnp.exp2(m_prev - m_new)
        p = jnp.exp2(s - m_new[:, :1])
        l_ref[hh] = alpha * l_ref[hh] + jnp.sum(p, axis=-1, keepdims=True)
        acc_ref[hh] = alpha * acc_ref[hh] + jnp.dot(p.astype(BF16), v, preferred_element_type=F32)
        m_ref[hh] = m_new

    @pl.when(ik == pl.num_programs(3) - 1)
    def _():
        lane = lax.broadcasted_iota(jnp.int32, acc_ref.shape[1:], 1)
        o_ref[0] = jnp.where(lane < V_DIM, acc_ref[0] / l_ref[0], acc_ref[1] / l_ref[1])


def _flash_bounded_kernel(q_ref, k_ref, v_ref, o_ref, l_ref, acc_ref):
    ik = pl.program_id(3)

    @pl.when(ik == 0)
    def _():
        l_ref[...] = jnp.zeros_like(l_ref)
        acc_ref[...] = jnp.zeros_like(acc_ref)

    v = v_ref[0, 0]
    for hh in range(2):
        s = lax.dot_general(q_ref[0, hh], k_ref[0, hh], (((1,), (1,)), ((), ())), preferred_element_type=F32)
        p = jnp.exp2(s)
        l_ref[hh] = l_ref[hh] + jnp.sum(p, axis=-1, keepdims=True)
        acc_ref[hh] = acc_ref[hh] + jnp.dot(p.astype(BF16), v, preferred_element_type=F32)

    @pl.when(ik == pl.num_programs(3) - 1)
    def _():
        lane = lax.broadcasted_iota(jnp.int32, acc_ref.shape[1:], 1)
        o_ref[0] = jnp.where(lane < V_DIM, acc_ref[0] / l_ref[0], acc_ref[1] / l_ref[1])


def _attention(q, k, v, bounded):
    b, h, s, _ = q.shape
    tq = min(ATTN_TQ, s)
    tk = min(ATTN_TK, s)
    hp = h // 2
    return pl.pallas_call(
        _flash_bounded_kernel if bounded else _flash_kernel,
        grid=(b, hp, s // tq, s // tk),
        in_specs=[pl.BlockSpec((1, 2, tq, HEAD_SLAB), lambda i, p, a, c: (i, p, a, 0)),
                  pl.BlockSpec((1, 2, tk, HEAD_SLAB), lambda i, p, a, c: (i, p, c, 0)),
                  pl.BlockSpec((1, 1, tk, LANES), lambda i, p, a, c: (i, p, c, 0))],
        out_specs=pl.BlockSpec((1, tq, LANES), lambda i, p, a, c: (i, a, p)),
        out_shape=jax.ShapeDtypeStruct((b, s, hp * LANES), F32),
        scratch_shapes=[pltpu.VMEM((2, tq, LANES), F32)] * (2 if bounded else 3),
        compiler_params=_cparams(("parallel", "parallel", "parallel", "arbitrary")),
        name="attention_bounded" if bounded else "attention",
    )(q, k, v)


def _ab_out_kernel(x_ref, hf_ref, hb_ref, y_ref, at_ref, w_ref, g_ref, o_ref):
    rg = (hf_ref[0] + hb_ref[0]) * jax.nn.gelu(y_ref[0])
    cat = jnp.concatenate([rg.astype(BF16), at_ref[0].astype(BF16)], axis=-1)
    m = jnp.dot(cat, w_ref[...], preferred_element_type=F32)
    o_ref[0] = x_ref[0] + g_ref[0] * m


def _ab_out(x, hf, hb, z, attn, w, gate):
    b, s, d = x.shape
    dr = hf.shape[2]
    da = attn.shape[2]
    tm = min(ROW_TILE, s)
    return pl.pallas_call(
        _ab_out_kernel,
        grid=(b, s // tm),
        in_specs=[pl.BlockSpec((1, tm, d), lambda i, j: (i, j, 0)),
                  pl.BlockSpec((1, tm, dr), lambda i, j: (i, j, 0)),
                  pl.BlockSpec((1, tm, dr), lambda i, j: (i, j, 0)),
                  pl.BlockSpec((1, tm, dr), lambda i, j: (i, j, 1)),
                  pl.BlockSpec((1, tm, da), lambda i, j: (i, j, 0)),
                  pl.BlockSpec(w.shape, lambda i, j: (0, 0)),
                  pl.BlockSpec((1, 1, d), lambda i, j: (i, 0, 0))],
        out_specs=pl.BlockSpec((1, tm, d), lambda i, j: (i, j, 0)),
        out_shape=jax.ShapeDtypeStruct((b, s, d), F32),
        compiler_params=_cparams(("parallel", "parallel")),
        name="ab_out",
    )(x, hf, hb, z, attn, w, gate)


def _c_in_kernel(x_ref, g_ref, sc_ref, sh_ref, w_ref, bg_ref, cx_ref):
    h = _norm_mod(x_ref[0], g_ref[...], sc_ref[0], sh_ref[0])
    z = _bdot(h, w_ref[...])
    dc = bg_ref.shape[2]
    bg_ref[0] = z[:, :dc]
    cx_ref[0] = z[:, dc:2 * dc] * z[:, 2 * dc:]


def _c_in(x, gain, sc, sh, w):
    b, s, d = x.shape
    dc = w.shape[1] // 3
    tm = min(ROW_TILE, s)
    sd = jax.ShapeDtypeStruct((b, s, dc), F32)
    return pl.pallas_call(
        _c_in_kernel,
        grid=(b, s // tm),
        in_specs=[pl.BlockSpec((1, tm, d), lambda i, j: (i, j, 0)),
                  pl.BlockSpec((1, d), lambda i, j: (0, 0)),
                  pl.BlockSpec((1, 1, d), lambda i, j: (i, 0, 0)),
                  pl.BlockSpec((1, 1, d), lambda i, j: (i, 0, 0)),
                  pl.BlockSpec(w.shape, lambda i, j: (0, 0))],
        out_specs=[pl.BlockSpec((1, tm, dc), lambda i, j: (i, j, 0))] * 2,
        out_shape=[sd, sd],
        compiler_params=_cparams(("parallel", "parallel")),
        name="c_in",
    )(x, gain, sc, sh, w)


def _c_out_kernel(x_ref, bg_ref, cx_ref, cp_ref, cn_ref, cw_ref, w_ref, g_ref, o_ref, ext_ref):
    j = pl.program_id(1)
    tm = cx_ref.shape[1]
    ext_ref[0:SUBLANES] = jnp.where(j == 0, 0.0, cp_ref[0])
    ext_ref[SUBLANES:SUBLANES + tm] = cx_ref[0]
    ext_ref[SUBLANES + tm:2 * SUBLANES + tm] = jnp.where(j == pl.num_programs(1) - 1, 0.0, cn_ref[0])
    conv = cw_ref[0:1, :] * ext_ref[SUBLANES - 1:SUBLANES - 1 + tm]
    conv = conv + cw_ref[1:2, :] * ext_ref[SUBLANES:SUBLANES + tm]
    conv = conv + cw_ref[2:3, :] * ext_ref[SUBLANES + 1:SUBLANES + 1 + tm]
    m = _bdot(bg_ref[0] * conv, w_ref[...])
    o_ref[0] = x_ref[0] + g_ref[0] * m


def _c_out(x, bg, cx, conv_w, w, gate):
    b, s, d = x.shape
    dc = bg.shape[2]
    tm = min(ROW_TILE, s)
    nb8 = s // SUBLANES
    t8 = tm // SUBLANES
    return pl.pallas_call(
        _c_out_kernel,
        grid=(b, s // tm),
        in_specs=[pl.BlockSpec((1, tm, d), lambda i, j: (i, j, 0)),
                  pl.BlockSpec((1, tm, dc), lambda i, j: (i, j, 0)),
                  pl.BlockSpec((1, tm, dc), lambda i, j: (i, j, 0)),
                  pl.BlockSpec((1, SUBLANES, dc), lambda i, j: (i, jnp.maximum(j * t8 - 1, 0), 0)),
                  pl.BlockSpec((1, SUBLANES, dc), lambda i, j: (i, jnp.minimum((j + 1) * t8, nb8 - 1), 0)),
                  pl.BlockSpec(conv_w.shape, lambda i, j: (0, 0)),
                  pl.BlockSpec(w.shape, lambda i, j: (0, 0)),
                  pl.BlockSpec((1, 1, d), lambda i, j: (i, 0, 0))],
        out_specs=pl.BlockSpec((1, tm, d), lambda i, j: (i, j, 0)),
        out_shape=jax.ShapeDtypeStruct((b, s, d), F32),
        scratch_shapes=[pltpu.VMEM((tm + 2 * SUBLANES, dc), F32)],
        compiler_params=_cparams(("parallel", "parallel")),
        name="c_out",
    )(x, bg, cx, cx, cx, conv_w, w, gate)


def _top_rows(s, k, payload=None):
    n = s.shape[0]
    ridx = lax.broadcasted_iota(jnp.int32, s.shape, 0).astype(F32)
    vals, ids = [], []
    for _ in range(k):
        m = jnp.max(s, axis=0, keepdims=True)
        first = jnp.min(jnp.where(s == m, ridx, float(n)), axis=0, keepdims=True)
        sel = ridx == first
        ids.append(first if payload is None else jnp.sum(jnp.where(sel, payload, 0.0), axis=0, keepdims=True))
        vals.append(m)
        s = jnp.where(sel, -jnp.inf, s)
    return vals, ids


_PAIRS = [(a, b) for a in range(PEER_TOPK) for b in range(PEER_TOPK) if (a + 1) * (b + 1) <= PEER_TOPK]


def _route_kernel(x_ref, g_ref, sc_ref, sh_ref, wq_ref, k1_ref, k2_ref, h_ref, idx_ref, gate_ref, q_scr):
    h = _norm_mod(x_ref[0], g_ref[...], sc_ref[0], sh_ref[0])
    h_ref[0] = h
    q_scr[...] = _bdot(h, wq_ref[...])
    n_keys = k1_ref.shape[0]
    half = k1_ref.shape[1]
    tm = x_ref.shape[1]
    nt = (((1,), (1,)), ((), ()))
    pad = -len(_PAIRS) % SUBLANES

    def head(hd, carry):
        off = pl.multiple_of(hd * 2 * half, 2 * half)
        q1 = q_scr[:, pl.ds(off, half)].astype(BF16)
        q2 = q_scr[:, pl.ds(off + half, half)].astype(BF16)
        s1 = lax.dot_general(k1_ref[...], q1, nt, preferred_element_type=F32)
        s2 = lax.dot_general(k2_ref[...], q2, nt, preferred_element_type=F32)
        v1, i1 = _top_rows(s1, PEER_TOPK)
        v2, i2 = _top_rows(s2, PEER_TOPK)
        cv = [v1[a] + v2[b] for a, b in _PAIRS] + [jnp.full((pad, tm), -jnp.inf, F32)]
        ce = [i1[a] * float(n_keys) + i2[b] for a, b in _PAIRS] + [jnp.zeros((pad, tm), F32)]
        vs, es = _top_rows(jnp.concatenate(cv, axis=0), PEER_TOPK, jnp.concatenate(ce, axis=0))
        vs = jnp.concatenate(vs, axis=0)
        e = jnp.exp(vs - vs[0:1])
        row = pl.multiple_of(hd * PEER_TOPK, PEER_TOPK)
        gate_ref[0, pl.ds(row, PEER_TOPK), :] = e / jnp.sum(e, axis=0, keepdims=True)
        idx_ref[0, pl.ds(row, PEER_TOPK), :] = jnp.concatenate(es, axis=0).astype(jnp.int32)
        return carry

    lax.fori_loop(0, PEER_HEADS, head, 0)


def _route(x, gain, sc, sh, wq, k1, k2):
    b, s, d = x.shape
    tm = min(ROUTE_TILE, s)
    nsel = PEER_HEADS * PEER_TOPK
    return pl.pallas_call(
        _route_kernel,
        grid=(b, s // tm),
        in_specs=[pl.BlockSpec((1, tm, d), lambda i, j: (i, j, 0)),
                  pl.BlockSpec((1, d), lambda i, j: (0, 0)),
                  pl.BlockSpec((1, 1, d), lambda i, j: (i, 0, 0)),
                  pl.BlockSpec((1, 1, d), lambda i, j: (i, 0, 0)),
                  pl.BlockSpec(wq.shape, lambda i, j: (0, 0)),
                  pl.BlockSpec(k1.shape, lambda i, j: (0, 0)),
                  pl.BlockSpec(k2.shape, lambda i, j: (0, 0))],
        out_specs=[pl.BlockSpec((1, tm, d), lambda i, j: (i, j, 0)),
                   pl.BlockSpec((1, nsel, tm), lambda i, j: (i, 0, j)),
                   pl.BlockSpec((1, nsel, tm), lambda i, j: (i, 0, j))],
        out_shape=[jax.ShapeDtypeStruct((b, s, d), F32),
                   jax.ShapeDtypeStruct((b, nsel, s), jnp.int32),
                   jax.ShapeDtypeStruct((b, nsel, s), F32)],
        scratch_shapes=[pltpu.VMEM((tm, wq.shape[1]), F32)],
        compiler_params=_cparams(("parallel", "parallel")),
        name="peer_route",
    )(x, gain, sc, sh, wq, k1, k2)


def _pack_table(t):
    n, d = t.shape
    chunks = d // LANES
    bits = lax.bitcast_convert_type(t.astype(BF16), jnp.uint16).astype(jnp.uint32)
    bits = bits.reshape(n * chunks // 2, 2, LANES)
    return lax.bitcast_convert_type(bits[:, 0, :] | (bits[:, 1, :] << 16), jnp.int32)


def _gather_rows(idx_ref, t, tbl_ref, wr):
    rows = [tbl_ref[pl.ds(pl.multiple_of(idx_ref[t, r], wr), wr), :] for r in range(idx_ref.shape[1])]
    return pltpu.bitcast(jnp.concatenate(rows, axis=0), BF16)


def _idx_copy(idx_hbm, half_block, smem, sem):
    sub = smem.shape[0]
    return pltpu.make_async_copy(idx_hbm.at[pl.ds(half_block * sub, sub)], smem, sem)


def _staged_halves(idx_hbm, smem_a, smem_b, sems, process):
    i = pl.program_id(0)
    sub = smem_a.shape[0]

    @pl.when(i == 0)
    def _():
        _idx_copy(idx_hbm, 0, smem_a, sems.at[0]).start()

    _idx_copy(idx_hbm, 2 * i, smem_a, sems.at[0]).wait()
    _idx_copy(idx_hbm, 2 * i + 1, smem_b, sems.at[1]).start()
    process(smem_a, 0)
    _idx_copy(idx_hbm, 2 * i + 1, smem_b, sems.at[1]).wait()

    @pl.when(i + 1 < pl.num_programs(0))
    def _():
        _idx_copy(idx_hbm, 2 * i + 2, smem_a, sems.at[0]).start()

    process(smem_b, sub)


def _split_bf16(a):
    hi = a.astype(BF16).astype(F32)
    return jnp.concatenate([hi, a - hi], axis=0).astype(BF16)


def _chunk_mask(chunks, width):
    lane = lax.broadcasted_iota(jnp.int32, (chunks, width), 1)
    return lane % chunks == lax.broadcasted_iota(jnp.int32, (chunks, width), 0)


def _peer_blocks(t):
    sub = min(PEER_TB, t // 2)
    assert t % (2 * sub) == 0
    return sub, 2 * sub


def _idx_scratch(sub, nsel):
    return [pltpu.SMEM((sub, nsel), jnp.int32), pltpu.SMEM((sub, nsel), jnp.int32), pltpu.SemaphoreType.DMA((2,))]


def _pack_pairs(t):
    d = t.shape[1]
    bits = lax.bitcast_convert_type(t.astype(BF16), jnp.uint16).astype(jnp.uint32)
    return lax.bitcast_convert_type(bits[:, :d // 2] | (bits[:, d // 2:] << 16), jnp.int32)


def _sc_scores(tbl, idx, x):
    t, nsel = idx.shape
    w = tbl.shape[1]
    info = plsc.get_sparse_core_info()
    lanes = info.num_lanes
    workers = info.num_cores * info.num_subcores
    per = t // workers
    half = nsel // 2
    assert t % (workers * SC_TOKENS) == 0 and half % lanes == 0 and w % lanes == 0
    mesh = plsc.VectorSubcoreMesh(core_axis_name="c", subcore_axis_name="s")

    @functools.partial(
        pl.kernel, mesh=mesh, out_type=jax.ShapeDtypeStruct((t, nsel), F32),
        scratch_types=[pltpu.VMEM((SC_TOKENS, nsel), jnp.int32), pltpu.VMEM((SC_TOKENS, 2 * w), F32),
                       pltpu.VMEM((2, half, w), jnp.int32), pltpu.VMEM((SC_TOKENS, nsel), F32),
                       pltpu.SemaphoreType.DMA((2,))],
        compiler_params=pltpu.CompilerParams(needs_layout_passes=False),
        name="peer_scores")
    def scores(tbl_hbm, idx_hbm, x_hbm, out_hbm, idx_v, x_v, rows_v, s_v, sems):
        wid = lax.axis_index("s") * info.num_cores + lax.axis_index("c")
        lane = lax.iota(jnp.int32, lanes)

        def gather(tt, hb):
            return pltpu.make_async_copy(tbl_hbm.at[idx_v.at[tt, pl.ds(hb * half, half)]], rows_v.at[hb], sems.at[hb])

        def compute(tt, hb):
            @pl.loop(0, half // lanes)
            def _(rg):
                def chunk(c, accs):
                    xl = x_v[tt, pl.ds(c * lanes, lanes)]
                    xh = x_v[tt, pl.ds(w + c * lanes, lanes)]
                    out = []
                    for r in range(lanes):
                        wv = rows_v[hb, rg * lanes + r, pl.ds(c * lanes, lanes)]
                        lo = lax.bitcast_convert_type(wv << 16, F32)
                        hi = lax.bitcast_convert_type(wv & jnp.int32(-65536), F32)
                        out.append(accs[r] + lo * xl + hi * xh)
                    return tuple(out)

                accs = lax.fori_loop(0, w // lanes, chunk, tuple(jnp.zeros((lanes,), F32) for _ in range(lanes)))
                res = jnp.zeros((lanes,), F32)
                for r in range(lanes):
                    res = jnp.where(lane == r, jnp.sum(accs[r]), res)
                s_v[tt, pl.ds(hb * half + rg * lanes, lanes)] = res

        @pl.loop(0, per // SC_TOKENS)
        def _(bi):
            t0 = wid * per + bi * SC_TOKENS
            pltpu.sync_copy(idx_hbm.at[pl.ds(t0, SC_TOKENS)], idx_v)
            pltpu.sync_copy(x_hbm.at[pl.ds(t0, SC_TOKENS)], x_v)
            gather(0, 0).start()

            @pl.loop(0, SC_TOKENS)
            def _(tt):
                gather(tt, 1).start()
                gather(tt, 0).wait()
                compute(tt, 0)

                @pl.when(tt + 1 < SC_TOKENS)
                def _():
                    gather(tt + 1, 0).start()

                gather(tt, 1).wait()
                compute(tt, 1)

            pltpu.sync_copy(s_v, out_hbm.at[pl.ds(t0, SC_TOKENS)])

    return scores(tbl, idx, x)


def _peer_v_kernel(idx_hbm, s_ref, gts_ref, spread_ref, tbl_ref, x_ref, g_ref, o_ref, smem_a, smem_b, sems, o_scr,
                   act_ref):
    tb = x_ref.shape[0]
    chunks = x_ref.shape[1] // LANES
    width = act_ref.shape[1]
    mask = _chunk_mask(chunks, width)
    a = _split_bf16(jax.nn.gelu(s_ref[...]) * gts_ref[...])
    a = jnp.dot(a, spread_ref[...], preferred_element_type=F32)
    act_ref[...] = a[:tb] + a[tb:]

    def process(idx_ref, first):
        for t in range(idx_ref.shape[0]):
            gb = _gather_rows(idx_ref, t, tbl_ref, chunks // 2)
            a = jnp.where(mask, jnp.broadcast_to(act_ref[first + t:first + t + 1, :], (chunks, width)), 0.0)
            o = jnp.dot(_split_bf16(a), gb, preferred_element_type=F32)
            o_scr[(first + t) * chunks:(first + t + 1) * chunks, :] = o[:chunks] + o[chunks:]

    _staged_halves(idx_hbm, smem_a, smem_b, sems, process)
    for c in range(chunks):
        sl = slice(c * LANES, (c + 1) * LANES)
        o_ref[:, sl] = x_ref[:, sl] + g_ref[0][:, sl] * o_scr[pl.ds(c, tb, stride=chunks), :]


def _peer_v(idx, scores, gts, table, x, gate, tokens_per_batch):
    t, nsel = idx.shape
    d = x.shape[1]
    chunks = d // LANES
    width = nsel * chunks
    sub, tb = _peer_blocks(t)
    assert tokens_per_batch % tb == 0
    spread = jnp.kron(jnp.eye(nsel, dtype=F32), jnp.ones((1, chunks), F32)).astype(BF16)
    return pl.pallas_call(
        _peer_v_kernel,
        grid=(t // tb,),
        in_specs=[pl.BlockSpec(memory_space=pl.ANY),
                  pl.BlockSpec((tb, nsel), lambda i: (i, 0)),
                  pl.BlockSpec((tb, nsel), lambda i: (i, 0)),
                  pl.BlockSpec(spread.shape, lambda i: (0, 0), pipeline_mode=pl.Buffered(1)),
                  pl.BlockSpec(table.shape, lambda i: (0, 0), pipeline_mode=pl.Buffered(1)),
                  pl.BlockSpec((tb, d), lambda i: (i, 0)),
                  pl.BlockSpec((1, 1, d), lambda i: (i * tb // tokens_per_batch, 0, 0))],
        out_specs=pl.BlockSpec((tb, d), lambda i: (i, 0)),
        out_shape=jax.ShapeDtypeStruct(x.shape, F32),
        scratch_shapes=_idx_scratch(sub, nsel) + [pltpu.VMEM((tb * chunks, LANES), F32), pltpu.VMEM((tb, width), F32)],
        compiler_params=_cparams(("arbitrary",)),
        name="peer_v",
    )(idx, scores, gts, spread, table, x, gate)


def _peer_front(x, gain, sc, sh, wq, k1, k2, u_tbl):
    b, s, d = x.shape
    h, idx, gts = _route(x, gain, sc, sh, wq, k1, k2)
    nsel = idx.shape[1]
    idx = jnp.transpose(idx, (0, 2, 1)).reshape(b * s, nsel)
    gts = jnp.transpose(gts, (0, 2, 1)).reshape(b * s, nsel)
    return idx, gts, _sc_scores(u_tbl, idx, h.reshape(b * s, d))


def _peer_back(x, gate, idx, gts, scores, v_tbl):
    b, s, d = x.shape
    out = _peer_v(idx * (d // LANES // 2), scores, gts, v_tbl, x.reshape(b * s, d), gate, s)
    return out.reshape(b, s, d)


def _block_diag(w):
    h, i, j = w.shape
    return jnp.einsum('hij,hg->higj', w, jnp.eye(h, dtype=w.dtype)).reshape(h * i, h * j)


def _rope_tables(s):
    half = QK_ROPE // 2
    inv = 1.0 / (ROPE_THETA ** (jnp.arange(0, QK_ROPE, 2, dtype=F32) / QK_ROPE))
    ang = jnp.arange(s, dtype=F32)[:, None] * inv[None, :]
    cos, sin = jnp.cos(ang), jnp.sin(ang)
    z = jnp.zeros((s, QK_NOPE), F32)
    tail = jnp.zeros((s, HEAD_SLAB - QK_DIM), F32)
    zh = jnp.zeros((s, half), F32)
    rc = jnp.concatenate([z + 1.0, cos, cos, tail + 1.0], axis=1)
    r1 = jnp.concatenate([z, -sin, zh, tail], axis=1)
    r2 = jnp.concatenate([z, zh, sin, tail], axis=1)
    return rc, r1, r2


def _pad_last(a, n):
    return jnp.pad(a, [(0, 0)] * (a.ndim - 1) + [(0, n - a.shape[-1])])


def _mixer_ab(x, n1, sc1, sh1, g1, p, j):
    b, s, d = x.shape
    d_rnn = p['rg_conv_w'].shape[2]
    q_lora = p['mla_q_norm'].shape[1]
    kv_lora = p['mla_kv_norm'].shape[1]
    w_in = p['ab_w_in'][j]
    lat = 2 * d_rnn + q_lora + kv_lora
    w0 = jnp.concatenate([w_in[:, :lat], jnp.zeros((d, QK_NOPE), F32), w_in[:, lat:],
                          jnp.zeros((d, HEAD_SLAB - QK_DIM), F32)], axis=1).astype(BF16)
    z = _in_proj(x, n1, sc1, sh1, w0)
    wg = jnp.stack([jnp.concatenate([_block_diag(p['rg_wa'][j][k]), _block_diag(p['rg_wx'][j][k])], axis=1)
                    for k in range(2)]).astype(BF16)
    bg = jnp.concatenate([p['rg_ba'][j], p['rg_bx'][j]], axis=1)[:, None, :]
    cl = (-RG_C * jax.nn.softplus(-p['rg_lambda'][j]))[:, None, :]
    hf, hb = _rglru(z, p['rg_conv_w'][j], p['rg_conv_b'][j][None, :], wg, bg, cl)
    wuq = _pad_last(p['mla_w_uq'][j].reshape(q_lora, MLA_HEADS, QK_DIM), HEAD_SLAB)
    wuq = wuq.reshape(q_lora, MLA_HEADS * HEAD_SLAB).astype(BF16)
    wkv = p['mla_w_ukv'][j].reshape(kv_lora, MLA_HEADS, QK_NOPE + V_DIM)
    wk = _pad_last(wkv[:, :, :QK_NOPE], HEAD_SLAB).reshape(kv_lora, MLA_HEADS * HEAD_SLAB).astype(BF16)
    wv = wkv[:, :, QK_NOPE:].reshape(kv_lora, MLA_HEADS * V_DIM).astype(BF16)
    gq = _pad_last(p['mla_qn_q'][j][None, :], HEAD_SLAB)
    gk = _pad_last(p['mla_qn_k'][j][None, :], HEAD_SLAB)
    rc, r1, r2 = _rope_tables(s)
    width = q_lora + kv_lora + HEAD_SLAB
    assert (2 * d_rnn) % width == 0
    bound = 1.02 * QK_DIM ** 0.5 * jnp.max(jnp.abs(gq)) * jnp.max(jnp.abs(gk))
    spare = jnp.arange(HEAD_SLAB)[None, :] == HEAD_SLAB - 1
    qb = jnp.where(spare, 1.0, 0.0).astype(F32)
    kb = jnp.where(spare, -bound * LOG2_E, 0.0).astype(F32)
    q, k, v = _mla_proj(z, 2 * d_rnn // width, p['mla_q_norm'][j][None, :], p['mla_kv_norm'][j][None, :],
                        wuq, wk, wv, gq, gk, qb, kb, rc, r1, r2)
    attn = lax.cond(bound < MAX_SOFTMAX_BOUND, functools.partial(_attention, bounded=True),
                    functools.partial(_attention, bounded=False), q, k, v)
    return _ab_out(x, hf, hb, z, attn, p['ab_w_out'][j].astype(BF16), g1)


def _mixer_c(x, n1, sc1, sh1, g1, p, j):
    bgate, cx = _c_in(x, n1, sc1, sh1, p['c_w_in'][j].astype(BF16))
    return _c_out(x, bgate, cx, p['c_conv_w'][j], p['c_w_out'][j].astype(BF16), g1)


def _layer_front(x, mod_i, p, i):
    sh1, sc1, g1, sh2, sc2, g2 = [m[:, None, :] for m in jnp.split(mod_i, 6, axis=-1)]
    mixer = _mixer_ab if i % 2 == 0 else _mixer_c
    x = mixer(x, p['norm1_g'][i][None, :], sc1, sh1, g1, p, i // 2)
    front = _peer_front(x, p['norm2_g'][i][None, :], sc2, sh2, p['peer_wq'][i].astype(BF16),
                        p['peer_k1'][i].astype(BF16), p['peer_k2'][i].astype(BF16), _pack_pairs(p['peer_u'][i]))
    return (x, g2) + front


def _layer_back(state, p, i):
    x, g2, idx, gts, scores = state
    return _peer_back(x, g2, idx, gts, scores, _pack_table(p['peer_v'][i]))


def kernel(x_prompt, x_sample, c_prompt, c_sample, ada_w, ada_b, norm1_g, norm2_g, ab_w_in, rg_conv_w, rg_conv_b, rg_wa, rg_ba, rg_wx, rg_bx, rg_lambda, mla_q_norm, mla_w_uq, mla_kv_norm, mla_w_ukv, mla_qn_q, mla_qn_k, ab_w_out, c_w_in, c_conv_w, c_w_out, peer_wq, peer_k1, peer_k2, peer_u, peer_v):
    p = dict(ada_w=ada_w, norm1_g=norm1_g, norm2_g=norm2_g, ab_w_in=ab_w_in, rg_conv_w=rg_conv_w,
             rg_conv_b=rg_conv_b, rg_wa=rg_wa, rg_ba=rg_ba, rg_wx=rg_wx, rg_bx=rg_bx, rg_lambda=rg_lambda,
             mla_q_norm=mla_q_norm, mla_w_uq=mla_w_uq, mla_kv_norm=mla_kv_norm, mla_w_ukv=mla_w_ukv,
             mla_qn_q=mla_qn_q, mla_qn_k=mla_qn_k, ab_w_out=ab_w_out, c_w_in=c_w_in, c_conv_w=c_conv_w,
             c_w_out=c_w_out, peer_wq=peer_wq, peer_k1=peer_k1, peer_k2=peer_k2, peer_u=peer_u, peer_v=peer_v)
    bp, bs = c_prompt.shape[0], c_sample.shape[0]
    rows = -(-(bp + bs) // SUBLANES) * SUBLANES
    c_all = jnp.pad(jnp.concatenate([c_prompt, c_sample], axis=0), ((0, rows - bp - bs), (0, 0)))
    mod = _modulation(c_all, ada_w, ada_b)
    xs = [x_sample, x_prompt]
    mods = [mod[:, bp:bp + bs], mod[:, :bp]]
    for i in range(ada_w.shape[0]):
        fronts = [_layer_front(x, m[i], p, i) for x, m in zip(xs, mods)]
        xs = [_layer_back(f, p, i) for f in fronts]
    return (xs[1], xs[0])
```

```python
import functools

import jax
import jax.numpy as jnp
from jax import lax
from jax.experimental import pallas as pl
from jax.experimental.pallas import tpu as pltpu
from jax.experimental.pallas import tpu_sc as plsc

F32 = jnp.float32
BF16 = jnp.bfloat16
EPS = 1e-6
LOG2_E = 1.4426950408889634

RG_HEADS = 8
RG_CONV = 4
RG_C = 8.0
MLA_HEADS = 8
QK_NOPE = 64
QK_ROPE = 32
V_DIM = 64
QK_DIM = QK_NOPE + QK_ROPE
ROPE_THETA = 10000.0
PEER_HEADS = 8
PEER_TOPK = 16
HEAD_SLAB = 128
MAX_SOFTMAX_BOUND = 40.0

LANES = 128
SUBLANES = 8
VMEM_LIMIT = 56 * 1024 * 1024

ROW_TILE = 512
SCAN_CHUNK = 1024
ATTN_TQ = 1024
ATTN_TK = 1024
ROUTE_TILE = 256
PEER_TB = 16
SC_TOKENS = 8


def _cparams(sem):
    return pltpu.CompilerParams(dimension_semantics=sem, vmem_limit_bytes=VMEM_LIMIT)


def _norm_mod(x, gain, sc, sh):
    ms = jnp.mean(x * x, axis=-1, keepdims=True)
    return x * lax.rsqrt(ms + EPS) * gain * (1.0 + sc) + sh


def _rms(x, gain, n):
    ms = jnp.sum(x * x, axis=-1, keepdims=True) * (1.0 / n)
    return x * lax.rsqrt(ms + EPS) * gain


def _bdot(a, b):
    return jnp.dot(a.astype(BF16), b, preferred_element_type=F32)


def _mod_kernel(c_ref, w_ref, b_ref, o_ref):
    c = c_ref[...]
    s = c * jax.nn.sigmoid(c)
    o_ref[0] = _bdot(s, w_ref[0].astype(BF16)) + b_ref[0]


def _modulation(c_all, ada_w, ada_b):
    depth, d, n = ada_w.shape
    rows = c_all.shape[0]
    tn = 1536
    return pl.pallas_call(
        _mod_kernel,
        grid=(depth, n // tn),
        in_specs=[
            pl.BlockSpec((rows, d), lambda i, j: (0, 0)),
            pl.BlockSpec((1, d, tn), lambda i, j: (i, 0, j)),
            pl.BlockSpec((1, 1, tn), lambda i, j: (i, 0, j)),
        ],
        out_specs=pl.BlockSpec((1, rows, tn), lambda i, j: (i, 0, j)),
        out_shape=jax.ShapeDtypeStruct((depth, rows, n), F32),
        compiler_params=_cparams(("parallel", "parallel")),
        name="modulation",
    )(c_all, ada_w, ada_b.reshape(depth, 1, n))


def _in_proj_kernel(x_ref, g_ref, sc_ref, sh_ref, w_ref, z_ref):
    h = _norm_mod(x_ref[0], g_ref[...], sc_ref[0], sh_ref[0])
    z_ref[0] = _bdot(h, w_ref[...])


def _in_proj(x, gain, sc, sh, w):
    b, s, d = x.shape
    n = w.shape[1]
    tm = min(ROW_TILE, s)
    return pl.pallas_call(
        _in_proj_kernel,
        grid=(b, s // tm),
        in_specs=[
            pl.BlockSpec((1, tm, d), lambda i, j: (i, j, 0)),
            pl.BlockSpec((1, d), lambda i, j: (0, 0)),
            pl.BlockSpec((1, 1, d), lambda i, j: (i, 0, 0)),
            pl.BlockSpec((1, 1, d), lambda i, j: (i, 0, 0)),
            pl.BlockSpec((d, n), lambda i, j: (0, 0)),
        ],
        out_specs=pl.BlockSpec((1, tm, n), lambda i, j: (i, j, 0)),
        out_shape=jax.ShapeDtypeStruct((b, s, n), F32),
        compiler_params=_cparams(("parallel", "parallel")),
        name="in_proj",
    )(x, gain, sc, sh, w)


def _rglru_kernel(xf_ref, xfp_ref, xfn_ref, xb_ref, xbp_ref, xbn_ref, cw_ref, cb_ref, wg_ref, bg_ref, cl_ref,
                  hf_ref, hb_ref, ext_ref, a_ref, b_ref, carry_ref):
    j = pl.program_id(1)
    nc = pl.num_programs(1)
    tc = xf_ref.shape[1]
    dr = xf_ref.shape[2]
    nt = tc // SUBLANES

    @pl.when(j == 0)
    def _():
        carry_ref[...] = jnp.zeros_like(carry_ref)

    def gates(x_ref, xp_ref, xn_ref, first, last, d):
        ext_ref[0:SUBLANES] = jnp.where(first, 0.0, xp_ref[0])
        ext_ref[SUBLANES:SUBLANES + tc] = x_ref[0]
        ext_ref[SUBLANES + tc:2 * SUBLANES + tc] = jnp.where(last, 0.0, xn_ref[0])
        xc = cb_ref[...]
        for k in range(RG_CONV):
            xc = xc + cw_ref[k:k + 1, :] * ext_ref[SUBLANES - 2 + k:SUBLANES - 2 + k + tc]
        g = _bdot(xc, wg_ref[d]) + bg_ref[d]
        r = jax.nn.sigmoid(g[:, :dr])
        i = jax.nn.sigmoid(g[:, dr:])
        log_a = r * cl_ref[d]
        a = jnp.exp(log_a)
        b = jnp.sqrt(-jnp.tanh(log_a) * (1.0 + a * a)) * (i * xc)
        a_ref[d] = a
        b_ref[d] = b

    rows = lax.broadcasted_iota(jnp.int32, (SUBLANES, dr), 0)

    def scan(d, reverse, out_ref):
        def body(it, carry):
            t = (nt - 1 - it) if reverse else it
            off = pl.multiple_of(t * SUBLANES, SUBLANES)
            a = a_ref[d, pl.ds(off, SUBLANES), :]
            b = b_ref[d, pl.ds(off, SUBLANES), :]
            for s in (1, 2, 4):
                if reverse:
                    a_s = pltpu.roll(a, SUBLANES - s, 0)
                    b_s = pltpu.roll(b, SUBLANES - s, 0)
                    m = rows < SUBLANES - s
                else:
                    a_s = pltpu.roll(a, s, 0)
                    b_s = pltpu.roll(b, s, 0)
                    m = rows >= s
                b = jnp.where(m, a * b_s + b, b)
                a = jnp.where(m, a * a_s, a)
            h = b + a * carry
            out_ref[0, pl.ds(off, SUBLANES), :] = h
            edge = h[0:1] if reverse else h[SUBLANES - 1:SUBLANES]
            return jnp.broadcast_to(edge, (SUBLANES, dr))

        carry_ref[d] = lax.fori_loop(0, nt, body, carry_ref[d])

    gates(xf_ref, xfp_ref, xfn_ref, j == 0, j == nc - 1, 0)
    scan(0, False, hf_ref)
    gates(xb_ref, xbp_ref, xbn_ref, j == nc - 1, j == 0, 1)
    scan(1, True, hb_ref)


def _rglru(z, conv_w, conv_b, wg, bg, cl):
    b, s, _ = z.shape
    dr = conv_w.shape[1]
    tc = min(SCAN_CHUNK, s)
    nc = s // tc
    nb8 = s // SUBLANES
    cb8 = tc // SUBLANES

    def main(rev):
        return pl.BlockSpec((1, tc, dr), (lambda i, j: (i, nc - 1 - j, 0)) if rev else (lambda i, j: (i, j, 0)))

    def prev(rev):
        def f(i, j):
            c = (nc - 1 - j) if rev else j
            return (i, jnp.maximum(c * cb8 - 1, 0), 0)
        return pl.BlockSpec((1, SUBLANES, dr), f)

    def nxt(rev):
        def f(i, j):
            c = (nc - 1 - j) if rev else j
            return (i, jnp.minimum((c + 1) * cb8, nb8 - 1), 0)
        return pl.BlockSpec((1, SUBLANES, dr), f)

    def const(shape):
        return pl.BlockSpec(shape, lambda i, j: (0,) * len(shape))

    out_sd = jax.ShapeDtypeStruct((b, s, dr), F32)
    return pl.pallas_call(
        _rglru_kernel,
        grid=(b, nc),
        in_specs=[main(False), prev(False), nxt(False), main(True), prev(True), nxt(True),
                  const((RG_CONV, dr)), const((1, dr)), const((2, dr, 2 * dr)), const((2, 1, 2 * dr)),
                  const((2, 1, dr))],
        out_specs=[pl.BlockSpec((1, tc, dr), lambda i, j: (i, j, 0)),
                   pl.BlockSpec((1, tc, dr), lambda i, j: (i, nc - 1 - j, 0))],
        out_shape=[out_sd, out_sd],
        scratch_shapes=[pltpu.VMEM((tc + 2 * SUBLANES, dr), F32), pltpu.VMEM((2, tc, dr), F32),
                        pltpu.VMEM((2, tc, dr), F32), pltpu.VMEM((2, SUBLANES, dr), F32)],
        compiler_params=_cparams(("parallel", "arbitrary")),
        name="rglru",
    )(z, z, z, z, z, z, conv_w, conv_b, wg, bg, cl)


def _mla_proj_kernel(z_ref, qn_ref, kvn_ref, wuq_ref, wk_ref, wv_ref, gq_ref, gk_ref, qb_ref, kb_ref,
                     rc_ref, r1_ref, r2_ref, q_ref, k_ref, v_ref):
    zz = z_ref[0]
    q_lora = qn_ref.shape[1]
    kv_lora = kvn_ref.shape[1]
    ql = zz[:, :q_lora]
    kvl = zz[:, q_lora:q_lora + kv_lora]
    kr = zz[:, q_lora + kv_lora:]
    q = _bdot(_rms(ql, qn_ref[...], q_lora), wuq_ref[...])
    kvn = _rms(kvl, kvn_ref[...], kv_lora)
    kk = _bdot(kvn, wk_ref[...])
    vv = _bdot(kvn, wv_ref[...])
    rc, r1, r2 = rc_ref[...], r1_ref[...], r2_ref[...]
    half = QK_ROPE // 2
    scale = QK_DIM ** -0.5 * LOG2_E

    def norm_rope(xh, g):
        xh = _rms(xh, g, QK_DIM)
        return xh * rc + pltpu.roll(xh, HEAD_SLAB - half, 1) * r1 + pltpu.roll(xh, half, 1) * r2

    for h in range(MLA_HEADS):
        sl = slice(h * HEAD_SLAB, (h + 1) * HEAD_SLAB)
        q_ref[0, h] = (norm_rope(q[:, sl], gq_ref[...]) * scale + qb_ref[...]).astype(BF16)
        k_ref[0, h] = (norm_rope(kk[:, sl] + kr, gk_ref[...]) + kb_ref[...]).astype(BF16)
    for p in range(MLA_HEADS // 2):
        v_ref[0, p] = vv[:, p * LANES:(p + 1) * LANES].astype(BF16)


def _mla_proj(z, col_block, qn, kvn, wuq, wk, wv, gq, gk, qb, kb, rc, r1, r2):
    b, s, _ = z.shape
    tm = min(ROW_TILE, s)
    width = qn.shape[1] + kvn.shape[1] + HEAD_SLAB

    def const(a):
        return pl.BlockSpec(a.shape, lambda i, j: (0,) * a.ndim)

    def rope(a):
        return pl.BlockSpec((tm, HEAD_SLAB), lambda i, j: (j, 0))

    hp = MLA_HEADS // 2
    return pl.pallas_call(
        _mla_proj_kernel,
        grid=(b, s // tm),
        in_specs=[pl.BlockSpec((1, tm, width), lambda i, j: (i, j, col_block)),
                  const(qn), const(kvn), const(wuq), const(wk), const(wv), const(gq), const(gk), const(qb), const(kb),
                  rope(rc), rope(r1), rope(r2)],
        out_specs=[pl.BlockSpec((1, MLA_HEADS, tm, HEAD_SLAB), lambda i, j: (i, 0, j, 0)),
                   pl.BlockSpec((1, MLA_HEADS, tm, HEAD_SLAB), lambda i, j: (i, 0, j, 0)),
                   pl.BlockSpec((1, hp, tm, LANES), lambda i, j: (i, 0, j, 0))],
        out_shape=[jax.ShapeDtypeStruct((b, MLA_HEADS, s, HEAD_SLAB), BF16),
                   jax.ShapeDtypeStruct((b, MLA_HEADS, s, HEAD_SLAB), BF16),
                   jax.ShapeDtypeStruct((b, hp, s, LANES), BF16)],
        compiler_params=_cparams(("parallel", "parallel")),
        name="mla_proj",
    )(z, qn, kvn, wuq, wk, wv, gq, gk, qb, kb, rc, r1, r2)


def _flash_kernel(q_ref, k_ref, v_ref, o_ref, m_ref, l_ref, acc_ref):
    ik = pl.program_id(3)

    @pl.when(ik == 0)
    def _():
        m_ref[...] = jnp.full_like(m_ref, -jnp.inf)
        l_ref[...] = jnp.zeros_like(l_ref)
        acc_ref[...] = jnp.zeros_like(acc_ref)

    v = v_ref[0, 0]
    for hh in range(2):
        s = lax.dot_general(q_ref[0, hh], k_ref[0, hh], (((1,), (1,)), ((), ())), preferred_element_type=F32)
        m_prev = m_ref[hh]
        m_new = jnp.maximum(m_prev, jnp.max(s, axis=-1, keepdims=True))
        alpha = jnp.exp2(m_prev - m_new)
        p = jnp.exp2(s - m_new[:, :1])
        l_ref[hh] = alpha * l_ref[hh] + jnp.sum(p, axis=-1, keepdims=True)
        acc_ref[hh] = alpha * acc_ref[hh] + jnp.dot(p.astype(BF16), v, preferred_element_type=F32)
        m_ref[hh] = m_new

    @pl.when(ik == pl.num_programs(3) - 1)
    def _():
        lane = lax.broadcasted_iota(jnp.int32, acc_ref.shape[1:], 1)
        o_ref[0] = jnp.where(lane < V_DIM, acc_ref[0] / l_ref[0], acc_ref[1] / l_ref[1])


def _flash_bounded_kernel(q_ref, k_ref, v_ref, o_ref, l_ref, acc_ref):
    ik = pl.program_id(3)

    @pl.when(ik == 0)
    def _():
        l_ref[...] = jnp.zeros_like(l_ref)
        acc_ref[...] = jnp.zeros_like(acc_ref)

    v = v_ref[0, 0]
    for hh in range(2):
        s = lax.dot_general(q_ref[0, hh], k_ref[0, hh], (((1,), (1,)), ((), ())), preferred_element_type=F32)
        p = jnp.exp2(s)
        l_ref[hh] = l_ref[hh] + jnp.sum(p, axis=-1, keepdims=True)
        acc_ref[hh] = acc_ref[hh] + jnp.dot(p.astype(BF16), v, preferred_element_type=F32)

    @pl.when(ik == pl.num_programs(3) - 1)
    def _():
        lane = lax.broadcasted_iota(jnp.int32, acc_ref.shape[1:], 1)
        o_ref[0] = jnp.where(lane < V_DIM, acc_ref[0] / l_ref[0], acc_ref[1] / l_ref[1])


def _attention(q, k, v, bounded):
    b, h, s, _ = q.shape
    tq = min(ATTN_TQ, s)
    tk = min(ATTN_TK, s)
    hp = h // 2
    return pl.pallas_call(
        _flash_bounded_kernel if bounded else _flash_kernel,
        grid=(b, hp, s // tq, s // tk),
        in_specs=[pl.BlockSpec((1, 2, tq, HEAD_SLAB), lambda i, p, a, c: (i, p, a, 0)),
                  pl.BlockSpec((1, 2, tk, HEAD_SLAB), lambda i, p, a, c: (i, p, c, 0)),
                  pl.BlockSpec((1, 1, tk, LANES), lambda i, p, a, c: (i, p, c, 0))],
        out_specs=pl.BlockSpec((1, tq, LANES), lambda i, p, a, c: (i, a, p)),
        out_shape=jax.ShapeDtypeStruct((b, s, hp * LANES), F32),
        scratch_shapes=[pltpu.VMEM((2, tq, LANES), F32)] * (2 if bounded else 3),
        compiler_params=_cparams(("parallel", "parallel", "parallel", "arbitrary")),
        name="attention_bounded" if bounded else "attention",
    )(q, k, v)


def _ab_out_kernel(x_ref, hf_ref, hb_ref, y_ref, at_ref, w_ref, g_ref, o_ref):
    rg = (hf_ref[0] + hb_ref[0]) * jax.nn.gelu(y_ref[0])
    cat = jnp.concatenate([rg.astype(BF16), at_ref[0].astype(BF16)], axis=-1)
    m = jnp.dot(cat, w_ref[...], preferred_element_type=F32)
    o_ref[0] = x_ref[0] + g_ref[0] * m


def _ab_out(x, hf, hb, z, attn, w, gate):
    b, s, d = x.shape
    dr = hf.shape[2]
    da = attn.shape[2]
    tm = min(ROW_TILE, s)
    return pl.pallas_call(
        _ab_out_kernel,
        grid=(b, s // tm),
        in_specs=[pl.BlockSpec((1, tm, d), lambda i, j: (i, j, 0)),
                  pl.BlockSpec((1, tm, dr), lambda i, j: (i, j, 0)),
                  pl.BlockSpec((1, tm, dr), lambda i, j: (i, j, 0)),
                  pl.BlockSpec((1, tm, dr), lambda i, j: (i, j, 1)),
                  pl.BlockSpec((1, tm, da), lambda i, j: (i, j, 0)),
                  pl.BlockSpec(w.shape, lambda i, j: (0, 0)),
                  pl.BlockSpec((1, 1, d), lambda i, j: (i, 0, 0))],
        out_specs=pl.BlockSpec((1, tm, d), lambda i, j: (i, j, 0)),
        out_shape=jax.ShapeDtypeStruct((b, s, d), F32),
        compiler_params=_cparams(("parallel", "parallel")),
        name="ab_out",
    )(x, hf, hb, z, attn, w, gate)


def _c_in_kernel(x_ref, g_ref, sc_ref, sh_ref, w_ref, bg_ref, cx_ref):
    h = _norm_mod(x_ref[0], g_ref[...], sc_ref[0], sh_ref[0])
    z = _bdot(h, w_ref[...])
    dc = bg_ref.shape[2]
    bg_ref[0] = z[:, :dc]
    cx_ref[0] = z[:, dc:2 * dc] * z[:, 2 * dc:]


def _c_in(x, gain, sc, sh, w):
    b, s, d = x.shape
    dc = w.shape[1] // 3
    tm = min(ROW_TILE, s)
    sd = jax.ShapeDtypeStruct((b, s, dc), F32)
    return pl.pallas_call(
        _c_in_kernel,
        grid=(b, s // tm),
        in_specs=[pl.BlockSpec((1, tm, d), lambda i, j: (i, j, 0)),
                  pl.BlockSpec((1, d), lambda i, j: (0, 0)),
                  pl.BlockSpec((1, 1, d), lambda i, j: (i, 0, 0)),
                  pl.BlockSpec((1, 1, d), lambda i, j: (i, 0, 0)),
                  pl.BlockSpec(w.shape, lambda i, j: (0, 0))],
        out_specs=[pl.BlockSpec((1, tm, dc), lambda i, j: (i, j, 0))] * 2,
        out_shape=[sd, sd],
        compiler_params=_cparams(("parallel", "parallel")),
        name="c_in",
    )(x, gain, sc, sh, w)


def _c_out_kernel(x_ref, bg_ref, cx_ref, cp_ref, cn_ref, cw_ref, w_ref, g_ref, o_ref, ext_ref):
    j = pl.program_id(1)
    tm = cx_ref.shape[1]
    ext_ref[0:SUBLANES] = jnp.where(j == 0, 0.0, cp_ref[0])
    ext_ref[SUBLANES:SUBLANES + tm] = cx_ref[0]
    ext_ref[SUBLANES + tm:2 * SUBLANES + tm] = jnp.where(j == pl.num_programs(1) - 1, 0.0, cn_ref[0])
    conv = cw_ref[0:1, :] * ext_ref[SUBLANES - 1:SUBLANES - 1 + tm]
    conv = conv + cw_ref[1:2, :] * ext_ref[SUBLANES:SUBLANES + tm]
    conv = conv + cw_ref[2:3, :] * ext_ref[SUBLANES + 1:SUBLANES + 1 + tm]
    m = _bdot(bg_ref[0] * conv, w_ref[...])
    o_ref[0] = x_ref[0] + g_ref[0] * m


def _c_out(x, bg, cx, conv_w, w, gate):
    b, s, d = x.shape
    dc = bg.shape[2]
    tm = min(ROW_TILE, s)
    nb8 = s // SUBLANES
    t8 = tm // SUBLANES
    return pl.pallas_call(
        _c_out_kernel,
        grid=(b, s // tm),
        in_specs=[pl.BlockSpec((1, tm, d), lambda i, j: (i, j, 0)),
                  pl.BlockSpec((1, tm, dc), lambda i, j: (i, j, 0)),
                  pl.BlockSpec((1, tm, dc), lambda i, j: (i, j, 0)),
                  pl.BlockSpec((1, SUBLANES, dc), lambda i, j: (i, jnp.maximum(j * t8 - 1, 0), 0)),
                  pl.BlockSpec((1, SUBLANES, dc), lambda i, j: (i, jnp.minimum((j + 1) * t8, nb8 - 1), 0)),
                  pl.BlockSpec(conv_w.shape, lambda i, j: (0, 0)),
                  pl.BlockSpec(w.shape, lambda i, j: (0, 0)),
                  pl.BlockSpec((1, 1, d), lambda i, j: (i, 0, 0))],
        out_specs=pl.BlockSpec((1, tm, d), lambda i, j: (i, j, 0)),
        out_shape=jax.ShapeDtypeStruct((b, s, d), F32),
        scratch_shapes=[pltpu.VMEM((tm + 2 * SUBLANES, dc), F32)],
        compiler_params=_cparams(("parallel", "parallel")),
        name="c_out",
    )(x, bg, cx, cx, cx, conv_w, w, gate)


def _top_rows(s, k, payload=None):
    n = s.shape[0]
    ridx = lax.broadcasted_iota(jnp.int32, s.shape, 0).astype(F32)
    vals, ids = [], []
    for _ in range(k):
        m = jnp.max(s, axis=0, keepdims=True)
        first = jnp.min(jnp.where(s == m, ridx, float(n)), axis=0, keepdims=True)
        sel = ridx == first
        ids.append(first if payload is None else jnp.sum(jnp.where(sel, payload, 0.0), axis=0, keepdims=True))
        vals.append(m)
        s = jnp.where(sel, -jnp.inf, s)
    return vals, ids


_PAIRS = [(a, b) for a in range(PEER_TOPK) for b in range(PEER_TOPK) if (a + 1) * (b + 1) <= PEER_TOPK]


def _route_kernel(x_ref, g_ref, sc_ref, sh_ref, wq_ref, k1_ref, k2_ref, h_ref, idx_ref, gate_ref, q_scr, sel_scr):
    h = _norm_mod(x_ref[0], g_ref[...], sc_ref[0], sh_ref[0])
    h_ref[0] = h
    q_scr[...] = _bdot(h, wq_ref[...])
    n_keys = k1_ref.shape[0]
    half = k1_ref.shape[1]
    tm = x_ref.shape[1]
    nt = (((1,), (1,)), ((), ()))
    pad = -len(_PAIRS) % SUBLANES

    def head(hd, carry):
        off = pl.multiple_of(hd * 2 * half, 2 * half)
        q1 = q_scr[:, pl.ds(off, half)].astype(BF16)
        q2 = q_scr[:, pl.ds(off + half, half)].astype(BF16)
        s1 = lax.dot_general(k1_ref[...], q1, nt, preferred_element_type=F32)
        s2 = lax.dot_general(k2_ref[...], q2, nt, preferred_element_type=F32)
        v1, i1 = _top_rows(s1, PEER_TOPK)
        v2, i2 = _top_rows(s2, PEER_TOPK)
        cv = [v1[a] + v2[b] for a, b in _PAIRS] + [jnp.full((pad, tm), -jnp.inf, F32)]
        ce = [i1[a] * float(n_keys) + i2[b] for a, b in _PAIRS] + [jnp.zeros((pad, tm), F32)]
        vs, es = _top_rows(jnp.concatenate(cv, axis=0), PEER_TOPK, jnp.concatenate(ce, axis=0))
        vs = jnp.concatenate(vs, axis=0)
        e = jnp.exp(vs - vs[0:1])
        row = pl.multiple_of(hd * PEER_TOPK, PEER_TOPK)
        sel_scr[0, pl.ds(row, PEER_TOPK), :] = e / jnp.sum(e, axis=0, keepdims=True)
        sel_scr[1, pl.ds(row, PEER_TOPK), :] = jnp.concatenate(es, axis=0)
        return carry

    lax.fori_loop(0, PEER_HEADS, head, 0)
    gate_ref[0] = sel_scr[0].T
    idx_ref[0] = sel_scr[1].T.astype(jnp.int32)


def _route(x, gain, sc, sh, wq, k1, k2):
    b, s, d = x.shape
    tm = min(ROUTE_TILE, s)
    nsel = PEER_HEADS * PEER_TOPK
    return pl.pallas_call(
        _route_kernel,
        grid=(b, s // tm),
        in_specs=[pl.BlockSpec((1, tm, d), lambda i, j: (i, j, 0)),
                  pl.BlockSpec((1, d), lambda i, j: (0, 0)),
                  pl.BlockSpec((1, 1, d), lambda i, j: (i, 0, 0)),
                  pl.BlockSpec((1, 1, d), lambda i, j: (i, 0, 0)),
                  pl.BlockSpec(wq.shape, lambda i, j: (0, 0)),
                  pl.BlockSpec(k1.shape, lambda i, j: (0, 0)),
                  pl.BlockSpec(k2.shape, lambda i, j: (0, 0))],
        out_specs=[pl.BlockSpec((1, tm, d), lambda i, j: (i, j, 0)),
                   pl.BlockSpec((1, tm, nsel), lambda i, j: (i, j, 0)),
                   pl.BlockSpec((1, tm, nsel), lambda i, j: (i, j, 0))],
        out_shape=[jax.ShapeDtypeStruct((b, s, d), F32),
                   jax.ShapeDtypeStruct((b, s, nsel), jnp.int32),
                   jax.ShapeDtypeStruct((b, s, nsel), F32)],
        scratch_shapes=[pltpu.VMEM((tm, wq.shape[1]), F32), pltpu.VMEM((2, nsel, tm), F32)],
        compiler_params=_cparams(("parallel", "parallel")),
        name="peer_route",
    )(x, gain, sc, sh, wq, k1, k2)


def _pack_table(t):
    n, d = t.shape
    chunks = d // LANES
    bits = lax.bitcast_convert_type(t.astype(BF16), jnp.uint16).astype(jnp.uint32)
    bits = bits.reshape(n * chunks // 2, 2, LANES)
    return lax.bitcast_convert_type(bits[:, 0, :] | (bits[:, 1, :] << 16), jnp.int32)


def _gather_rows(idx_ref, t, tbl_ref, wr):
    rows = [tbl_ref[pl.ds(pl.multiple_of(idx_ref[t, r], wr), wr), :] for r in range(idx_ref.shape[1])]
    return pltpu.bitcast(jnp.concatenate(rows, axis=0), BF16)


def _idx_copy(idx_hbm, half_block, smem, sem):
    sub = smem.shape[0]
    return pltpu.make_async_copy(idx_hbm.at[pl.ds(half_block * sub, sub)], smem, sem)


def _staged_halves(idx_hbm, smem_a, smem_b, sems, process):
    i = pl.program_id(0)
    sub = smem_a.shape[0]

    @pl.when(i == 0)
    def _():
        _idx_copy(idx_hbm, 0, smem_a, sems.at[0]).start()

    _idx_copy(idx_hbm, 2 * i, smem_a, sems.at[0]).wait()
    _idx_copy(idx_hbm, 2 * i + 1, smem_b, sems.at[1]).start()
    process(smem_a, 0)
    _idx_copy(idx_hbm, 2 * i + 1, smem_b, sems.at[1]).wait()

    @pl.when(i + 1 < pl.num_programs(0))
    def _():
        _idx_copy(idx_hbm, 2 * i + 2, smem_a, sems.at[0]).start()

    process(smem_b, sub)


def _split_bf16(a):
    hi = a.astype(BF16).astype(F32)
    return jnp.concatenate([hi, a - hi], axis=0).astype(BF16)


def _chunk_mask(chunks, width):
    lane = lax.broadcasted_iota(jnp.int32, (chunks, width), 1)
    return lane % chunks == lax.broadcasted_iota(jnp.int32, (chunks, width), 0)


def _peer_blocks(t):
    sub = min(PEER_TB, t // 2)
    assert t % (2 * sub) == 0
    return sub, 2 * sub


def _idx_scratch(sub, nsel):
    return [pltpu.SMEM((sub, nsel), jnp.int32), pltpu.SMEM((sub, nsel), jnp.int32), pltpu.SemaphoreType.DMA((2,))]


def _pack_pairs(t):
    d = t.shape[1]
    bits = lax.bitcast_convert_type(t.astype(BF16), jnp.uint16).astype(jnp.uint32)
    return lax.bitcast_convert_type(bits[:, :d // 2] | (bits[:, d // 2:] << 16), jnp.int32)


def _sc_scores(tbl, idx, x):
    t, nsel = idx.shape
    w = tbl.shape[1]
    info = plsc.get_sparse_core_info()
    lanes = info.num_lanes
    workers = info.num_cores * info.num_subcores
    per = t // workers
    half = nsel // 2
    assert t % (workers * SC_TOKENS) == 0 and half % lanes == 0 and w % lanes == 0
    mesh = plsc.VectorSubcoreMesh(core_axis_name="c", subcore_axis_name="s")

    @functools.partial(
        pl.kernel, mesh=mesh, out_type=jax.ShapeDtypeStruct((t, nsel), F32),
        scratch_types=[pltpu.VMEM((SC_TOKENS, nsel), jnp.int32), pltpu.VMEM((SC_TOKENS, 2 * w), F32),
                       pltpu.VMEM((2, half, w), jnp.int32), pltpu.VMEM((SC_TOKENS, nsel), F32),
                       pltpu.SemaphoreType.DMA((2,))],
        compiler_params=pltpu.CompilerParams(needs_layout_passes=False),
        name="peer_scores")
    def scores(tbl_hbm, idx_hbm, x_hbm, out_hbm, idx_v, x_v, rows_v, s_v, sems):
        wid = lax.axis_index("s") * info.num_cores + lax.axis_index("c")
        lane = lax.iota(jnp.int32, lanes)

        def gather(tt, hb):
            return pltpu.make_async_copy(tbl_hbm.at[idx_v.at[tt, pl.ds(hb * half, half)]], rows_v.at[hb], sems.at[hb])

        def compute(tt, hb):
            @pl.loop(0, half // lanes)
            def _(rg):
                def chunk(c, accs):
                    xl = x_v[tt, pl.ds(c * lanes, lanes)]
                    xh = x_v[tt, pl.ds(w + c * lanes, lanes)]
                    out = []
                    for r in range(lanes):
                        wv = rows_v[hb, rg * lanes + r, pl.ds(c * lanes, lanes)]
                        lo = lax.bitcast_convert_type(wv << 16, F32)
                        hi = lax.bitcast_convert_type(wv & jnp.int32(-65536), F32)
                        out.append(accs[r] + lo * xl + hi * xh)
                    return tuple(out)

                accs = lax.fori_loop(0, w // lanes, chunk, tuple(jnp.zeros((lanes,), F32) for _ in range(lanes)))
                res = jnp.zeros((lanes,), F32)
                for r in range(lanes):
                    res = jnp.where(lane == r, jnp.sum(accs[r]), res)
                s_v[tt, pl.ds(hb * half + rg * lanes, lanes)] = res

        @pl.loop(0, per // SC_TOKENS)
        def _(bi):
            t0 = wid * per + bi * SC_TOKENS
            pltpu.sync_copy(idx_hbm.at[pl.ds(t0, SC_TOKENS)], idx_v)
            pltpu.sync_copy(x_hbm.at[pl.ds(t0, SC_TOKENS)], x_v)
            gather(0, 0).start()

            @pl.loop(0, SC_TOKENS)
            def _(tt):
                gather(tt, 1).start()
                gather(tt, 0).wait()
                compute(tt, 0)

                @pl.when(tt + 1 < SC_TOKENS)
                def _():
                    gather(tt + 1, 0).start()

                gather(tt, 1).wait()
                compute(tt, 1)

            pltpu.sync_copy(s_v, out_hbm.at[pl.ds(t0, SC_TOKENS)])

    return scores(tbl, idx, x)


def _peer_v_kernel(idx_hbm, s_ref, gts_ref, spread_ref, tbl_ref, x_ref, g_ref, o_ref, smem_a, smem_b, sems, o_scr,
                   act_ref):
    tb = x_ref.shape[0]
    chunks = x_ref.shape[1] // LANES
    width = act_ref.shape[1]
    mask = _chunk_mask(chunks, width)
    a = _split_bf16(jax.nn.gelu(s_ref[...]) * gts_ref[...])
    a = jnp.dot(a, spread_ref[...], preferred_element_type=F32)
    act_ref[...] = a[:tb] + a[tb:]

    def process(idx_ref, first):
        for t in range(idx_ref.shape[0]):
            gb = _gather_rows(idx_ref, t, tbl_ref, chunks // 2)
            a = jnp.where(mask, jnp.broadcast_to(act_ref[first + t:first + t + 1, :], (chunks, width)), 0.0)
            o = jnp.dot(_split_bf16(a), gb, preferred_element_type=F32)
            o_scr[(first + t) * chunks:(first + t + 1) * chunks, :] = o[:chunks] + o[chunks:]

    _staged_halves(idx_hbm, smem_a, smem_b, sems, process)
    for c in range(chunks):
        sl = slice(c * LANES, (c + 1) * LANES)
        o_ref[:, sl] = x_ref[:, sl] + g_ref[0][:, sl] * o_scr[pl.ds(c, tb, stride=chunks), :]


def _peer_v(idx, scores, gts, table, x, gate, tokens_per_batch):
    t, nsel = idx.shape
    d = x.shape[1]
    chunks = d // LANES
    width = nsel * chunks
    sub, tb = _peer_blocks(t)
    assert tokens_per_batch % tb == 0
    spread = jnp.kron(jnp.eye(nsel, dtype=F32), jnp.ones((1, chunks), F32)).astype(BF16)
    return pl.pallas_call(
        _peer_v_kernel,
        grid=(t // tb,),
        in_specs=[pl.BlockSpec(memory_space=pl.ANY),
                  pl.BlockSpec((tb, nsel), lambda i: (i, 0)),
                  pl.BlockSpec((tb, nsel), lambda i: (i, 0)),
                  pl.BlockSpec(spread.shape, lambda i: (0, 0), pipeline_mode=pl.Buffered(1)),
                  pl.BlockSpec(table.shape, lambda i: (0, 0), pipeline_mode=pl.Buffered(1)),
                  pl.BlockSpec((tb, d), lambda i: (i, 0)),
                  pl.BlockSpec((1, 1, d), lambda i: (i * tb // tokens_per_batch, 0, 0))],
        out_specs=pl.BlockSpec((tb, d), lambda i: (i, 0)),
        out_shape=jax.ShapeDtypeStruct(x.shape, F32),
        scratch_shapes=_idx_scratch(sub, nsel) + [pltpu.VMEM((tb * chunks, LANES), F32), pltpu.VMEM((tb, width), F32)],
        compiler_params=_cparams(("arbitrary",)),
        name="peer_v",
    )(idx, scores, gts, spread, table, x, gate)


def _peer_front(x, gain, sc, sh, wq, k1, k2, u_tbl):
    b, s, d = x.shape
    h, idx, gts = _route(x, gain, sc, sh, wq, k1, k2)
    nsel = idx.shape[2]
    idx = idx.reshape(b * s, nsel)
    return idx, gts.reshape(b * s, nsel), _sc_scores(u_tbl, idx, h.reshape(b * s, d))


def _peer_back(x, gate, idx, gts, scores, v_tbl):
    b, s, d = x.shape
    out = _peer_v(idx * (d // LANES // 2), scores, gts, v_tbl, x.reshape(b * s, d), gate, s)
    return out.reshape(b, s, d)


def _block_diag(w):
    h, i, j = w.shape
    return jnp.einsum('hij,hg->higj', w, jnp.eye(h, dtype=w.dtype)).reshape(h * i, h * j)


def _rope_tables(s):
    half = QK_ROPE // 2
    inv = 1.0 / (ROPE_THETA ** (jnp.arange(0, QK_ROPE, 2, dtype=F32) / QK_ROPE))
    ang = jnp.arange(s, dtype=F32)[:, None] * inv[None, :]
    cos, sin = jnp.cos(ang), jnp.sin(ang)
    z = jnp.zeros((s, QK_NOPE), F32)
    tail = jnp.zeros((s, HEAD_SLAB - QK_DIM), F32)
    zh = jnp.zeros((s, half), F32)
    rc = jnp.concatenate([z + 1.0, cos, cos, tail + 1.0], axis=1)
    r1 = jnp.concatenate([z, -sin, zh, tail], axis=1)
    r2 = jnp.concatenate([z, zh, sin, tail], axis=1)
    return rc, r1, r2


def _pad_last(a, n):
    return jnp.pad(a, [(0, 0)] * (a.ndim - 1) + [(0, n - a.shape[-1])])


def _mixer_ab(x, n1, sc1, sh1, g1, p, j):
    b, s, d = x.shape
    d_rnn = p['rg_conv_w'].shape[2]
    q_lora = p['mla_q_norm'].shape[1]
    kv_lora = p['mla_kv_norm'].shape[1]
    w_in = p['ab_w_in'][j]
    lat = 2 * d_rnn + q_lora + kv_lora
    w0 = jnp.concatenate([w_in[:, :lat], jnp.zeros((d, QK_NOPE), F32), w_in[:, lat:],
                          jnp.zeros((d, HEAD_SLAB - QK_DIM), F32)], axis=1).astype(BF16)
    z = _in_proj(x, n1, sc1, sh1, w0)
    wg = jnp.stack([jnp.concatenate([_block_diag(p['rg_wa'][j][k]), _block_diag(p['rg_wx'][j][k])], axis=1)
                    for k in range(2)]).astype(BF16)
    bg = jnp.concatenate([p['rg_ba'][j], p['rg_bx'][j]], axis=1)[:, None, :]
    cl = (-RG_C * jax.nn.softplus(-p['rg_lambda'][j]))[:, None, :]
    hf, hb = _rglru(z, p['rg_conv_w'][j], p['rg_conv_b'][j][None, :], wg, bg, cl)
    wuq = _pad_last(p['mla_w_uq'][j].reshape(q_lora, MLA_HEADS, QK_DIM), HEAD_SLAB)
    wuq = wuq.reshape(q_lora, MLA_HEADS * HEAD_SLAB).astype(BF16)
    wkv = p['mla_w_ukv'][j].reshape(kv_lora, MLA_HEADS, QK_NOPE + V_DIM)
    wk = _pad_last(wkv[:, :, :QK_NOPE], HEAD_SLAB).reshape(kv_lora, MLA_HEADS * HEAD_SLAB).astype(BF16)
    wv = wkv[:, :, QK_NOPE:].reshape(kv_lora, MLA_HEADS * V_DIM).astype(BF16)
    gq = _pad_last(p['mla_qn_q'][j][None, :], HEAD_SLAB)
    gk = _pad_last(p['mla_qn_k'][j][None, :], HEAD_SLAB)
    rc, r1, r2 = _rope_tables(s)
    width = q_lora + kv_lora + HEAD_SLAB
    assert (2 * d_rnn) % width == 0
    bound = 1.02 * QK_DIM ** 0.5 * jnp.max(jnp.abs(gq)) * jnp.max(jnp.abs(gk))
    spare = jnp.arange(HEAD_SLAB)[None, :] == HEAD_SLAB - 1
    qb = jnp.where(spare, 1.0, 0.0).astype(F32)
    kb = jnp.where(spare, -bound * LOG2_E, 0.0).astype(F32)
    q, k, v = _mla_proj(z, 2 * d_rnn // width, p['mla_q_norm'][j][None, :], p['mla_kv_norm'][j][None, :],
                        wuq, wk, wv, gq, gk, qb, kb, rc, r1, r2)
    attn = lax.cond(bound < MAX_SOFTMAX_BOUND, functools.partial(_attention, bounded=True),
                    functools.partial(_attention, bounded=False), q, k, v)
    return _ab_out(x, hf, hb, z, attn, p['ab_w_out'][j].astype(BF16), g1)


def _mixer_c(x, n1, sc1, sh1, g1, p, j):
    bgate, cx = _c_in(x, n1, sc1, sh1, p['c_w_in'][j].astype(BF16))
    return _c_out(x, bgate, cx, p['c_conv_w'][j], p['c_w_out'][j].astype(BF16), g1)


def _layer_front(x, mod_i, p, i):
    sh1, sc1, g1, sh2, sc2, g2 = [m[:, None, :] for m in jnp.split(mod_i, 6, axis=-1)]
    mixer = _mixer_ab if i % 2 == 0 else _mixer_c
    x = mixer(x, p['norm1_g'][i][None, :], sc1, sh1, g1, p, i // 2)
    front = _peer_front(x, p['norm2_g'][i][None, :], sc2, sh2, p['peer_wq'][i].astype(BF16),
                        p['peer_k1'][i].astype(BF16), p['peer_k2'][i].astype(BF16), _pack_pairs(p['peer_u'][i]))
    return (x, g2) + front


def _layer_back(state, p, i):
    x, g2, idx, gts, scores = state
    return _peer_back(x, g2, idx, gts, scores, _pack_table(p['peer_v'][i]))


def kernel(x_prompt, x_sample, c_prompt, c_sample, ada_w, ada_b, norm1_g, norm2_g, ab_w_in, rg_conv_w, rg_conv_b, rg_wa, rg_ba, rg_wx, rg_bx, rg_lambda, mla_q_norm, mla_w_uq, mla_kv_norm, mla_w_ukv, mla_qn_q, mla_qn_k, ab_w_out, c_w_in, c_conv_w, c_w_out, peer_wq, peer_k1, peer_k2, peer_u, peer_v):
    p = dict(ada_w=ada_w, norm1_g=norm1_g, norm2_g=norm2_g, ab_w_in=ab_w_in, rg_conv_w=rg_conv_w,
             rg_conv_b=rg_conv_b, rg_wa=rg_wa, rg_ba=rg_ba, rg_wx=rg_wx, rg_bx=rg_bx, rg_lambda=rg_lambda,
             mla_q_norm=mla_q_norm, mla_w_uq=mla_w_uq, mla_kv_norm=mla_kv_norm, mla_w_ukv=mla_w_ukv,
             mla_qn_q=mla_qn_q, mla_qn_k=mla_qn_k, ab_w_out=ab_w_out, c_w_in=c_w_in, c_conv_w=c_conv_w,
             c_w_out=c_w_out, peer_wq=peer_wq, peer_k1=peer_k1, peer_k2=peer_k2, peer_u=peer_u, peer_v=peer_v)
    bp, bs = c_prompt.shape[0], c_sample.shape[0]
    rows = -(-(bp + bs) // SUBLANES) * SUBLANES
    c_all = jnp.pad(jnp.concatenate([c_prompt, c_sample], axis=0), ((0, rows - bp - bs), (0, 0)))
    mod = _modulation(c_all, ada_w, ada_b)
    xs = [x_sample, x_prompt]
    mods = [mod[:, bp:bp + bs], mod[:, :bp]]
    for i in range(ada_w.shape[0]):
        fronts = [_layer_front(x, m[i], p, i) for x, m in zip(xs, mods)]
        xs = [_layer_back(f, p, i) for f in fronts]
    return (xs[1], xs[0])
```

```python
import functools

import jax
import jax.numpy as jnp
from jax import lax
from jax.experimental import pallas as pl
from jax.experimental.pallas import tpu as pltpu
from jax.experimental.pallas import tpu_sc as plsc

F32 = jnp.float32
BF16 = jnp.bfloat16
EPS = 1e-6
LOG2_E = 1.4426950408889634

RG_HEADS = 8
RG_CONV = 4
RG_C = 8.0
MLA_HEADS = 8
QK_NOPE = 64
QK_ROPE = 32
V_DIM = 64
QK_DIM = QK_NOPE + QK_ROPE
ROPE_THETA = 10000.0
PEER_HEADS = 8
PEER_TOPK = 16
HEAD_SLAB = 128
MAX_SOFTMAX_BOUND = 40.0

LANES = 128
SUBLANES = 8
VMEM_LIMIT = 56 * 1024 * 1024

ROW_TILE = 512
SCAN_CHUNK = 1024
ATTN_TQ = 1024
ATTN_TK = 1024
ROUTE_TILE = 256
PEER_TB = 16
SC_TOKENS = 8


def _cparams(sem):
    return pltpu.CompilerParams(dimension_semantics=sem, vmem_limit_bytes=VMEM_LIMIT)


def _norm_mod(x, gain, sc, sh):
    ms = jnp.mean(x * x, axis=-1, keepdims=True)
    return x * lax.rsqrt(ms + EPS) * gain * (1.0 + sc) + sh


def _rms(x, gain, n):
    ms = jnp.sum(x * x, axis=-1, keepdims=True) * (1.0 / n)
    return x * lax.rsqrt(ms + EPS) * gain


def _bdot(a, b):
    return jnp.dot(a.astype(BF16), b, preferred_element_type=F32)


def _mod_kernel(c_ref, w_ref, b_ref, o_ref):
    c = c_ref[...]
    s = c * jax.nn.sigmoid(c)
    o_ref[0] = _bdot(s, w_ref[0].astype(BF16)) + b_ref[0]


def _modulation(c_all, ada_w, ada_b):
    depth, d, n = ada_w.shape
    rows = c_all.shape[0]
    tn = 1536
    return pl.pallas_call(
        _mod_kernel,
        grid=(depth, n // tn),
        in_specs=[
            pl.BlockSpec((rows, d), lambda i, j: (0, 0)),
            pl.BlockSpec((1, d, tn), lambda i, j: (i, 0, j)),
            pl.BlockSpec((1, 1, tn), lambda i, j: (i, 0, j)),
        ],
        out_specs=pl.BlockSpec((1, rows, tn), lambda i, j: (i, 0, j)),
        out_shape=jax.ShapeDtypeStruct((depth, rows, n), F32),
        compiler_params=_cparams(("parallel", "parallel")),
        name="modulation",
    )(c_all, ada_w, ada_b.reshape(depth, 1, n))


def _in_proj_kernel(x_ref, g_ref, sc_ref, sh_ref, w_ref, z_ref):
    h = _norm_mod(x_ref[0], g_ref[...], sc_ref[0], sh_ref[0])
    z_ref[0] = _bdot(h, w_ref[...])


def _in_proj(x, gain, sc, sh, w):
    b, s, d = x.shape
    n = w.shape[1]
    tm = min(ROW_TILE, s)
    return pl.pallas_call(
        _in_proj_kernel,
        grid=(b, s // tm),
        in_specs=[
            pl.BlockSpec((1, tm, d), lambda i, j: (i, j, 0)),
            pl.BlockSpec((1, d), lambda i, j: (0, 0)),
            pl.BlockSpec((1, 1, d), lambda i, j: (i, 0, 0)),
            pl.BlockSpec((1, 1, d), lambda i, j: (i, 0, 0)),
            pl.BlockSpec((d, n), lambda i, j: (0, 0)),
        ],
        out_specs=pl.BlockSpec((1, tm, n), lambda i, j: (i, j, 0)),
        out_shape=jax.ShapeDtypeStruct((b, s, n), F32),
        compiler_params=_cparams(("parallel", "parallel")),
        name="in_proj",
    )(x, gain, sc, sh, w)


def _rglru_kernel(xf_ref, xfp_ref, xfn_ref, xb_ref, xbp_ref, xbn_ref, cw_ref, cb_ref, wg_ref, bg_ref, cl_ref,
                  hf_ref, hb_ref, ext_ref, a_ref, b_ref, carry_ref):
    j = pl.program_id(1)
    nc = pl.num_programs(1)
    tc = xf_ref.shape[1]
    dr = xf_ref.shape[2]
    nt = tc // SUBLANES

    @pl.when(j == 0)
    def _():
        carry_ref[...] = jnp.zeros_like(carry_ref)

    def gates(x_ref, xp_ref, xn_ref, first, last, d):
        ext_ref[0:SUBLANES] = jnp.where(first, 0.0, xp_ref[0])
        ext_ref[SUBLANES:SUBLANES + tc] = x_ref[0]
        ext_ref[SUBLANES + tc:2 * SUBLANES + tc] = jnp.where(last, 0.0, xn_ref[0])
        xc = cb_ref[...]
        for k in range(RG_CONV):
            xc = xc + cw_ref[k:k + 1, :] * ext_ref[SUBLANES - 2 + k:SUBLANES - 2 + k + tc]
        g = _bdot(xc, wg_ref[d]) + bg_ref[d]
        r = jax.nn.sigmoid(g[:, :dr])
        i = jax.nn.sigmoid(g[:, dr:])
        log_a = r * cl_ref[d]
        a = jnp.exp(log_a)
        b = jnp.sqrt(-jnp.tanh(log_a) * (1.0 + a * a)) * (i * xc)
        a_ref[d] = a
        b_ref[d] = b

    rows = lax.broadcasted_iota(jnp.int32, (SUBLANES, dr), 0)

    def scan(d, reverse, out_ref):
        def body(it, carry):
            t = (nt - 1 - it) if reverse else it
            off = pl.multiple_of(t * SUBLANES, SUBLANES)
            a = a_ref[d, pl.ds(off, SUBLANES), :]
            b = b_ref[d, pl.ds(off, SUBLANES), :]
            for s in (1, 2, 4):
                if reverse:
                    a_s = pltpu.roll(a, SUBLANES - s, 0)
                    b_s = pltpu.roll(b, SUBLANES - s, 0)
                    m = rows < SUBLANES - s
                else:
                    a_s = pltpu.roll(a, s, 0)
                    b_s = pltpu.roll(b, s, 0)
                    m = rows >= s
                b = jnp.where(m, a * b_s + b, b)
                a = jnp.where(m, a * a_s, a)
            h = b + a * carry
            out_ref[0, pl.ds(off, SUBLANES), :] = h
            edge = h[0:1] if reverse else h[SUBLANES - 1:SUBLANES]
            return jnp.broadcast_to(edge, (SUBLANES, dr))

        carry_ref[d] = lax.fori_loop(0, nt, body, carry_ref[d])

    gates(xf_ref, xfp_ref, xfn_ref, j == 0, j == nc - 1, 0)
    scan(0, False, hf_ref)
    gates(xb_ref, xbp_ref, xbn_ref, j == nc - 1, j == 0, 1)
    scan(1, True, hb_ref)


def _rglru(z, conv_w, conv_b, wg, bg, cl):
    b, s, _ = z.shape
    dr = conv_w.shape[1]
    tc = min(SCAN_CHUNK, s)
    nc = s // tc
    nb8 = s // SUBLANES
    cb8 = tc // SUBLANES

    def main(rev):
        return pl.BlockSpec((1, tc, dr), (lambda i, j: (i, nc - 1 - j, 0)) if rev else (lambda i, j: (i, j, 0)))

    def prev(rev):
        def f(i, j):
            c = (nc - 1 - j) if rev else j
            return (i, jnp.maximum(c * cb8 - 1, 0), 0)
        return pl.BlockSpec((1, SUBLANES, dr), f)

    def nxt(rev):
        def f(i, j):
            c = (nc - 1 - j) if rev else j
            return (i, jnp.minimum((c + 1) * cb8, nb8 - 1), 0)
        return pl.BlockSpec((1, SUBLANES, dr), f)

    def const(shape):
        return pl.BlockSpec(shape, lambda i, j: (0,) * len(shape))

    out_sd = jax.ShapeDtypeStruct((b, s, dr), F32)
    return pl.pallas_call(
        _rglru_kernel,
        grid=(b, nc),
        in_specs=[main(False), prev(False), nxt(False), main(True), prev(True), nxt(True),
                  const((RG_CONV, dr)), const((1, dr)), const((2, dr, 2 * dr)), const((2, 1, 2 * dr)),
                  const((2, 1, dr))],
        out_specs=[pl.BlockSpec((1, tc, dr), lambda i, j: (i, j, 0)),
                   pl.BlockSpec((1, tc, dr), lambda i, j: (i, nc - 1 - j, 0))],
        out_shape=[out_sd, out_sd],
        scratch_shapes=[pltpu.VMEM((tc + 2 * SUBLANES, dr), F32), pltpu.VMEM((2, tc, dr), F32),
                        pltpu.VMEM((2, tc, dr), F32), pltpu.VMEM((2, SUBLANES, dr), F32)],
        compiler_params=_cparams(("parallel", "arbitrary")),
        name="rglru",
    )(z, z, z, z, z, z, conv_w, conv_b, wg, bg, cl)


def _mla_proj_kernel(z_ref, qn_ref, kvn_ref, wuq_ref, wk_ref, wv_ref, gq_ref, gk_ref, qb_ref, kb_ref,
                     rc_ref, r1_ref, r2_ref, q_ref, k_ref, v_ref):
    zz = z_ref[0]
    q_lora = qn_ref.shape[1]
    kv_lora = kvn_ref.shape[1]
    ql = zz[:, :q_lora]
    kvl = zz[:, q_lora:q_lora + kv_lora]
    kr = zz[:, q_lora + kv_lora:]
    q = _bdot(_rms(ql, qn_ref[...], q_lora), wuq_ref[...])
    kvn = _rms(kvl, kvn_ref[...], kv_lora)
    kk = _bdot(kvn, wk_ref[...])
    vv = _bdot(kvn, wv_ref[...])
    rc, r1, r2 = rc_ref[...], r1_ref[...], r2_ref[...]
    half = QK_ROPE // 2
    scale = QK_DIM ** -0.5 * LOG2_E

    def norm_rope(xh, g):
        xh = _rms(xh, g, QK_DIM)
        return xh * rc + pltpu.roll(xh, HEAD_SLAB - half, 1) * r1 + pltpu.roll(xh, half, 1) * r2

    for h in range(MLA_HEADS):
        sl = slice(h * HEAD_SLAB, (h + 1) * HEAD_SLAB)
        q_ref[0, h] = (norm_rope(q[:, sl], gq_ref[...]) * scale + qb_ref[...]).astype(BF16)
        k_ref[0, h] = (norm_rope(kk[:, sl] + kr, gk_ref[...]) + kb_ref[...]).astype(BF16)
    for p in range(MLA_HEADS // 2):
        v_ref[0, p] = vv[:, p * LANES:(p + 1) * LANES].astype(BF16)


def _mla_proj(z, col_block, qn, kvn, wuq, wk, wv, gq, gk, qb, kb, rc, r1, r2):
    b, s, _ = z.shape
    tm = min(ROW_TILE, s)
    width = qn.shape[1] + kvn.shape[1] + HEAD_SLAB

    def const(a):
        return pl.BlockSpec(a.shape, lambda i, j: (0,) * a.ndim)

    def rope(a):
        return pl.BlockSpec((tm, HEAD_SLAB), lambda i, j: (j, 0))

    hp = MLA_HEADS // 2
    return pl.pallas_call(
        _mla_proj_kernel,
        grid=(b, s // tm),
        in_specs=[pl.BlockSpec((1, tm, width), lambda i, j: (i, j, col_block)),
                  const(qn), const(kvn), const(wuq), const(wk), const(wv), const(gq), const(gk), const(qb), const(kb),
                  rope(rc), rope(r1), rope(r2)],
        out_specs=[pl.BlockSpec((1, MLA_HEADS, tm, HEAD_SLAB), lambda i, j: (i, 0, j, 0)),
                   pl.BlockSpec((1, MLA_HEADS, tm, HEAD_SLAB), lambda i, j: (i, 0, j, 0)),
                   pl.BlockSpec((1, hp, tm, LANES), lambda i, j: (i, 0, j, 0))],
        out_shape=[jax.ShapeDtypeStruct((b, MLA_HEADS, s, HEAD_SLAB), BF16),
                   jax.ShapeDtypeStruct((b, MLA_HEADS, s, HEAD_SLAB), BF16),
                   jax.ShapeDtypeStruct((b, hp, s, LANES), BF16)],
        compiler_params=_cparams(("parallel", "parallel")),
        name="mla_proj",
    )(z, qn, kvn, wuq, wk, wv, gq, gk, qb, kb, rc, r1, r2)


def _flash_kernel(q_ref, k_ref, v_ref, o_ref, m_ref, l_ref, acc_ref):
    ik = pl.program_id(3)

    @pl.when(ik == 0)
    def _():
        m_ref[...] = jnp.full_like(m_ref, -jnp.inf)
        l_ref[...] = jnp.zeros_like(l_ref)
        acc_ref[...] = jnp.zeros_like(acc_ref)

    v = v_ref[0, 0]
    for hh in range(2):
        s = lax.dot_general(q_ref[0, hh], k_ref[0, hh], (((1,), (1,)), ((), ())), preferred_element_type=F32)
        m_prev = m_ref[hh]
        m_new = jnp.maximum(m_prev, jnp.max(s, axis=-1, keepdims=True))
        alpha = jnp.exp2(m_prev - m_new)
        p = jnp.exp2(s - m_new[:, :1])
        l_ref[hh] = alpha * l_ref[hh] + jnp.sum(p, axis=-1, keepdims=True)
        acc_ref[hh] = alpha * acc_ref[hh] + jnp.dot(p.astype(BF16), v, preferred_element_type=F32)
        m_ref[hh] = m_new

    @pl.when(ik == pl.num_programs(3) - 1)
    def _():
        lane = lax.broadcasted_iota(jnp.int32, acc_ref.shape[1:], 1)
        o_ref[0] = jnp.where(lane < V_DIM, acc_ref[0] / l_ref[0], acc_ref[1] / l_ref[1])


def _flash_bounded_kernel(q_ref, k_ref, v_ref, o_ref, l_ref, acc_ref):
    ik = pl.program_id(3)

    @pl.when(ik == 0)
    def _():
        l_ref[...] = jnp.zeros_like(l_ref)
        acc_ref[...] = jnp.zeros_like(acc_ref)

    v = v_ref[0, 0]
    for hh in range(2):
        s = lax.dot_general(q_ref[0, hh], k_ref[0, hh], (((1,), (1,)), ((), ())), preferred_element_type=F32)
        p = jnp.exp2(s)
        l_ref[hh] = l_ref[hh] + jnp.sum(p, axis=-1, keepdims=True)
        acc_ref[hh] = acc_ref[hh] + jnp.dot(p.astype(BF16), v, preferred_element_type=F32)

    @pl.when(ik == pl.num_programs(3) - 1)
    def _():
        lane = lax.broadcasted_iota(jnp.int32, acc_ref.shape[1:], 1)
        o_ref[0] = jnp.where(lane < V_DIM, acc_ref[0] / l_ref[0], acc_ref[1] / l_ref[1])


def _attention(q, k, v, bounded):
    b, h, s, _ = q.shape
    tq = min(ATTN_TQ, s)
    tk = min(ATTN_TK, s)
    hp = h // 2
    return pl.pallas_call(
        _flash_bounded_kernel if bounded else _flash_kernel,
        grid=(b, hp, s // tq, s // tk),
        in_specs=[pl.BlockSpec((1, 2, tq, HEAD_SLAB), lambda i, p, a, c: (i, p, a, 0)),
                  pl.BlockSpec((1, 2, tk, HEAD_SLAB), lambda i, p, a, c: (i, p, c, 0)),
                  pl.BlockSpec((1, 1, tk, LANES), lambda i, p, a, c: (i, p, c, 0))],
        out_specs=pl.BlockSpec((1, tq, LANES), lambda i, p, a, c: (i, a, p)),
        out_shape=jax.ShapeDtypeStruct((b, s, hp * LANES), F32),
        scratch_shapes=[pltpu.VMEM((2, tq, LANES), F32)] * (2 if bounded else 3),
        compiler_params=_cparams(("parallel", "parallel", "parallel", "arbitrary")),
        name="attention_bounded" if bounded else "attention",
    )(q, k, v)


def _ab_out_kernel(x_ref, hf_ref, hb_ref, y_ref, at_ref, w_ref, g_ref, o_ref):
    rg = (hf_ref[0] + hb_ref[0]) * jax.nn.gelu(y_ref[0])
    cat = jnp.concatenate([rg.astype(BF16), at_ref[0].astype(BF16)], axis=-1)
    m = jnp.dot(cat, w_ref[...], preferred_element_type=F32)
    o_ref[0] = x_ref[0] + g_ref[0] * m


def _ab_out(x, hf, hb, z, attn, w, gate):
    b, s, d = x.shape
    dr = hf.shape[2]
    da = attn.shape[2]
    tm = min(ROW_TILE, s)
    return pl.pallas_call(
        _ab_out_kernel,
        grid=(b, s // tm),
        in_specs=[pl.BlockSpec((1, tm, d), lambda i, j: (i, j, 0)),
                  pl.BlockSpec((1, tm, dr), lambda i, j: (i, j, 0)),
                  pl.BlockSpec((1, tm, dr), lambda i, j: (i, j, 0)),
                  pl.BlockSpec((1, tm, dr), lambda i, j: (i, j, 1)),
                  pl.BlockSpec((1, tm, da), lambda i, j: (i, j, 0)),
                  pl.BlockSpec(w.shape, lambda i, j: (0, 0)),
                  pl.BlockSpec((1, 1, d), lambda i, j: (i, 0, 0))],
        out_specs=pl.BlockSpec((1, tm, d), lambda i, j: (i, j, 0)),
        out_shape=jax.ShapeDtypeStruct((b, s, d), F32),
        compiler_params=_cparams(("parallel", "parallel")),
        name="ab_out",
    )(x, hf, hb, z, attn, w, gate)


def _c_in_kernel(x_ref, g_ref, sc_ref, sh_ref, w_ref, bg_ref, cx_ref):
    h = _norm_mod(x_ref[0], g_ref[...], sc_ref[0], sh_ref[0])
    z = _bdot(h, w_ref[...])
    dc = bg_ref.shape[2]
    bg_ref[0] = z[:, :dc]
    cx_ref[0] = z[:, dc:2 * dc] * z[:, 2 * dc:]


def _c_in(x, gain, sc, sh, w):
    b, s, d = x.shape
    dc = w.shape[1] // 3
    tm = min(ROW_TILE, s)
    sd = jax.ShapeDtypeStruct((b, s, dc), F32)
    return pl.pallas_call(
        _c_in_kernel,
        grid=(b, s // tm),
        in_specs=[pl.BlockSpec((1, tm, d), lambda i, j: (i, j, 0)),
                  pl.BlockSpec((1, d), lambda i, j: (0, 0)),
                  pl.BlockSpec((1, 1, d), lambda i, j: (i, 0, 0)),
                  pl.BlockSpec((1, 1, d), lambda i, j: (i, 0, 0)),
                  pl.BlockSpec(w.shape, lambda i, j: (0, 0))],
        out_specs=[pl.BlockSpec((1, tm, dc), lambda i, j: (i, j, 0))] * 2,
        out_shape=[sd, sd],
        compiler_params=_cparams(("parallel", "parallel")),
        name="c_in",
    )(x, gain, sc, sh, w)


def _c_out_kernel(x_ref, bg_ref, cx_ref, cp_ref, cn_ref, cw_ref, w_ref, g_ref, o_ref, ext_ref):
    j = pl.program_id(1)
    tm = cx_ref.shape[1]
    ext_ref[0:SUBLANES] = jnp.where(j == 0, 0.0, cp_ref[0])
    ext_ref[SUBLANES:SUBLANES + tm] = cx_ref[0]
    ext_ref[SUBLANES + tm:2 * SUBLANES + tm] = jnp.where(j == pl.num_programs(1) - 1, 0.0, cn_ref[0])
    conv = cw_ref[0:1, :] * ext_ref[SUBLANES - 1:SUBLANES - 1 + tm]
    conv = conv + cw_ref[1:2, :] * ext_ref[SUBLANES:SUBLANES + tm]
    conv = conv + cw_ref[2:3, :] * ext_ref[SUBLANES + 1:SUBLANES + 1 + tm]
    m = _bdot(bg_ref[0] * conv, w_ref[...])
    o_ref[0] = x_ref[0] + g_ref[0] * m


def _c_out(x, bg, cx, conv_w, w, gate):
    b, s, d = x.shape
    dc = bg.shape[2]
    tm = min(ROW_TILE, s)
    nb8 = s // SUBLANES
    t8 = tm // SUBLANES
    return pl.pallas_call(
        _c_out_kernel,
        grid=(b, s // tm),
        in_specs=[pl.BlockSpec((1, tm, d), lambda i, j: (i, j, 0)),
                  pl.BlockSpec((1, tm, dc), lambda i, j: (i, j, 0)),
                  pl.BlockSpec((1, tm, dc), lambda i, j: (i, j, 0)),
                  pl.BlockSpec((1, SUBLANES, dc), lambda i, j: (i, jnp.maximum(j * t8 - 1, 0), 0)),
                  pl.BlockSpec((1, SUBLANES, dc), lambda i, j: (i, jnp.minimum((j + 1) * t8, nb8 - 1), 0)),
                  pl.BlockSpec(conv_w.shape, lambda i, j: (0, 0)),
                  pl.BlockSpec(w.shape, lambda i, j: (0, 0)),
                  pl.BlockSpec((1, 1, d), lambda i, j: (i, 0, 0))],
        out_specs=pl.BlockSpec((1, tm, d), lambda i, j: (i, j, 0)),
        out_shape=jax.ShapeDtypeStruct((b, s, d), F32),
        scratch_shapes=[pltpu.VMEM((tm + 2 * SUBLANES, dc), F32)],
        compiler_params=_cparams(("parallel", "parallel")),
        name="c_out",
    )(x, bg, cx, cx, cx, conv_w, w, gate)


def _top_rows(s, k, payload=None):
    n = s.shape[0]
    ridx = lax.broadcasted_iota(jnp.int32, s.shape, 0).astype(F32)
    vals, ids = [], []
    for _ in range(k):
        m = jnp.max(s, axis=0, keepdims=True)
        first = jnp.min(jnp.where(s == m, ridx, float(n)), axis=0, keepdims=True)
        sel = ridx == first
        ids.append(first if payload is None else jnp.sum(jnp.where(sel, payload, 0.0), axis=0, keepdims=True))
        vals.append(m)
        s = jnp.where(sel, -jnp.inf, s)
    return vals, ids


_PAIRS = [(a, b) for a in range(PEER_TOPK) for b in range(PEER_TOPK) if (a + 1) * (b + 1) <= PEER_TOPK]


def _route_kernel(x_ref, g_ref, sc_ref, sh_ref, wq_ref, k1_ref, k2_ref, h_ref, idx_ref, gate_ref, q_scr, sel_scr):
    h = _norm_mod(x_ref[0], g_ref[...], sc_ref[0], sh_ref[0])
    h_ref[0] = h
    q_scr[...] = _bdot(h, wq_ref[...])
    n_keys = k1_ref.shape[0]
    half = k1_ref.shape[1]
    tm = x_ref.shape[1]
    nt = (((1,), (1,)), ((), ()))
    pad = -len(_PAIRS) % SUBLANES

    def head(hd, carry):
        off = pl.multiple_of(hd * 2 * half, 2 * half)
        q1 = q_scr[:, pl.ds(off, half)].astype(BF16)
        q2 = q_scr[:, pl.ds(off + half, half)].astype(BF16)
        s1 = lax.dot_general(k1_ref[...], q1, nt, preferred_element_type=F32)
        s2 = lax.dot_general(k2_ref[...], q2, nt, preferred_element_type=F32)
        v1, i1 = _top_rows(s1, PEER_TOPK)
        v2, i2 = _top_rows(s2, PEER_TOPK)
        cv = [v1[a] + v2[b] for a, b in _PAIRS] + [jnp.full((pad, tm), -jnp.inf, F32)]
        ce = [i1[a] * float(n_keys) + i2[b] for a, b in _PAIRS] + [jnp.zeros((pad, tm), F32)]
        vs, es = _top_rows(jnp.concatenate(cv, axis=0), PEER_TOPK, jnp.concatenate(ce, axis=0))
        vs = jnp.concatenate(vs, axis=0)
        e = jnp.exp(vs - vs[0:1])
        row = pl.multiple_of(hd * PEER_TOPK, PEER_TOPK)
        sel_scr[0, pl.ds(row, PEER_TOPK), :] = e / jnp.sum(e, axis=0, keepdims=True)
        sel_scr[1, pl.ds(row, PEER_TOPK), :] = jnp.concatenate(es, axis=0)
        return carry

    lax.fori_loop(0, PEER_HEADS, head, 0)
    gate_ref[0] = sel_scr[0].T
    idx_ref[0] = sel_scr[1].T.astype(jnp.int32)


def _route(x, gain, sc, sh, wq, k1, k2):
    b, s, d = x.shape
    tm = min(ROUTE_TILE, s)
    nsel = PEER_HEADS * PEER_TOPK
    return pl.pallas_call(
        _route_kernel,
        grid=(b, s // tm),
        in_specs=[pl.BlockSpec((1, tm, d), lambda i, j: (i, j, 0)),
                  pl.BlockSpec((1, d), lambda i, j: (0, 0)),
                  pl.BlockSpec((1, 1, d), lambda i, j: (i, 0, 0)),
                  pl.BlockSpec((1, 1, d), lambda i, j: (i, 0, 0)),
                  pl.BlockSpec(wq.shape, lambda i, j: (0, 0)),
                  pl.BlockSpec(k1.shape, lambda i, j: (0, 0)),
                  pl.BlockSpec(k2.shape, lambda i, j: (0, 0))],
        out_specs=[pl.BlockSpec((1, tm, d), lambda i, j: (i, j, 0)),
                   pl.BlockSpec((1, tm, nsel), lambda i, j: (i, j, 0)),
                   pl.BlockSpec((1, tm, nsel), lambda i, j: (i, j, 0))],
        out_shape=[jax.ShapeDtypeStruct((b, s, d), F32),
                   jax.ShapeDtypeStruct((b, s, nsel), jnp.int32),
                   jax.ShapeDtypeStruct((b, s, nsel), F32)],
        scratch_shapes=[pltpu.VMEM((tm, wq.shape[1]), F32), pltpu.VMEM((2, nsel, tm), F32)],
        compiler_params=_cparams(("parallel", "parallel")),
        name="peer_route",
    )(x, gain, sc, sh, wq, k1, k2)


def _pack_table(t):
    n, d = t.shape
    chunks = d // LANES
    bits = lax.bitcast_convert_type(t.astype(BF16), jnp.uint16).astype(jnp.uint32)
    bits = bits.reshape(n * chunks // 2, 2, LANES)
    return lax.bitcast_convert_type(bits[:, 0, :] | (bits[:, 1, :] << 16), jnp.int32)


def _gather_rows(idx_ref, t, tbl_ref, wr):
    rows = [tbl_ref[pl.ds(pl.multiple_of(idx_ref[t, r], wr), wr), :] for r in range(idx_ref.shape[1])]
    return pltpu.bitcast(jnp.concatenate(rows, axis=0), BF16)


def _idx_copy(idx_vmem, half, smem, sem):
    sub = smem.shape[0]
    return pltpu.make_async_copy(idx_vmem.at[pl.ds(half * sub, sub)], smem, sem)


def _staged_halves(idx_cur, idx_next, smem_a, smem_b, sems, process):
    i = pl.program_id(0)
    sub = smem_a.shape[0]

    @pl.when(i == 0)
    def _():
        _idx_copy(idx_cur, 0, smem_a, sems.at[0]).start()

    _idx_copy(idx_cur, 0, smem_a, sems.at[0]).wait()
    _idx_copy(idx_cur, 1, smem_b, sems.at[1]).start()
    process(smem_a, 0)
    _idx_copy(idx_cur, 1, smem_b, sems.at[1]).wait()

    @pl.when(i + 1 < pl.num_programs(0))
    def _():
        _idx_copy(idx_next, 0, smem_a, sems.at[0]).start()

    process(smem_b, sub)


def _split_bf16(a):
    hi = a.astype(BF16).astype(F32)
    return jnp.concatenate([hi, a - hi], axis=0).astype(BF16)


def _chunk_mask(chunks, width):
    lane = lax.broadcasted_iota(jnp.int32, (chunks, width), 1)
    return lane % chunks == lax.broadcasted_iota(jnp.int32, (chunks, width), 0)


def _peer_blocks(t):
    sub = min(PEER_TB, t // 2)
    assert t % (2 * sub) == 0
    return sub, 2 * sub


def _idx_scratch(sub, nsel):
    return [pltpu.SMEM((sub, nsel), jnp.int32), pltpu.SMEM((sub, nsel), jnp.int32), pltpu.SemaphoreType.DMA((2,))]


def _pack_pairs(t):
    d = t.shape[1]
    bits = lax.bitcast_convert_type(t.astype(BF16), jnp.uint16).astype(jnp.uint32)
    return lax.bitcast_convert_type(bits[:, :d // 2] | (bits[:, d // 2:] << 16), jnp.int32)


def _sc_scores(tbl, idx, x):
    t, nsel = idx.shape
    w = tbl.shape[1]
    info = plsc.get_sparse_core_info()
    lanes = info.num_lanes
    workers = info.num_cores * info.num_subcores
    per = t // workers
    half = nsel // 2
    assert t % (workers * SC_TOKENS) == 0 and half % lanes == 0 and w % lanes == 0
    mesh = plsc.VectorSubcoreMesh(core_axis_name="c", subcore_axis_name="s")

    @functools.partial(
        pl.kernel, mesh=mesh, out_type=jax.ShapeDtypeStruct((t, nsel), F32),
        scratch_types=[pltpu.VMEM((SC_TOKENS, nsel), jnp.int32), pltpu.VMEM((SC_TOKENS, 2 * w), F32),
                       pltpu.VMEM((2, half, w), jnp.int32), pltpu.VMEM((SC_TOKENS, nsel), F32),
                       pltpu.SemaphoreType.DMA((2,))],
        compiler_params=pltpu.CompilerParams(needs_layout_passes=False),
        name="peer_scores")
    def scores(tbl_hbm, idx_hbm, x_hbm, out_hbm, idx_v, x_v, rows_v, s_v, sems):
        wid = lax.axis_index("s") * info.num_cores + lax.axis_index("c")
        lane = lax.iota(jnp.int32, lanes)

        def gather(tt, hb):
            return pltpu.make_async_copy(tbl_hbm.at[idx_v.at[tt, pl.ds(hb * half, half)]], rows_v.at[hb], sems.at[hb])

        def compute(tt, hb):
            @pl.loop(0, half // lanes)
            def _(rg):
                def chunk(c, accs):
                    xl = x_v[tt, pl.ds(c * lanes, lanes)]
                    xh = x_v[tt, pl.ds(w + c * lanes, lanes)]
                    out = []
                    for r in range(lanes):
                        wv = rows_v[hb, rg * lanes + r, pl.ds(c * lanes, lanes)]
                        lo = lax.bitcast_convert_type(wv << 16, F32)
                        hi = lax.bitcast_convert_type(wv & jnp.int32(-65536), F32)
                        out.append(accs[r] + lo * xl + hi * xh)
                    return tuple(out)

                accs = lax.fori_loop(0, w // lanes, chunk, tuple(jnp.zeros((lanes,), F32) for _ in range(lanes)))
                res = jnp.zeros((lanes,), F32)
                for r in range(lanes):
                    res = jnp.where(lane == r, jnp.sum(accs[r]), res)
                s_v[tt, pl.ds(hb * half + rg * lanes, lanes)] = res

        @pl.loop(0, per // SC_TOKENS)
        def _(bi):
            t0 = wid * per + bi * SC_TOKENS
            pltpu.sync_copy(idx_hbm.at[pl.ds(t0, SC_TOKENS)], idx_v)
            pltpu.sync_copy(x_hbm.at[pl.ds(t0, SC_TOKENS)], x_v)
            gather(0, 0).start()

            @pl.loop(0, SC_TOKENS)
            def _(tt):
                gather(tt, 1).start()
                gather(tt, 0).wait()
                compute(tt, 0)

                @pl.when(tt + 1 < SC_TOKENS)
                def _():
                    gather(tt + 1, 0).start()

                gather(tt, 1).wait()
                compute(tt, 1)

            pltpu.sync_copy(s_v, out_hbm.at[pl.ds(t0, SC_TOKENS)])

    return scores(tbl, idx, x)


def _peer_v_kernel(idx_cur, idx_next, s_ref, gts_ref, spread_ref, tbl_ref, x_ref, g_ref, o_ref, smem_a, smem_b, sems, o_scr,
                   act_ref):
    tb = x_ref.shape[0]
    chunks = x_ref.shape[1] // LANES
    width = act_ref.shape[1]
    mask = _chunk_mask(chunks, width)
    a = _split_bf16(jax.nn.gelu(s_ref[...]) * gts_ref[...])
    a = jnp.dot(a, spread_ref[...], preferred_element_type=F32)
    act_ref[...] = a[:tb] + a[tb:]

    def process(idx_ref, first):
        for t in range(idx_ref.shape[0]):
            gb = _gather_rows(idx_ref, t, tbl_ref, chunks // 2)
            a = jnp.where(mask, jnp.broadcast_to(act_ref[first + t:first + t + 1, :], (chunks, width)), 0.0)
            o = jnp.dot(_split_bf16(a), gb, preferred_element_type=F32)
            o_scr[(first + t) * chunks:(first + t + 1) * chunks, :] = o[:chunks] + o[chunks:]

    _staged_halves(idx_cur, idx_next, smem_a, smem_b, sems, process)
    for c in range(chunks):
        sl = slice(c * LANES, (c + 1) * LANES)
        o_ref[:, sl] = x_ref[:, sl] + g_ref[0][:, sl] * o_scr[pl.ds(c, tb, stride=chunks), :]


def _peer_v(idx, scores, gts, table, x, gate, tokens_per_batch):
    t, nsel = idx.shape
    d = x.shape[1]
    chunks = d // LANES
    width = nsel * chunks
    sub, tb = _peer_blocks(t)
    assert tokens_per_batch % tb == 0
    spread = jnp.kron(jnp.eye(nsel, dtype=F32), jnp.ones((1, chunks), F32)).astype(BF16)
    steps = t // tb
    return pl.pallas_call(
        _peer_v_kernel,
        grid=(steps,),
        in_specs=[pl.BlockSpec((tb, nsel), lambda i: (i, 0)),
                  pl.BlockSpec((tb, nsel), lambda i: (jnp.minimum(i + 1, steps - 1), 0)),
                  pl.BlockSpec((tb, nsel), lambda i: (i, 0)),
                  pl.BlockSpec((tb, nsel), lambda i: (i, 0)),
                  pl.BlockSpec(spread.shape, lambda i: (0, 0), pipeline_mode=pl.Buffered(1)),
                  pl.BlockSpec(table.shape, lambda i: (0, 0), pipeline_mode=pl.Buffered(1)),
                  pl.BlockSpec((tb, d), lambda i: (i, 0)),
                  pl.BlockSpec((1, 1, d), lambda i: (i * tb // tokens_per_batch, 0, 0))],
        out_specs=pl.BlockSpec((tb, d), lambda i: (i, 0)),
        out_shape=jax.ShapeDtypeStruct(x.shape, F32),
        scratch_shapes=_idx_scratch(sub, nsel) + [pltpu.VMEM((tb * chunks, LANES), F32), pltpu.VMEM((tb, width), F32)],
        compiler_params=_cparams(("arbitrary",)),
        name="peer_v",
    )(idx, idx, scores, gts, spread, table, x, gate)


def _peer_front(x, gain, sc, sh, wq, k1, k2, u_tbl):
    b, s, d = x.shape
    h, idx, gts = _route(x, gain, sc, sh, wq, k1, k2)
    nsel = idx.shape[2]
    idx = idx.reshape(b * s, nsel)
    return idx, gts.reshape(b * s, nsel), _sc_scores(u_tbl, idx, h.reshape(b * s, d))


def _peer_back(x, gate, idx, gts, scores, v_tbl):
    b, s, d = x.shape
    out = _peer_v(idx * (d // LANES // 2), scores, gts, v_tbl, x.reshape(b * s, d), gate, s)
    return out.reshape(b, s, d)


def _block_diag(w):
    h, i, j = w.shape
    return jnp.einsum('hij,hg->higj', w, jnp.eye(h, dtype=w.dtype)).reshape(h * i, h * j)


def _rope_tables(s):
    half = QK_ROPE // 2
    inv = 1.0 / (ROPE_THETA ** (jnp.arange(0, QK_ROPE, 2, dtype=F32) / QK_ROPE))
    ang = jnp.arange(s, dtype=F32)[:, None] * inv[None, :]
    cos, sin = jnp.cos(ang), jnp.sin(ang)
    z = jnp.zeros((s, QK_NOPE), F32)
    tail = jnp.zeros((s, HEAD_SLAB - QK_DIM), F32)
    zh = jnp.zeros((s, half), F32)
    rc = jnp.concatenate([z + 1.0, cos, cos, tail + 1.0], axis=1)
    r1 = jnp.concatenate([z, -sin, zh, tail], axis=1)
    r2 = jnp.concatenate([z, zh, sin, tail], axis=1)
    return rc, r1, r2


def _pad_last(a, n):
    return jnp.pad(a, [(0, 0)] * (a.ndim - 1) + [(0, n - a.shape[-1])])


def _mixer_ab(x, n1, sc1, sh1, g1, p, j):
    b, s, d = x.shape
    d_rnn = p['rg_conv_w'].shape[2]
    q_lora = p['mla_q_norm'].shape[1]
    kv_lora = p['mla_kv_norm'].shape[1]
    w_in = p['ab_w_in'][j]
    lat = 2 * d_rnn + q_lora + kv_lora
    w0 = jnp.concatenate([w_in[:, :lat], jnp.zeros((d, QK_NOPE), F32), w_in[:, lat:],
                          jnp.zeros((d, HEAD_SLAB - QK_DIM), F32)], axis=1).astype(BF16)
    z = _in_proj(x, n1, sc1, sh1, w0)
    wg = jnp.stack([jnp.concatenate([_block_diag(p['rg_wa'][j][k]), _block_diag(p['rg_wx'][j][k])], axis=1)
                    for k in range(2)]).astype(BF16)
    bg = jnp.concatenate([p['rg_ba'][j], p['rg_bx'][j]], axis=1)[:, None, :]
    cl = (-RG_C * jax.nn.softplus(-p['rg_lambda'][j]))[:, None, :]
    hf, hb = _rglru(z, p['rg_conv_w'][j], p['rg_conv_b'][j][None, :], wg, bg, cl)
    wuq = _pad_last(p['mla_w_uq'][j].reshape(q_lora, MLA_HEADS, QK_DIM), HEAD_SLAB)
    wuq = wuq.reshape(q_lora, MLA_HEADS * HEAD_SLAB).astype(BF16)
    wkv = p['mla_w_ukv'][j].reshape(kv_lora, MLA_HEADS, QK_NOPE + V_DIM)
    wk = _pad_last(wkv[:, :, :QK_NOPE], HEAD_SLAB).reshape(kv_lora, MLA_HEADS * HEAD_SLAB).astype(BF16)
    wv = wkv[:, :, QK_NOPE:].reshape(kv_lora, MLA_HEADS * V_DIM).astype(BF16)
    gq = _pad_last(p['mla_qn_q'][j][None, :], HEAD_SLAB)
    gk = _pad_last(p['mla_qn_k'][j][None, :], HEAD_SLAB)
    rc, r1, r2 = _rope_tables(s)
    width = q_lora + kv_lora + HEAD_SLAB
    assert (2 * d_rnn) % width == 0
    bound = 1.02 * QK_DIM ** 0.5 * jnp.max(jnp.abs(gq)) * jnp.max(jnp.abs(gk))
    spare = jnp.arange(HEAD_SLAB)[None, :] == HEAD_SLAB - 1
    qb = jnp.where(spare, 1.0, 0.0).astype(F32)
    kb = jnp.where(spare, -bound * LOG2_E, 0.0).astype(F32)
    q, k, v = _mla_proj(z, 2 * d_rnn // width, p['mla_q_norm'][j][None, :], p['mla_kv_norm'][j][None, :],
                        wuq, wk, wv, gq, gk, qb, kb, rc, r1, r2)
    attn = lax.cond(bound < MAX_SOFTMAX_BOUND, functools.partial(_attention, bounded=True),
                    functools.partial(_attention, bounded=False), q, k, v)
    return _ab_out(x, hf, hb, z, attn, p['ab_w_out'][j].astype(BF16), g1)


def _mixer_c(x, n1, sc1, sh1, g1, p, j):
    bgate, cx = _c_in(x, n1, sc1, sh1, p['c_w_in'][j].astype(BF16))
    return _c_out(x, bgate, cx, p['c_conv_w'][j], p['c_w_out'][j].astype(BF16), g1)


def _layer_front(x, mod_i, p, i):
    sh1, sc1, g1, sh2, sc2, g2 = [m[:, None, :] for m in jnp.split(mod_i, 6, axis=-1)]
    mixer = _mixer_ab if i % 2 == 0 else _mixer_c
    x = mixer(x, p['norm1_g'][i][None, :], sc1, sh1, g1, p, i // 2)
    front = _peer_front(x, p['norm2_g'][i][None, :], sc2, sh2, p['peer_wq'][i].astype(BF16),
                        p['peer_k1'][i].astype(BF16), p['peer_k2'][i].astype(BF16), _pack_pairs(p['peer_u'][i]))
    return (x, g2) + front


def _layer_back(state, p, i):
    x, g2, idx, gts, scores = state
    return _peer_back(x, g2, idx, gts, scores, _pack_table(p['peer_v'][i]))


def kernel(x_prompt, x_sample, c_prompt, c_sample, ada_w, ada_b, norm1_g, norm2_g, ab_w_in, rg_conv_w, rg_conv_b, rg_wa, rg_ba, rg_wx, rg_bx, rg_lambda, mla_q_norm, mla_w_uq, mla_kv_norm, mla_w_ukv, mla_qn_q, mla_qn_k, ab_w_out, c_w_in, c_conv_w, c_w_out, peer_wq, peer_k1, peer_k2, peer_u, peer_v):
    p = dict(ada_w=ada_w, norm1_g=norm1_g, norm2_g=norm2_g, ab_w_in=ab_w_in, rg_conv_w=rg_conv_w,
             rg_conv_b=rg_conv_b, rg_wa=rg_wa, rg_ba=rg_ba, rg_wx=rg_wx, rg_bx=rg_bx, rg_lambda=rg_lambda,
             mla_q_norm=mla_q_norm, mla_w_uq=mla_w_uq, mla_kv_norm=mla_kv_norm, mla_w_ukv=mla_w_ukv,
             mla_qn_q=mla_qn_q, mla_qn_k=mla_qn_k, ab_w_out=ab_w_out, c_w_in=c_w_in, c_conv_w=c_conv_w,
             c_w_out=c_w_out, peer_wq=peer_wq, peer_k1=peer_k1, peer_k2=peer_k2, peer_u=peer_u, peer_v=peer_v)
    bp, bs = c_prompt.shape[0], c_sample.shape[0]
    rows = -(-(bp + bs) // SUBLANES) * SUBLANES
    c_all = jnp.pad(jnp.concatenate([c_prompt, c_sample], axis=0), ((0, rows - bp - bs), (0, 0)))
    mod = _modulation(c_all, ada_w, ada_b)
    xs = [x_sample, x_prompt]
    mods = [mod[:, bp:bp + bs], mod[:, :bp]]
    for i in range(ada_w.shape[0]):
        fronts = [_layer_front(x, m[i], p, i) for x, m in zip(xs, mods)]
        xs = [_layer_back(f, p, i) for f in fronts]
    return (xs[1], xs[0])
```

```python
import functools

import jax
import jax.numpy as jnp
from jax import lax
from jax.experimental import pallas as pl
from jax.experimental.pallas import tpu as pltpu
from jax.experimental.pallas import tpu_sc as plsc

F32 = jnp.float32
BF16 = jnp.bfloat16
EPS = 1e-6
LOG2_E = 1.4426950408889634

RG_HEADS = 8
RG_CONV = 4
RG_C = 8.0
MLA_HEADS = 8
QK_NOPE = 64
QK_ROPE = 32
V_DIM = 64
QK_DIM = QK_NOPE + QK_ROPE
ROPE_THETA = 10000.0
PEER_HEADS = 8
PEER_TOPK = 16
HEAD_SLAB = 128
MAX_SOFTMAX_BOUND = 40.0

LANES = 128
SUBLANES = 8
VMEM_LIMIT = 56 * 1024 * 1024

ROW_TILE = 512
SCAN_CHUNK = 1024
ATTN_TQ = 1024
ATTN_TK = 1024
ROUTE_TILE = 256
PEER_TB = 32
SC_TOKENS = 8


def _cparams(sem):
    return pltpu.CompilerParams(dimension_semantics=sem, vmem_limit_bytes=VMEM_LIMIT)


def _norm_mod(x, gain, sc, sh):
    ms = jnp.mean(x * x, axis=-1, keepdims=True)
    return x * lax.rsqrt(ms + EPS) * gain * (1.0 + sc) + sh


def _rms(x, gain, n):
    ms = jnp.sum(x * x, axis=-1, keepdims=True) * (1.0 / n)
    return x * lax.rsqrt(ms + EPS) * gain


def _bdot(a, b):
    return jnp.dot(a.astype(BF16), b, preferred_element_type=F32)


def _mod_kernel(c_ref, w_ref, b_ref, o_ref):
    c = c_ref[...]
    s = c * jax.nn.sigmoid(c)
    o_ref[0] = _bdot(s, w_ref[0].astype(BF16)) + b_ref[0]


def _modulation(c_all, ada_w, ada_b):
    depth, d, n = ada_w.shape
    rows = c_all.shape[0]
    tn = 1536
    return pl.pallas_call(
        _mod_kernel,
        grid=(depth, n // tn),
        in_specs=[
            pl.BlockSpec((rows, d), lambda i, j: (0, 0)),
            pl.BlockSpec((1, d, tn), lambda i, j: (i, 0, j)),
            pl.BlockSpec((1, 1, tn), lambda i, j: (i, 0, j)),
        ],
        out_specs=pl.BlockSpec((1, rows, tn), lambda i, j: (i, 0, j)),
        out_shape=jax.ShapeDtypeStruct((depth, rows, n), F32),
        compiler_params=_cparams(("parallel", "parallel")),
        name="modulation",
    )(c_all, ada_w, ada_b.reshape(depth, 1, n))


def _in_proj_kernel(x_ref, g_ref, sc_ref, sh_ref, w_ref, z_ref):
    h = _norm_mod(x_ref[0], g_ref[...], sc_ref[0], sh_ref[0])
    z_ref[0] = _bdot(h, w_ref[...])


def _in_proj(x, gain, sc, sh, w):
    b, s, d = x.shape
    n = w.shape[1]
    tm = min(ROW_TILE, s)
    return pl.pallas_call(
        _in_proj_kernel,
        grid=(b, s // tm),
        in_specs=[
            pl.BlockSpec((1, tm, d), lambda i, j: (i, j, 0)),
            pl.BlockSpec((1, d), lambda i, j: (0, 0)),
            pl.BlockSpec((1, 1, d), lambda i, j: (i, 0, 0)),
            pl.BlockSpec((1, 1, d), lambda i, j: (i, 0, 0)),
            pl.BlockSpec((d, n), lambda i, j: (0, 0)),
        ],
        out_specs=pl.BlockSpec((1, tm, n), lambda i, j: (i, j, 0)),
        out_shape=jax.ShapeDtypeStruct((b, s, n), F32),
        compiler_params=_cparams(("parallel", "parallel")),
        name="in_proj",
    )(x, gain, sc, sh, w)


def _rglru_kernel(xf_ref, xfp_ref, xfn_ref, xb_ref, xbp_ref, xbn_ref, cw_ref, cb_ref, wg_ref, bg_ref, cl_ref,
                  hf_ref, hb_ref, ext_ref, a_ref, b_ref, carry_ref):
    j = pl.program_id(1)
    nc = pl.num_programs(1)
    tc = xf_ref.shape[1]
    dr = xf_ref.shape[2]
    nt = tc // SUBLANES

    @pl.when(j == 0)
    def _():
        carry_ref[...] = jnp.zeros_like(carry_ref)

    def gates(x_ref, xp_ref, xn_ref, first, last, d):
        ext_ref[0:SUBLANES] = jnp.where(first, 0.0, xp_ref[0])
        ext_ref[SUBLANES:SUBLANES + tc] = x_ref[0]
        ext_ref[SUBLANES + tc:2 * SUBLANES + tc] = jnp.where(last, 0.0, xn_ref[0])
        xc = cb_ref[...]
        for k in range(RG_CONV):
            xc = xc + cw_ref[k:k + 1, :] * ext_ref[SUBLANES - 2 + k:SUBLANES - 2 + k + tc]
        g = _bdot(xc, wg_ref[d]) + bg_ref[d]
        r = jax.nn.sigmoid(g[:, :dr])
        i = jax.nn.sigmoid(g[:, dr:])
        log_a = r * cl_ref[d]
        a = jnp.exp(log_a)
        b = jnp.sqrt(-jnp.tanh(log_a) * (1.0 + a * a)) * (i * xc)
        a_ref[d] = a
        b_ref[d] = b

    rows = lax.broadcasted_iota(jnp.int32, (SUBLANES, dr), 0)

    def scan(d, reverse, out_ref):
        def body(it, carry):
            t = (nt - 1 - it) if reverse else it
            off = pl.multiple_of(t * SUBLANES, SUBLANES)
            a = a_ref[d, pl.ds(off, SUBLANES), :]
            b = b_ref[d, pl.ds(off, SUBLANES), :]
            for s in (1, 2, 4):
                if reverse:
                    a_s = pltpu.roll(a, SUBLANES - s, 0)
                    b_s = pltpu.roll(b, SUBLANES - s, 0)
                    m = rows < SUBLANES - s
                else:
                    a_s = pltpu.roll(a, s, 0)
                    b_s = pltpu.roll(b, s, 0)
                    m = rows >= s
                b = jnp.where(m, a * b_s + b, b)
                a = jnp.where(m, a * a_s, a)
            h = b + a * carry
            out_ref[0, pl.ds(off, SUBLANES), :] = h
            edge = h[0:1] if reverse else h[SUBLANES - 1:SUBLANES]
            return jnp.broadcast_to(edge, (SUBLANES, dr))

        carry_ref[d] = lax.fori_loop(0, nt, body, carry_ref[d])

    gates(xf_ref, xfp_ref, xfn_ref, j == 0, j == nc - 1, 0)
    scan(0, False, hf_ref)
    gates(xb_ref, xbp_ref, xbn_ref, j == nc - 1, j == 0, 1)
    scan(1, True, hb_ref)


def _rglru(z, conv_w, conv_b, wg, bg, cl):
    b, s, _ = z.shape
    dr = conv_w.shape[1]
    tc = min(SCAN_CHUNK, s)
    nc = s // tc
    nb8 = s // SUBLANES
    cb8 = tc // SUBLANES

    def main(rev):
        return pl.BlockSpec((1, tc, dr), (lambda i, j: (i, nc - 1 - j, 0)) if rev else (lambda i, j: (i, j, 0)))

    def prev(rev):
        def f(i, j):
            c = (nc - 1 - j) if rev else j
            return (i, jnp.maximum(c * cb8 - 1, 0), 0)
        return pl.BlockSpec((1, SUBLANES, dr), f)

    def nxt(rev):
        def f(i, j):
            c = (nc - 1 - j) if rev else j
            return (i, jnp.minimum((c + 1) * cb8, nb8 - 1), 0)
        return pl.BlockSpec((1, SUBLANES, dr), f)

    def const(shape):
        return pl.BlockSpec(shape, lambda i, j: (0,) * len(shape))

    out_sd = jax.ShapeDtypeStruct((b, s, dr), F32)
    return pl.pallas_call(
        _rglru_kernel,
        grid=(b, nc),
        in_specs=[main(False), prev(False), nxt(False), main(True), prev(True), nxt(True),
                  const((RG_CONV, dr)), const((1, dr)), const((2, dr, 2 * dr)), const((2, 1, 2 * dr)),
                  const((2, 1, dr))],
        out_specs=[pl.BlockSpec((1, tc, dr), lambda i, j: (i, j, 0)),
                   pl.BlockSpec((1, tc, dr), lambda i, j: (i, nc - 1 - j, 0))],
        out_shape=[out_sd, out_sd],
        scratch_shapes=[pltpu.VMEM((tc + 2 * SUBLANES, dr), F32), pltpu.VMEM((2, tc, dr), F32),
                        pltpu.VMEM((2, tc, dr), F32), pltpu.VMEM((2, SUBLANES, dr), F32)],
        compiler_params=_cparams(("parallel", "arbitrary")),
        name="rglru",
    )(z, z, z, z, z, z, conv_w, conv_b, wg, bg, cl)


def _mla_proj_kernel(z_ref, qn_ref, kvn_ref, wuq_ref, wk_ref, wv_ref, gq_ref, gk_ref, qb_ref, kb_ref,
                     rc_ref, r1_ref, r2_ref, q_ref, k_ref, v_ref):
    zz = z_ref[0]
    q_lora = qn_ref.shape[1]
    kv_lora = kvn_ref.shape[1]
    ql = zz[:, :q_lora]
    kvl = zz[:, q_lora:q_lora + kv_lora]
    kr = zz[:, q_lora + kv_lora:]
    q = _bdot(_rms(ql, qn_ref[...], q_lora), wuq_ref[...])
    kvn = _rms(kvl, kvn_ref[...], kv_lora)
    kk = _bdot(kvn, wk_ref[...])
    vv = _bdot(kvn, wv_ref[...])
    rc, r1, r2 = rc_ref[...], r1_ref[...], r2_ref[...]
    half = QK_ROPE // 2
    scale = QK_DIM ** -0.5 * LOG2_E

    def norm_rope(xh, g):
        xh = _rms(xh, g, QK_DIM)
        return xh * rc + pltpu.roll(xh, HEAD_SLAB - half, 1) * r1 + pltpu.roll(xh, half, 1) * r2

    for h in range(MLA_HEADS):
        sl = slice(h * HEAD_SLAB, (h + 1) * HEAD_SLAB)
        q_ref[0, h] = (norm_rope(q[:, sl], gq_ref[...]) * scale + qb_ref[...]).astype(BF16)
        k_ref[0, h] = (norm_rope(kk[:, sl] + kr, gk_ref[...]) + kb_ref[...]).astype(BF16)
    for p in range(MLA_HEADS // 2):
        v_ref[0, p] = vv[:, p * LANES:(p + 1) * LANES].astype(BF16)


def _mla_proj(z, col_block, qn, kvn, wuq, wk, wv, gq, gk, qb, kb, rc, r1, r2):
    b, s, _ = z.shape
    tm = min(ROW_TILE, s)
    width = qn.shape[1] + kvn.shape[1] + HEAD_SLAB

    def const(a):
        return pl.BlockSpec(a.shape, lambda i, j: (0,) * a.ndim)

    def rope(a):
        return pl.BlockSpec((tm, HEAD_SLAB), lambda i, j: (j, 0))

    hp = MLA_HEADS // 2
    return pl.pallas_call(
        _mla_proj_kernel,
        grid=(b, s // tm),
        in_specs=[pl.BlockSpec((1, tm, width), lambda i, j: (i, j, col_block)),
                  const(qn), const(kvn), const(wuq), const(wk), const(wv), const(gq), const(gk), const(qb), const(kb),
                  rope(rc), rope(r1), rope(r2)],
        out_specs=[pl.BlockSpec((1, MLA_HEADS, tm, HEAD_SLAB), lambda i, j: (i, 0, j, 0)),
                   pl.BlockSpec((1, MLA_HEADS, tm, HEAD_SLAB), lambda i, j: (i, 0, j, 0)),
                   pl.BlockSpec((1, hp, tm, LANES), lambda i, j: (i, 0, j, 0))],
        out_shape=[jax.ShapeDtypeStruct((b, MLA_HEADS, s, HEAD_SLAB), BF16),
                   jax.ShapeDtypeStruct((b, MLA_HEADS, s, HEAD_SLAB), BF16),
                   jax.ShapeDtypeStruct((b, hp, s, LANES), BF16)],
        compiler_params=_cparams(("parallel", "parallel")),
        name="mla_proj",
    )(z, qn, kvn, wuq, wk, wv, gq, gk, qb, kb, rc, r1, r2)


def _flash_kernel(q_ref, k_ref, v_ref, o_ref, m_ref, l_ref, acc_ref):
    ik = pl.program_id(3)

    @pl.when(ik == 0)
    def _():
        m_ref[...] = jnp.full_like(m_ref, -jnp.inf)
        l_ref[...] = jnp.zeros_like(l_ref)
        acc_ref[...] = jnp.zeros_like(acc_ref)

    v = v_ref[0, 0]
    for hh in range(2):
        s = lax.dot_general(q_ref[0, hh], k_ref[0, hh], (((1,), (1,)), ((), ())), preferred_element_type=F32)
        m_prev = m_ref[hh]
        m_new = jnp.maximum(m_prev, jnp.max(s, axis=-1, keepdims=True))
        alpha = jnp.exp2(m_prev - m_new)
        p = jnp.exp2(s - m_new[:, :1])
        l_ref[hh] = alpha * l_ref[hh] + jnp.sum(p, axis=-1, keepdims=True)
        acc_ref[hh] = alpha * acc_ref[hh] + jnp.dot(p.astype(BF16), v, preferred_element_type=F32)
        m_ref[hh] = m_new

    @pl.when(ik == pl.num_programs(3) - 1)
    def _():
        lane = lax.broadcasted_iota(jnp.int32, acc_ref.shape[1:], 1)
        o_ref[0] = jnp.where(lane < V_DIM, acc_ref[0] / l_ref[0], acc_ref[1] / l_ref[1])


def _flash_bounded_kernel(q_ref, k_ref, v_ref, o_ref, l_ref, acc_ref):
    ik = pl.program_id(3)

    @pl.when(ik == 0)
    def _():
        l_ref[...] = jnp.zeros_like(l_ref)
        acc_ref[...] = jnp.zeros_like(acc_ref)

    v = v_ref[0, 0]
    for hh in range(2):
        s = lax.dot_general(q_ref[0, hh], k_ref[0, hh], (((1,), (1,)), ((), ())), preferred_element_type=F32)
        p = jnp.exp2(s)
        l_ref[hh] = l_ref[hh] + jnp.sum(p, axis=-1, keepdims=True)
        acc_ref[hh] = acc_ref[hh] + jnp.dot(p.astype(BF16), v, preferred_element_type=F32)

    @pl.when(ik == pl.num_programs(3) - 1)
    def _():
        lane = lax.broadcasted_iota(jnp.int32, acc_ref.shape[1:], 1)
        o_ref[0] = jnp.where(lane < V_DIM, acc_ref[0] / l_ref[0], acc_ref[1] / l_ref[1])


def _attention(q, k, v, bounded):
    b, h, s, _ = q.shape
    tq = min(ATTN_TQ, s)
    tk = min(ATTN_TK, s)
    hp = h // 2
    return pl.pallas_call(
        _flash_bounded_kernel if bounded else _flash_kernel,
        grid=(b, hp, s // tq, s // tk),
        in_specs=[pl.BlockSpec((1, 2, tq, HEAD_SLAB), lambda i, p, a, c: (i, p, a, 0)),
                  pl.BlockSpec((1, 2, tk, HEAD_SLAB), lambda i, p, a, c: (i, p, c, 0)),
                  pl.BlockSpec((1, 1, tk, LANES), lambda i, p, a, c: (i, p, c, 0))],
        out_specs=pl.BlockSpec((1, tq, LANES), lambda i, p, a, c: (i, a, p)),
        out_shape=jax.ShapeDtypeStruct((b, s, hp * LANES), F32),
        scratch_shapes=[pltpu.VMEM((2, tq, LANES), F32)] * (2 if bounded else 3),
        compiler_params=_cparams(("parallel", "parallel", "parallel", "arbitrary")),
        name="attention_bounded" if bounded else "attention",
    )(q, k, v)


def _ab_out_kernel(x_ref, hf_ref, hb_ref, y_ref, at_ref, w_ref, g_ref, o_ref):
    rg = (hf_ref[0] + hb_ref[0]) * jax.nn.gelu(y_ref[0])
    cat = jnp.concatenate([rg.astype(BF16), at_ref[0].astype(BF16)], axis=-1)
    m = jnp.dot(cat, w_ref[...], preferred_element_type=F32)
    o_ref[0] = x_ref[0] + g_ref[0] * m


def _ab_out(x, hf, hb, z, attn, w, gate):
    b, s, d = x.shape
    dr = hf.shape[2]
    da = attn.shape[2]
    tm = min(ROW_TILE, s)
    return pl.pallas_call(
        _ab_out_kernel,
        grid=(b, s // tm),
        in_specs=[pl.BlockSpec((1, tm, d), lambda i, j: (i, j, 0)),
                  pl.BlockSpec((1, tm, dr), lambda i, j: (i, j, 0)),
                  pl.BlockSpec((1, tm, dr), lambda i, j: (i, j, 0)),
                  pl.BlockSpec((1, tm, dr), lambda i, j: (i, j, 1)),
                  pl.BlockSpec((1, tm, da), lambda i, j: (i, j, 0)),
                  pl.BlockSpec(w.shape, lambda i, j: (0, 0)),
                  pl.BlockSpec((1, 1, d), lambda i, j: (i, 0, 0))],
        out_specs=pl.BlockSpec((1, tm, d), lambda i, j: (i, j, 0)),
        out_shape=jax.ShapeDtypeStruct((b, s, d), F32),
        compiler_params=_cparams(("parallel", "parallel")),
        name="ab_out",
    )(x, hf, hb, z, attn, w, gate)


def _c_in_kernel(x_ref, g_ref, sc_ref, sh_ref, w_ref, bg_ref, cx_ref):
    h = _norm_mod(x_ref[0], g_ref[...], sc_ref[0], sh_ref[0])
    z = _bdot(h, w_ref[...])
    dc = bg_ref.shape[2]
    bg_ref[0] = z[:, :dc]
    cx_ref[0] = z[:, dc:2 * dc] * z[:, 2 * dc:]


def _c_in(x, gain, sc, sh, w):
    b, s, d = x.shape
    dc = w.shape[1] // 3
    tm = min(ROW_TILE, s)
    sd = jax.ShapeDtypeStruct((b, s, dc), F32)
    return pl.pallas_call(
        _c_in_kernel,
        grid=(b, s // tm),
        in_specs=[pl.BlockSpec((1, tm, d), lambda i, j: (i, j, 0)),
                  pl.BlockSpec((1, d), lambda i, j: (0, 0)),
                  pl.BlockSpec((1, 1, d), lambda i, j: (i, 0, 0)),
                  pl.BlockSpec((1, 1, d), lambda i, j: (i, 0, 0)),
                  pl.BlockSpec(w.shape, lambda i, j: (0, 0))],
        out_specs=[pl.BlockSpec((1, tm, dc), lambda i, j: (i, j, 0))] * 2,
        out_shape=[sd, sd],
        compiler_params=_cparams(("parallel", "parallel")),
        name="c_in",
    )(x, gain, sc, sh, w)


def _c_out_kernel(x_ref, bg_ref, cx_ref, cp_ref, cn_ref, cw_ref, w_ref, g_ref, o_ref, ext_ref):
    j = pl.program_id(1)
    tm = cx_ref.shape[1]
    ext_ref[0:SUBLANES] = jnp.where(j == 0, 0.0, cp_ref[0])
    ext_ref[SUBLANES:SUBLANES + tm] = cx_ref[0]
    ext_ref[SUBLANES + tm:2 * SUBLANES + tm] = jnp.where(j == pl.num_programs(1) - 1, 0.0, cn_ref[0])
    conv = cw_ref[0:1, :] * ext_ref[SUBLANES - 1:SUBLANES - 1 + tm]
    conv = conv + cw_ref[1:2, :] * ext_ref[SUBLANES:SUBLANES + tm]
    conv = conv + cw_ref[2:3, :] * ext_ref[SUBLANES + 1:SUBLANES + 1 + tm]
    m = _bdot(bg_ref[0] * conv, w_ref[...])
    o_ref[0] = x_ref[0] + g_ref[0] * m


def _c_out(x, bg, cx, conv_w, w, gate):
    b, s, d = x.shape
    dc = bg.shape[2]
    tm = min(ROW_TILE, s)
    nb8 = s // SUBLANES
    t8 = tm // SUBLANES
    return pl.pallas_call(
        _c_out_kernel,
        grid=(b, s // tm),
        in_specs=[pl.BlockSpec((1, tm, d), lambda i, j: (i, j, 0)),
                  pl.BlockSpec((1, tm, dc), lambda i, j: (i, j, 0)),
                  pl.BlockSpec((1, tm, dc), lambda i, j: (i, j, 0)),
                  pl.BlockSpec((1, SUBLANES, dc), lambda i, j: (i, jnp.maximum(j * t8 - 1, 0), 0)),
                  pl.BlockSpec((1, SUBLANES, dc), lambda i, j: (i, jnp.minimum((j + 1) * t8, nb8 - 1), 0)),
                  pl.BlockSpec(conv_w.shape, lambda i, j: (0, 0)),
                  pl.BlockSpec(w.shape, lambda i, j: (0, 0)),
                  pl.BlockSpec((1, 1, d), lambda i, j: (i, 0, 0))],
        out_specs=pl.BlockSpec((1, tm, d), lambda i, j: (i, j, 0)),
        out_shape=jax.ShapeDtypeStruct((b, s, d), F32),
        scratch_shapes=[pltpu.VMEM((tm + 2 * SUBLANES, dc), F32)],
        compiler_params=_cparams(("parallel", "parallel")),
        name="c_out",
    )(x, bg, cx, cx, cx, conv_w, w, gate)


def _top_rows(s, k, payload=None):
    n = s.shape[0]
    ridx = lax.broadcasted_iota(jnp.int32, s.shape, 0).astype(F32)
    vals, ids = [], []
    for _ in range(k):
        m = jnp.max(s, axis=0, keepdims=True)
        first = jnp.min(jnp.where(s == m, ridx, float(n)), axis=0, keepdims=True)
        sel = ridx == first
        ids.append(first if payload is None else jnp.sum(jnp.where(sel, payload, 0.0), axis=0, keepdims=True))
        vals.append(m)
        s = jnp.where(sel, -jnp.inf, s)
    return vals, ids


_PAIRS = [(a, b) for a in range(PEER_TOPK) for b in range(PEER_TOPK) if (a + 1) * (b + 1) <= PEER_TOPK]


def _route_kernel(x_ref, g_ref, sc_ref, sh_ref, wq_ref, k1_ref, k2_ref, h_ref, idx_ref, gate_ref, q_scr, sel_scr):
    h = _norm_mod(x_ref[0], g_ref[...], sc_ref[0], sh_ref[0])
    h_ref[0] = h
    q_scr[...] = _bdot(h, wq_ref[...])
    n_keys = k1_ref.shape[0]
    half = k1_ref.shape[1]
    tm = x_ref.shape[1]
    nt = (((1,), (1,)), ((), ()))
    pad = -len(_PAIRS) % SUBLANES

    def head(hd, carry):
        off = pl.multiple_of(hd * 2 * half, 2 * half)
        q1 = q_scr[:, pl.ds(off, half)].astype(BF16)
        q2 = q_scr[:, pl.ds(off + half, half)].astype(BF16)
        s1 = lax.dot_general(k1_ref[...], q1, nt, preferred_element_type=F32)
        s2 = lax.dot_general(k2_ref[...], q2, nt, preferred_element_type=F32)
        v1, i1 = _top_rows(s1, PEER_TOPK)
        v2, i2 = _top_rows(s2, PEER_TOPK)
        cv = [v1[a] + v2[b] for a, b in _PAIRS] + [jnp.full((pad, tm), -jnp.inf, F32)]
        ce = [i1[a] * float(n_keys) + i2[b] for a, b in _PAIRS] + [jnp.zeros((pad, tm), F32)]
        vs, es = _top_rows(jnp.concatenate(cv, axis=0), PEER_TOPK, jnp.concatenate(ce, axis=0))
        vs = jnp.concatenate(vs, axis=0)
        e = jnp.exp(vs - vs[0:1])
        row = pl.multiple_of(hd * PEER_TOPK, PEER_TOPK)
        sel_scr[0, pl.ds(row, PEER_TOPK), :] = e / jnp.sum(e, axis=0, keepdims=True)
        sel_scr[1, pl.ds(row, PEER_TOPK), :] = jnp.concatenate(es, axis=0)
        return carry

    lax.fori_loop(0, PEER_HEADS, head, 0)
    gate_ref[0] = sel_scr[0].T
    idx_ref[0] = sel_scr[1].T.astype(jnp.int32)


def _route(x, gain, sc, sh, wq, k1, k2):
    b, s, d = x.shape
    tm = min(ROUTE_TILE, s)
    nsel = PEER_HEADS * PEER_TOPK
    return pl.pallas_call(
        _route_kernel,
        grid=(b, s // tm),
        in_specs=[pl.BlockSpec((1, tm, d), lambda i, j: (i, j, 0)),
                  pl.BlockSpec((1, d), lambda i, j: (0, 0)),
                  pl.BlockSpec((1, 1, d), lambda i, j: (i, 0, 0)),
                  pl.BlockSpec((1, 1, d), lambda i, j: (i, 0, 0)),
                  pl.BlockSpec(wq.shape, lambda i, j: (0, 0)),
                  pl.BlockSpec(k1.shape, lambda i, j: (0, 0)),
                  pl.BlockSpec(k2.shape, lambda i, j: (0, 0))],
        out_specs=[pl.BlockSpec((1, tm, d), lambda i, j: (i, j, 0)),
                   pl.BlockSpec((1, tm, nsel), lambda i, j: (i, j, 0)),
                   pl.BlockSpec((1, tm, nsel), lambda i, j: (i, j, 0))],
        out_shape=[jax.ShapeDtypeStruct((b, s, d), F32),
                   jax.ShapeDtypeStruct((b, s, nsel), jnp.int32),
                   jax.ShapeDtypeStruct((b, s, nsel), F32)],
        scratch_shapes=[pltpu.VMEM((tm, wq.shape[1]), F32), pltpu.VMEM((2, nsel, tm), F32)],
        compiler_params=_cparams(("parallel", "parallel")),
        name="peer_route",
    )(x, gain, sc, sh, wq, k1, k2)


def _pack_table(t):
    n, d = t.shape
    chunks = d // LANES
    bits = lax.bitcast_convert_type(t.astype(BF16), jnp.uint16).astype(jnp.uint32)
    bits = bits.reshape(n * chunks // 2, 2, LANES)
    return lax.bitcast_convert_type(bits[:, 0, :] | (bits[:, 1, :] << 16), jnp.int32)


def _gather_rows(idx_ref, t, tbl_ref, wr):
    rows = [tbl_ref[pl.ds(pl.multiple_of(idx_ref[t, r], wr), wr), :] for r in range(idx_ref.shape[1])]
    return pltpu.bitcast(jnp.concatenate(rows, axis=0), BF16)


def _idx_copy(idx_vmem, half, smem, sem):
    sub = smem.shape[0]
    return pltpu.make_async_copy(idx_vmem.at[pl.ds(half * sub, sub)], smem, sem)


def _staged_halves(idx_cur, idx_next, smem_a, smem_b, sems, process):
    i = pl.program_id(0)
    sub = smem_a.shape[0]

    @pl.when(i == 0)
    def _():
        _idx_copy(idx_cur, 0, smem_a, sems.at[0]).start()

    _idx_copy(idx_cur, 0, smem_a, sems.at[0]).wait()
    _idx_copy(idx_cur, 1, smem_b, sems.at[1]).start()
    process(smem_a, 0)
    _idx_copy(idx_cur, 1, smem_b, sems.at[1]).wait()

    @pl.when(i + 1 < pl.num_programs(0))
    def _():
        _idx_copy(idx_next, 0, smem_a, sems.at[0]).start()

    process(smem_b, sub)


def _split_bf16(a):
    hi = a.astype(BF16).astype(F32)
    return jnp.concatenate([hi, a - hi], axis=0).astype(BF16)


def _chunk_mask(chunks, width):
    lane = lax.broadcasted_iota(jnp.int32, (chunks, width), 1)
    return lane % chunks == lax.broadcasted_iota(jnp.int32, (chunks, width), 0)


def _peer_blocks(t):
    sub = min(PEER_TB, t // 2)
    assert t % (2 * sub) == 0
    return sub, 2 * sub


def _idx_scratch(sub, nsel):
    return [pltpu.SMEM((sub, nsel), jnp.int32), pltpu.SMEM((sub, nsel), jnp.int32), pltpu.SemaphoreType.DMA((2,))]


def _pack_pairs(t):
    d = t.shape[1]
    bits = lax.bitcast_convert_type(t.astype(BF16), jnp.uint16).astype(jnp.uint32)
    return lax.bitcast_convert_type(bits[:, :d // 2] | (bits[:, d // 2:] << 16), jnp.int32)


def _sc_scores(tbl, idx, x):
    t, nsel = idx.shape
    w = tbl.shape[1]
    info = plsc.get_sparse_core_info()
    lanes = info.num_lanes
    workers = info.num_cores * info.num_subcores
    per = t // workers
    half = nsel // 2
    assert t % (workers * SC_TOKENS) == 0 and half % lanes == 0 and w % lanes == 0
    mesh = plsc.VectorSubcoreMesh(core_axis_name="c", subcore_axis_name="s")

    @functools.partial(
        pl.kernel, mesh=mesh, out_type=jax.ShapeDtypeStruct((t, nsel), F32),
        scratch_types=[pltpu.VMEM((SC_TOKENS, nsel), jnp.int32), pltpu.VMEM((SC_TOKENS, 2 * w), F32),
                       pltpu.VMEM((2, half, w), jnp.int32), pltpu.VMEM((SC_TOKENS, nsel), F32),
                       pltpu.SemaphoreType.DMA((2,))],
        compiler_params=pltpu.CompilerParams(needs_layout_passes=False),
        name="peer_scores")
    def scores(tbl_hbm, idx_hbm, x_hbm, out_hbm, idx_v, x_v, rows_v, s_v, sems):
        wid = lax.axis_index("s") * info.num_cores + lax.axis_index("c")
        lane = lax.iota(jnp.int32, lanes)

        def gather(tt, hb):
            return pltpu.make_async_copy(tbl_hbm.at[idx_v.at[tt, pl.ds(hb * half, half)]], rows_v.at[hb], sems.at[hb])

        def compute(tt, hb):
            @pl.loop(0, half // lanes)
            def _(rg):
                def chunk(c, accs):
                    xl = x_v[tt, pl.ds(c * lanes, lanes)]
                    xh = x_v[tt, pl.ds(w + c * lanes, lanes)]
                    out = []
                    for r in range(lanes):
                        wv = rows_v[hb, rg * lanes + r, pl.ds(c * lanes, lanes)]
                        lo = lax.bitcast_convert_type(wv << 16, F32)
                        hi = lax.bitcast_convert_type(wv & jnp.int32(-65536), F32)
                        out.append(accs[r] + lo * xl + hi * xh)
                    return tuple(out)

                accs = lax.fori_loop(0, w // lanes, chunk, tuple(jnp.zeros((lanes,), F32) for _ in range(lanes)))
                res = jnp.zeros((lanes,), F32)
                for r in range(lanes):
                    res = jnp.where(lane == r, jnp.sum(accs[r]), res)
                s_v[tt, pl.ds(hb * half + rg * lanes, lanes)] = res

        @pl.loop(0, per // SC_TOKENS)
        def _(bi):
            t0 = wid * per + bi * SC_TOKENS
            pltpu.sync_copy(idx_hbm.at[pl.ds(t0, SC_TOKENS)], idx_v)
            pltpu.sync_copy(x_hbm.at[pl.ds(t0, SC_TOKENS)], x_v)
            gather(0, 0).start()

            @pl.loop(0, SC_TOKENS)
            def _(tt):
                gather(tt, 1).start()
                gather(tt, 0).wait()
                compute(tt, 0)

                @pl.when(tt + 1 < SC_TOKENS)
                def _():
                    gather(tt + 1, 0).start()

                gather(tt, 1).wait()
                compute(tt, 1)

            pltpu.sync_copy(s_v, out_hbm.at[pl.ds(t0, SC_TOKENS)])

    return scores(tbl, idx, x)


def _peer_v_kernel(idx_cur, idx_next, s_ref, gts_ref, spread_ref, tbl_ref, x_ref, g_ref, o_ref, smem_a, smem_b, sems, o_scr,
                   act_ref):
    tb = x_ref.shape[0]
    chunks = x_ref.shape[1] // LANES
    width = act_ref.shape[1]
    mask = _chunk_mask(chunks, width)
    a = _split_bf16(jax.nn.gelu(s_ref[...]) * gts_ref[...])
    a = jnp.dot(a, spread_ref[...], preferred_element_type=F32)
    act_ref[...] = a[:tb] + a[tb:]

    def process(idx_ref, first):
        for t in range(idx_ref.shape[0]):
            gb = _gather_rows(idx_ref, t, tbl_ref, chunks // 2)
            a = jnp.where(mask, jnp.broadcast_to(act_ref[first + t:first + t + 1, :], (chunks, width)), 0.0)
            o = jnp.dot(_split_bf16(a), gb, preferred_element_type=F32)
            o_scr[(first + t) * chunks:(first + t + 1) * chunks, :] = o[:chunks] + o[chunks:]

    _staged_halves(idx_cur, idx_next, smem_a, smem_b, sems, process)
    for c in range(chunks):
        sl = slice(c * LANES, (c + 1) * LANES)
        o_ref[:, sl] = x_ref[:, sl] + g_ref[0][:, sl] * o_scr[pl.ds(c, tb, stride=chunks), :]


def _peer_v(idx, scores, gts, table, x, gate, tokens_per_batch):
    t, nsel = idx.shape
    d = x.shape[1]
    chunks = d // LANES
    width = nsel * chunks
    sub, tb = _peer_blocks(t)
    assert tokens_per_batch % tb == 0
    spread = jnp.kron(jnp.eye(nsel, dtype=F32), jnp.ones((1, chunks), F32)).astype(BF16)
    steps = t // tb
    return pl.pallas_call(
        _peer_v_kernel,
        grid=(steps,),
        in_specs=[pl.BlockSpec((tb, nsel), lambda i: (i, 0)),
                  pl.BlockSpec((tb, nsel), lambda i: (jnp.minimum(i + 1, steps - 1), 0)),
                  pl.BlockSpec((tb, nsel), lambda i: (i, 0)),
                  pl.BlockSpec((tb, nsel), lambda i: (i, 0)),
                  pl.BlockSpec(spread.shape, lambda i: (0, 0), pipeline_mode=pl.Buffered(1)),
                  pl.BlockSpec(table.shape, lambda i: (0, 0), pipeline_mode=pl.Buffered(1)),
                  pl.BlockSpec((tb, d), lambda i: (i, 0)),
                  pl.BlockSpec((1, 1, d), lambda i: (i * tb // tokens_per_batch, 0, 0))],
        out_specs=pl.BlockSpec((tb, d), lambda i: (i, 0)),
        out_shape=jax.ShapeDtypeStruct(x.shape, F32),
        scratch_shapes=_idx_scratch(sub, nsel) + [pltpu.VMEM((tb * chunks, LANES), F32), pltpu.VMEM((tb, width), F32)],
        compiler_params=_cparams(("arbitrary",)),
        name="peer_v",
    )(idx, idx, scores, gts, spread, table, x, gate)


def _peer_front(x, gain, sc, sh, wq, k1, k2, u_tbl):
    b, s, d = x.shape
    h, idx, gts = _route(x, gain, sc, sh, wq, k1, k2)
    nsel = idx.shape[2]
    idx = idx.reshape(b * s, nsel)
    return idx, gts.reshape(b * s, nsel), _sc_scores(u_tbl, idx, h.reshape(b * s, d))


def _peer_back(x, gate, idx, gts, scores, v_tbl):
    b, s, d = x.shape
    out = _peer_v(idx * (d // LANES // 2), scores, gts, v_tbl, x.reshape(b * s, d), gate, s)
    return out.reshape(b, s, d)


def _block_diag(w):
    h, i, j = w.shape
    return jnp.einsum('hij,hg->higj', w, jnp.eye(h, dtype=w.dtype)).reshape(h * i, h * j)


def _rope_tables(s):
    half = QK_ROPE // 2
    inv = 1.0 / (ROPE_THETA ** (jnp.arange(0, QK_ROPE, 2, dtype=F32) / QK_ROPE))
    ang = jnp.arange(s, dtype=F32)[:, None] * inv[None, :]
    cos, sin = jnp.cos(ang), jnp.sin(ang)
    z = jnp.zeros((s, QK_NOPE), F32)
    tail = jnp.zeros((s, HEAD_SLAB - QK_DIM), F32)
    zh = jnp.zeros((s, half), F32)
    rc = jnp.concatenate([z + 1.0, cos, cos, tail + 1.0], axis=1)
    r1 = jnp.concatenate([z, -sin, zh, tail], axis=1)
    r2 = jnp.concatenate([z, zh, sin, tail], axis=1)
    return rc, r1, r2


def _pad_last(a, n):
    return jnp.pad(a, [(0, 0)] * (a.ndim - 1) + [(0, n - a.shape[-1])])


def _mixer_ab(x, n1, sc1, sh1, g1, p, j):
    b, s, d = x.shape
    d_rnn = p['rg_conv_w'].shape[2]
    q_lora = p['mla_q_norm'].shape[1]
    kv_lora = p['mla_kv_norm'].shape[1]
    w_in = p['ab_w_in'][j]
    lat = 2 * d_rnn + q_lora + kv_lora
    w0 = jnp.concatenate([w_in[:, :lat], jnp.zeros((d, QK_NOPE), F32), w_in[:, lat:],
                          jnp.zeros((d, HEAD_SLAB - QK_DIM), F32)], axis=1).astype(BF16)
    z = _in_proj(x, n1, sc1, sh1, w0)
    wg = jnp.stack([jnp.concatenate([_block_diag(p['rg_wa'][j][k]), _block_diag(p['rg_wx'][j][k])], axis=1)
                    for k in range(2)]).astype(BF16)
    bg = jnp.concatenate([p['rg_ba'][j], p['rg_bx'][j]], axis=1)[:, None, :]
    cl = (-RG_C * jax.nn.softplus(-p['rg_lambda'][j]))[:, None, :]
    hf, hb = _rglru(z, p['rg_conv_w'][j], p['rg_conv_b'][j][None, :], wg, bg, cl)
    wuq = _pad_last(p['mla_w_uq'][j].reshape(q_lora, MLA_HEADS, QK_DIM), HEAD_SLAB)
    wuq = wuq.reshape(q_lora, MLA_HEADS * HEAD_SLAB).astype(BF16)
    wkv = p['mla_w_ukv'][j].reshape(kv_lora, MLA_HEADS, QK_NOPE + V_DIM)
    wk = _pad_last(wkv[:, :, :QK_NOPE], HEAD_SLAB).reshape(kv_lora, MLA_HEADS * HEAD_SLAB).astype(BF16)
    wv = wkv[:, :, QK_NOPE:].reshape(kv_lora, MLA_HEADS * V_DIM).astype(BF16)
    gq = _pad_last(p['mla_qn_q'][j][None, :], HEAD_SLAB)
    gk = _pad_last(p['mla_qn_k'][j][None, :], HEAD_SLAB)
    rc, r1, r2 = _rope_tables(s)
    width = q_lora + kv_lora + HEAD_SLAB
    assert (2 * d_rnn) % width == 0
    bound = 1.02 * QK_DIM ** 0.5 * jnp.max(jnp.abs(gq)) * jnp.max(jnp.abs(gk))
    spare = jnp.arange(HEAD_SLAB)[None, :] == HEAD_SLAB - 1
    qb = jnp.where(spare, 1.0, 0.0).astype(F32)
    kb = jnp.where(spare, -bound * LOG2_E, 0.0).astype(F32)
    q, k, v = _mla_proj(z, 2 * d_rnn // width, p['mla_q_norm'][j][None, :], p['mla_kv_norm'][j][None, :],
                        wuq, wk, wv, gq, gk, qb, kb, rc, r1, r2)
    attn = lax.cond(bound < MAX_SOFTMAX_BOUND, functools.partial(_attention, bounded=True),
                    functools.partial(_attention, bounded=False), q, k, v)
    return _ab_out(x, hf, hb, z, attn, p['ab_w_out'][j].astype(BF16), g1)


def _mixer_c(x, n1, sc1, sh1, g1, p, j):
    bgate, cx = _c_in(x, n1, sc1, sh1, p['c_w_in'][j].astype(BF16))
    return _c_out(x, bgate, cx, p['c_conv_w'][j], p['c_w_out'][j].astype(BF16), g1)


def _layer_front(x, mod_i, p, i):
    sh1, sc1, g1, sh2, sc2, g2 = [m[:, None, :] for m in jnp.split(mod_i, 6, axis=-1)]
    mixer = _mixer_ab if i % 2 == 0 else _mixer_c
    x = mixer(x, p['norm1_g'][i][None, :], sc1, sh1, g1, p, i // 2)
    front = _peer_front(x, p['norm2_g'][i][None, :], sc2, sh2, p['peer_wq'][i].astype(BF16),
                        p['peer_k1'][i].astype(BF16), p['peer_k2'][i].astype(BF16), _pack_pairs(p['peer_u'][i]))
    return (x, g2) + front


def _layer_back(state, p, i):
    x, g2, idx, gts, scores = state
    return _peer_back(x, g2, idx, gts, scores, _pack_table(p['peer_v'][i]))


def kernel(x_prompt, x_sample, c_prompt, c_sample, ada_w, ada_b, norm1_g, norm2_g, ab_w_in, rg_conv_w, rg_conv_b, rg_wa, rg_ba, rg_wx, rg_bx, rg_lambda, mla_q_norm, mla_w_uq, mla_kv_norm, mla_w_ukv, mla_qn_q, mla_qn_k, ab_w_out, c_w_in, c_conv_w, c_w_out, peer_wq, peer_k1, peer_k2, peer_u, peer_v):
    p = dict(ada_w=ada_w, norm1_g=norm1_g, norm2_g=norm2_g, ab_w_in=ab_w_in, rg_conv_w=rg_conv_w,
             rg_conv_b=rg_conv_b, rg_wa=rg_wa, rg_ba=rg_ba, rg_wx=rg_wx, rg_bx=rg_bx, rg_lambda=rg_lambda,
             mla_q_norm=mla_q_norm, mla_w_uq=mla_w_uq, mla_kv_norm=mla_kv_norm, mla_w_ukv=mla_w_ukv,
             mla_qn_q=mla_qn_q, mla_qn_k=mla_qn_k, ab_w_out=ab_w_out, c_w_in=c_w_in, c_conv_w=c_conv_w,
             c_w_out=c_w_out, peer_wq=peer_wq, peer_k1=peer_k1, peer_k2=peer_k2, peer_u=peer_u, peer_v=peer_v)
    bp, bs = c_prompt.shape[0], c_sample.shape[0]
    rows = -(-(bp + bs) // SUBLANES) * SUBLANES
    c_all = jnp.pad(jnp.concatenate([c_prompt, c_sample], axis=0), ((0, rows - bp - bs), (0, 0)))
    mod = _modulation(c_all, ada_w, ada_b)
    xs = [x_sample] + [x_prompt[b:b + 1] for b in range(bp)]
    mods = [mod[:, bp:bp + bs]] + [mod[:, b:b + 1] for b in range(bp)]
    for i in range(ada_w.shape[0]):
        fronts = [_layer_front(x, m[i], p, i) for x, m in zip(xs, mods)]
        xs = [_layer_back(f, p, i) for f in fronts]
    return (jnp.concatenate(xs[1:], axis=0), xs[0])
```

```python
import functools

import jax
import jax.numpy as jnp
from jax import lax
from jax.experimental import pallas as pl
from jax.experimental.pallas import tpu as pltpu
from jax.experimental.pallas import tpu_sc as plsc

F32 = jnp.float32
BF16 = jnp.bfloat16
EPS = 1e-6
LOG2_E = 1.4426950408889634

RG_HEADS = 8
RG_CONV = 4
RG_C = 8.0
MLA_HEADS = 8
QK_NOPE = 64
QK_ROPE = 32
V_DIM = 64
QK_DIM = QK_NOPE + QK_ROPE
ROPE_THETA = 10000.0
PEER_HEADS = 8
PEER_TOPK = 16
HEAD_SLAB = 128
MAX_SOFTMAX_BOUND = 40.0

LANES = 128
SUBLANES = 8
VMEM_LIMIT = 56 * 1024 * 1024

ROW_TILE = 512
SCAN_CHUNK = 1024
ATTN_TQ = 1024
ATTN_TK = 1024
ROUTE_TILE = 256
PEER_TB = 32
SC_TOKENS = 8
PIECE_TOKENS = 16384


def _cparams(sem):
    return pltpu.CompilerParams(dimension_semantics=sem, vmem_limit_bytes=VMEM_LIMIT)


def _norm_mod(x, gain, sc, sh):
    ms = jnp.mean(x * x, axis=-1, keepdims=True)
    return x * lax.rsqrt(ms + EPS) * gain * (1.0 + sc) + sh


def _rms(x, gain, n):
    ms = jnp.sum(x * x, axis=-1, keepdims=True) * (1.0 / n)
    return x * lax.rsqrt(ms + EPS) * gain


def _bdot(a, b):
    return jnp.dot(a.astype(BF16), b, preferred_element_type=F32)


def _mod_kernel(c_ref, w_ref, b_ref, o_ref):
    c = c_ref[...]
    s = c * jax.nn.sigmoid(c)
    o_ref[0] = _bdot(s, w_ref[0].astype(BF16)) + b_ref[0]


def _modulation(c_all, ada_w, ada_b):
    depth, d, n = ada_w.shape
    rows = c_all.shape[0]
    tn = 1536
    return pl.pallas_call(
        _mod_kernel,
        grid=(depth, n // tn),
        in_specs=[
            pl.BlockSpec((rows, d), lambda i, j: (0, 0)),
            pl.BlockSpec((1, d, tn), lambda i, j: (i, 0, j)),
            pl.BlockSpec((1, 1, tn), lambda i, j: (i, 0, j)),
        ],
        out_specs=pl.BlockSpec((1, rows, tn), lambda i, j: (i, 0, j)),
        out_shape=jax.ShapeDtypeStruct((depth, rows, n), F32),
        compiler_params=_cparams(("parallel", "parallel")),
        name="modulation",
    )(c_all, ada_w, ada_b.reshape(depth, 1, n))


def _in_proj_kernel(x_ref, g_ref, sc_ref, sh_ref, w_ref, z_ref):
    h = _norm_mod(x_ref[0], g_ref[...], sc_ref[0], sh_ref[0])
    z_ref[0] = _bdot(h, w_ref[...])


def _in_proj(x, gain, sc, sh, w):
    b, s, d = x.shape
    n = w.shape[1]
    tm = min(ROW_TILE, s)
    return pl.pallas_call(
        _in_proj_kernel,
        grid=(b, s // tm),
        in_specs=[
            pl.BlockSpec((1, tm, d), lambda i, j: (i, j, 0)),
            pl.BlockSpec((1, d), lambda i, j: (0, 0)),
            pl.BlockSpec((1, 1, d), lambda i, j: (i, 0, 0)),
            pl.BlockSpec((1, 1, d), lambda i, j: (i, 0, 0)),
            pl.BlockSpec((d, n), lambda i, j: (0, 0)),
        ],
        out_specs=pl.BlockSpec((1, tm, n), lambda i, j: (i, j, 0)),
        out_shape=jax.ShapeDtypeStruct((b, s, n), F32),
        compiler_params=_cparams(("parallel", "parallel")),
        name="in_proj",
    )(x, gain, sc, sh, w)


def _rglru_kernel(xf_ref, xfp_ref, xfn_ref, xb_ref, xbp_ref, xbn_ref, cw_ref, cb_ref, wg_ref, bg_ref, cl_ref,
                  hf_ref, hb_ref, ext_ref, a_ref, b_ref, carry_ref):
    j = pl.program_id(1)
    nc = pl.num_programs(1)
    tc = xf_ref.shape[1]
    dr = xf_ref.shape[2]
    nt = tc // SUBLANES

    @pl.when(j == 0)
    def _():
        carry_ref[...] = jnp.zeros_like(carry_ref)

    def gates(x_ref, xp_ref, xn_ref, first, last, d):
        ext_ref[0:SUBLANES] = jnp.where(first, 0.0, xp_ref[0])
        ext_ref[SUBLANES:SUBLANES + tc] = x_ref[0]
        ext_ref[SUBLANES + tc:2 * SUBLANES + tc] = jnp.where(last, 0.0, xn_ref[0])
        xc = cb_ref[...]
        for k in range(RG_CONV):
            xc = xc + cw_ref[k:k + 1, :] * ext_ref[SUBLANES - 2 + k:SUBLANES - 2 + k + tc]
        g = _bdot(xc, wg_ref[d]) + bg_ref[d]
        r = jax.nn.sigmoid(g[:, :dr])
        i = jax.nn.sigmoid(g[:, dr:])
        log_a = r * cl_ref[d]
        a = jnp.exp(log_a)
        b = jnp.sqrt(-jnp.tanh(log_a) * (1.0 + a * a)) * (i * xc)
        a_ref[d] = a
        b_ref[d] = b

    rows = lax.broadcasted_iota(jnp.int32, (SUBLANES, dr), 0)

    def scan(d, reverse, out_ref):
        def body(it, carry):
            t = (nt - 1 - it) if reverse else it
            off = pl.multiple_of(t * SUBLANES, SUBLANES)
            a = a_ref[d, pl.ds(off, SUBLANES), :]
            b = b_ref[d, pl.ds(off, SUBLANES), :]
            for s in (1, 2, 4):
                if reverse:
                    a_s = pltpu.roll(a, SUBLANES - s, 0)
                    b_s = pltpu.roll(b, SUBLANES - s, 0)
                    m = rows < SUBLANES - s
                else:
                    a_s = pltpu.roll(a, s, 0)
                    b_s = pltpu.roll(b, s, 0)
                    m = rows >= s
                b = jnp.where(m, a * b_s + b, b)
                a = jnp.where(m, a * a_s, a)
            h = b + a * carry
            out_ref[0, pl.ds(off, SUBLANES), :] = h
            edge = h[0:1] if reverse else h[SUBLANES - 1:SUBLANES]
            return jnp.broadcast_to(edge, (SUBLANES, dr))

        carry_ref[d] = lax.fori_loop(0, nt, body, carry_ref[d])

    gates(xf_ref, xfp_ref, xfn_ref, j == 0, j == nc - 1, 0)
    scan(0, False, hf_ref)
    gates(xb_ref, xbp_ref, xbn_ref, j == nc - 1, j == 0, 1)
    scan(1, True, hb_ref)


def _rglru(z, conv_w, conv_b, wg, bg, cl):
    b, s, _ = z.shape
    dr = conv_w.shape[1]
    tc = min(SCAN_CHUNK, s)
    nc = s // tc
    nb8 = s // SUBLANES
    cb8 = tc // SUBLANES

    def main(rev):
        return pl.BlockSpec((1, tc, dr), (lambda i, j: (i, nc - 1 - j, 0)) if rev else (lambda i, j: (i, j, 0)))

    def prev(rev):
        def f(i, j):
            c = (nc - 1 - j) if rev else j
            return (i, jnp.maximum(c * cb8 - 1, 0), 0)
        return pl.BlockSpec((1, SUBLANES, dr), f)

    def nxt(rev):
        def f(i, j):
            c = (nc - 1 - j) if rev else j
            return (i, jnp.minimum((c + 1) * cb8, nb8 - 1), 0)
        return pl.BlockSpec((1, SUBLANES, dr), f)

    def const(shape):
        return pl.BlockSpec(shape, lambda i, j: (0,) * len(shape))

    out_sd = jax.ShapeDtypeStruct((b, s, dr), F32)
    return pl.pallas_call(
        _rglru_kernel,
        grid=(b, nc),
        in_specs=[main(False), prev(False), nxt(False), main(True), prev(True), nxt(True),
                  const((RG_CONV, dr)), const((1, dr)), const((2, dr, 2 * dr)), const((2, 1, 2 * dr)),
                  const((2, 1, dr))],
        out_specs=[pl.BlockSpec((1, tc, dr), lambda i, j: (i, j, 0)),
                   pl.BlockSpec((1, tc, dr), lambda i, j: (i, nc - 1 - j, 0))],
        out_shape=[out_sd, out_sd],
        scratch_shapes=[pltpu.VMEM((tc + 2 * SUBLANES, dr), F32), pltpu.VMEM((2, tc, dr), F32),
                        pltpu.VMEM((2, tc, dr), F32), pltpu.VMEM((2, SUBLANES, dr), F32)],
        compiler_params=_cparams(("parallel", "arbitrary")),
        name="rglru",
    )(z, z, z, z, z, z, conv_w, conv_b, wg, bg, cl)


def _mla_proj_kernel(z_ref, qn_ref, kvn_ref, wuq_ref, wk_ref, wv_ref, gq_ref, gk_ref, qb_ref, kb_ref,
                     rc_ref, r1_ref, r2_ref, q_ref, k_ref, v_ref):
    zz = z_ref[0]
    q_lora = qn_ref.shape[1]
    kv_lora = kvn_ref.shape[1]
    ql = zz[:, :q_lora]
    kvl = zz[:, q_lora:q_lora + kv_lora]
    kr = zz[:, q_lora + kv_lora:]
    q = _bdot(_rms(ql, qn_ref[...], q_lora), wuq_ref[...])
    kvn = _rms(kvl, kvn_ref[...], kv_lora)
    kk = _bdot(kvn, wk_ref[...])
    vv = _bdot(kvn, wv_ref[...])
    rc, r1, r2 = rc_ref[...], r1_ref[...], r2_ref[...]
    half = QK_ROPE // 2
    scale = QK_DIM ** -0.5 * LOG2_E

    def norm_rope(xh, g):
        xh = _rms(xh, g, QK_DIM)
        return xh * rc + pltpu.roll(xh, HEAD_SLAB - half, 1) * r1 + pltpu.roll(xh, half, 1) * r2

    for h in range(MLA_HEADS):
        sl = slice(h * HEAD_SLAB, (h + 1) * HEAD_SLAB)
        q_ref[0, h] = (norm_rope(q[:, sl], gq_ref[...]) * scale + qb_ref[...]).astype(BF16)
        k_ref[0, h] = (norm_rope(kk[:, sl] + kr, gk_ref[...]) + kb_ref[...]).astype(BF16)
    for p in range(MLA_HEADS // 2):
        v_ref[0, p] = vv[:, p * LANES:(p + 1) * LANES].astype(BF16)


def _mla_proj(z, col_block, qn, kvn, wuq, wk, wv, gq, gk, qb, kb, rc, r1, r2):
    b, s, _ = z.shape
    tm = min(ROW_TILE, s)
    width = qn.shape[1] + kvn.shape[1] + HEAD_SLAB

    def const(a):
        return pl.BlockSpec(a.shape, lambda i, j: (0,) * a.ndim)

    def rope(a):
        return pl.BlockSpec((tm, HEAD_SLAB), lambda i, j: (j, 0))

    hp = MLA_HEADS // 2
    return pl.pallas_call(
        _mla_proj_kernel,
        grid=(b, s // tm),
        in_specs=[pl.BlockSpec((1, tm, width), lambda i, j: (i, j, col_block)),
                  const(qn), const(kvn), const(wuq), const(wk), const(wv), const(gq), const(gk), const(qb), const(kb),
                  rope(rc), rope(r1), rope(r2)],
        out_specs=[pl.BlockSpec((1, MLA_HEADS, tm, HEAD_SLAB), lambda i, j: (i, 0, j, 0)),
                   pl.BlockSpec((1, MLA_HEADS, tm, HEAD_SLAB), lambda i, j: (i, 0, j, 0)),
                   pl.BlockSpec((1, hp, tm, LANES), lambda i, j: (i, 0, j, 0))],
        out_shape=[jax.ShapeDtypeStruct((b, MLA_HEADS, s, HEAD_SLAB), BF16),
                   jax.ShapeDtypeStruct((b, MLA_HEADS, s, HEAD_SLAB), BF16),
                   jax.ShapeDtypeStruct((b, hp, s, LANES), BF16)],
        compiler_params=_cparams(("parallel", "parallel")),
        name="mla_proj",
    )(z, qn, kvn, wuq, wk, wv, gq, gk, qb, kb, rc, r1, r2)


def _flash_kernel(q_ref, k_ref, v_ref, o_ref, m_ref, l_ref, acc_ref):
    ik = pl.program_id(3)

    @pl.when(ik == 0)
    def _():
        m_ref[...] = jnp.full_like(m_ref, -jnp.inf)
        l_ref[...] = jnp.zeros_like(l_ref)
        acc_ref[...] = jnp.zeros_like(acc_ref)

    v = v_ref[0, 0]
    for hh in range(2):
        s = lax.dot_general(q_ref[0, hh], k_ref[0, hh], (((1,), (1,)), ((), ())), preferred_element_type=F32)
        m_prev = m_ref[hh]
        m_new = jnp.maximum(m_prev, jnp.max(s, axis=-1, keepdims=True))
        alpha = jnp.exp2(m_prev - m_new)
        p = jnp.exp2(s - m_new[:, :1])
        l_ref[hh] = alpha * l_ref[hh] + jnp.sum(p, axis=-1, keepdims=True)
        acc_ref[hh] = alpha * acc_ref[hh] + jnp.dot(p.astype(BF16), v, preferred_element_type=F32)
        m_ref[hh] = m_new

    @pl.when(ik == pl.num_programs(3) - 1)
    def _():
        lane = lax.broadcasted_iota(jnp.int32, acc_ref.shape[1:], 1)
        o_ref[0] = jnp.where(lane < V_DIM, acc_ref[0] / l_ref[0], acc_ref[1] / l_ref[1])


def _flash_bounded_kernel(q_ref, k_ref, v_ref, o_ref, l_ref, acc_ref):
    ik = pl.program_id(3)

    @pl.when(ik == 0)
    def _():
        l_ref[...] = jnp.zeros_like(l_ref)
        acc_ref[...] = jnp.zeros_like(acc_ref)

    v = v_ref[0, 0]
    for hh in range(2):
        s = lax.dot_general(q_ref[0, hh], k_ref[0, hh], (((1,), (1,)), ((), ())), preferred_element_type=F32)
        p = jnp.exp2(s)
        l_ref[hh] = l_ref[hh] + jnp.sum(p, axis=-1, keepdims=True)
        acc_ref[hh] = acc_ref[hh] + jnp.dot(p.astype(BF16), v, preferred_element_type=F32)

    @pl.when(ik == pl.num_programs(3) - 1)
    def _():
        lane = lax.broadcasted_iota(jnp.int32, acc_ref.shape[1:], 1)
        o_ref[0] = jnp.where(lane < V_DIM, acc_ref[0] / l_ref[0], acc_ref[1] / l_ref[1])


def _attention(q, k, v, bounded):
    b, h, s, _ = q.shape
    tq = min(ATTN_TQ, s)
    tk = min(ATTN_TK, s)
    hp = h // 2
    return pl.pallas_call(
        _flash_bounded_kernel if bounded else _flash_kernel,
        grid=(b, hp, s // tq, s // tk),
        in_specs=[pl.BlockSpec((1, 2, tq, HEAD_SLAB), lambda i, p, a, c: (i, p, a, 0)),
                  pl.BlockSpec((1, 2, tk, HEAD_SLAB), lambda i, p, a, c: (i, p, c, 0)),
                  pl.BlockSpec((1, 1, tk, LANES), lambda i, p, a, c: (i, p, c, 0))],
        out_specs=pl.BlockSpec((1, tq, LANES), lambda i, p, a, c: (i, a, p)),
        out_shape=jax.ShapeDtypeStruct((b, s, hp * LANES), F32),
        scratch_shapes=[pltpu.VMEM((2, tq, LANES), F32)] * (2 if bounded else 3),
        compiler_params=_cparams(("parallel", "parallel", "parallel", "arbitrary")),
        name="attention_bounded" if bounded else "attention",
    )(q, k, v)


def _ab_out_kernel(x_ref, hf_ref, hb_ref, y_ref, at_ref, w_ref, g_ref, o_ref):
    rg = (hf_ref[0] + hb_ref[0]) * jax.nn.gelu(y_ref[0])
    cat = jnp.concatenate([rg.astype(BF16), at_ref[0].astype(BF16)], axis=-1)
    m = jnp.dot(cat, w_ref[...], preferred_element_type=F32)
    o_ref[0] = x_ref[0] + g_ref[0] * m


def _ab_out(x, hf, hb, z, attn, w, gate):
    b, s, d = x.shape
    dr = hf.shape[2]
    da = attn.shape[2]
    tm = min(ROW_TILE, s)
    return pl.pallas_call(
        _ab_out_kernel,
        grid=(b, s // tm),
        in_specs=[pl.BlockSpec((1, tm, d), lambda i, j: (i, j, 0)),
                  pl.BlockSpec((1, tm, dr), lambda i, j: (i, j, 0)),
                  pl.BlockSpec((1, tm, dr), lambda i, j: (i, j, 0)),
                  pl.BlockSpec((1, tm, dr), lambda i, j: (i, j, 1)),
                  pl.BlockSpec((1, tm, da), lambda i, j: (i, j, 0)),
                  pl.BlockSpec(w.shape, lambda i, j: (0, 0)),
                  pl.BlockSpec((1, 1, d), lambda i, j: (i, 0, 0))],
        out_specs=pl.BlockSpec((1, tm, d), lambda i, j: (i, j, 0)),
        out_shape=jax.ShapeDtypeStruct((b, s, d), F32),
        compiler_params=_cparams(("parallel", "parallel")),
        name="ab_out",
    )(x, hf, hb, z, attn, w, gate)


def _c_in_kernel(x_ref, g_ref, sc_ref, sh_ref, w_ref, bg_ref, cx_ref):
    h = _norm_mod(x_ref[0], g_ref[...], sc_ref[0], sh_ref[0])
    z = _bdot(h, w_ref[...])
    dc = bg_ref.shape[2]
    bg_ref[0] = z[:, :dc]
    cx_ref[0] = z[:, dc:2 * dc] * z[:, 2 * dc:]


def _c_in(x, gain, sc, sh, w):
    b, s, d = x.shape
    dc = w.shape[1] // 3
    tm = min(ROW_TILE, s)
    sd = jax.ShapeDtypeStruct((b, s, dc), F32)
    return pl.pallas_call(
        _c_in_kernel,
        grid=(b, s // tm),
        in_specs=[pl.BlockSpec((1, tm, d), lambda i, j: (i, j, 0)),
                  pl.BlockSpec((1, d), lambda i, j: (0, 0)),
                  pl.BlockSpec((1, 1, d), lambda i, j: (i, 0, 0)),
                  pl.BlockSpec((1, 1, d), lambda i, j: (i, 0, 0)),
                  pl.BlockSpec(w.shape, lambda i, j: (0, 0))],
        out_specs=[pl.BlockSpec((1, tm, dc), lambda i, j: (i, j, 0))] * 2,
        out_shape=[sd, sd],
        compiler_params=_cparams(("parallel", "parallel")),
        name="c_in",
    )(x, gain, sc, sh, w)


def _c_out_kernel(x_ref, bg_ref, cx_ref, cp_ref, cn_ref, cw_ref, w_ref, g_ref, o_ref, ext_ref):
    j = pl.program_id(1)
    tm = cx_ref.shape[1]
    ext_ref[0:SUBLANES] = jnp.where(j == 0, 0.0, cp_ref[0])
    ext_ref[SUBLANES:SUBLANES + tm] = cx_ref[0]
    ext_ref[SUBLANES + tm:2 * SUBLANES + tm] = jnp.where(j == pl.num_programs(1) - 1, 0.0, cn_ref[0])
    conv = cw_ref[0:1, :] * ext_ref[SUBLANES - 1:SUBLANES - 1 + tm]
    conv = conv + cw_ref[1:2, :] * ext_ref[SUBLANES:SUBLANES + tm]
    conv = conv + cw_ref[2:3, :] * ext_ref[SUBLANES + 1:SUBLANES + 1 + tm]
    m = _bdot(bg_ref[0] * conv, w_ref[...])
    o_ref[0] = x_ref[0] + g_ref[0] * m


def _c_out(x, bg, cx, conv_w, w, gate):
    b, s, d = x.shape
    dc = bg.shape[2]
    tm = min(ROW_TILE, s)
    nb8 = s // SUBLANES
    t8 = tm // SUBLANES
    return pl.pallas_call(
        _c_out_kernel,
        grid=(b, s // tm),
        in_specs=[pl.BlockSpec((1, tm, d), lambda i, j: (i, j, 0)),
                  pl.BlockSpec((1, tm, dc), lambda i, j: (i, j, 0)),
                  pl.BlockSpec((1, tm, dc), lambda i, j: (i, j, 0)),
                  pl.BlockSpec((1, SUBLANES, dc), lambda i, j: (i, jnp.maximum(j * t8 - 1, 0), 0)),
                  pl.BlockSpec((1, SUBLANES, dc), lambda i, j: (i, jnp.minimum((j + 1) * t8, nb8 - 1), 0)),
                  pl.BlockSpec(conv_w.shape, lambda i, j: (0, 0)),
                  pl.BlockSpec(w.shape, lambda i, j: (0, 0)),
                  pl.BlockSpec((1, 1, d), lambda i, j: (i, 0, 0))],
        out_specs=pl.BlockSpec((1, tm, d), lambda i, j: (i, j, 0)),
        out_shape=jax.ShapeDtypeStruct((b, s, d), F32),
        scratch_shapes=[pltpu.VMEM((tm + 2 * SUBLANES, dc), F32)],
        compiler_params=_cparams(("parallel", "parallel")),
        name="c_out",
    )(x, bg, cx, cx, cx, conv_w, w, gate)


def _top_rows(s, k, payload=None):
    n = s.shape[0]
    ridx = lax.broadcasted_iota(jnp.int32, s.shape, 0).astype(F32)
    vals, ids = [], []
    for _ in range(k):
        m = jnp.max(s, axis=0, keepdims=True)
        first = jnp.min(jnp.where(s == m, ridx, float(n)), axis=0, keepdims=True)
        sel = ridx == first
        ids.append(first if payload is None else jnp.sum(jnp.where(sel, payload, 0.0), axis=0, keepdims=True))
        vals.append(m)
        s = jnp.where(sel, -jnp.inf, s)
    return vals, ids


_PAIRS = [(a, b) for a in range(PEER_TOPK) for b in range(PEER_TOPK) if (a + 1) * (b + 1) <= PEER_TOPK]


def _route_kernel(x_ref, g_ref, sc_ref, sh_ref, wq_ref, k1_ref, k2_ref, h_ref, idx_ref, gate_ref, q_scr, sel_scr):
    h = _norm_mod(x_ref[0], g_ref[...], sc_ref[0], sh_ref[0])
    h_ref[0] = h
    q_scr[...] = _bdot(h, wq_ref[...])
    n_keys = k1_ref.shape[0]
    half = k1_ref.shape[1]
    tm = x_ref.shape[1]
    nt = (((1,), (1,)), ((), ()))
    pad = -len(_PAIRS) % SUBLANES

    def head(hd, carry):
        off = pl.multiple_of(hd * 2 * half, 2 * half)
        q1 = q_scr[:, pl.ds(off, half)].astype(BF16)
        q2 = q_scr[:, pl.ds(off + half, half)].astype(BF16)
        s1 = lax.dot_general(k1_ref[...], q1, nt, preferred_element_type=F32)
        s2 = lax.dot_general(k2_ref[...], q2, nt, preferred_element_type=F32)
        v1, i1 = _top_rows(s1, PEER_TOPK)
        v2, i2 = _top_rows(s2, PEER_TOPK)
        cv = [v1[a] + v2[b] for a, b in _PAIRS] + [jnp.full((pad, tm), -jnp.inf, F32)]
        ce = [i1[a] * float(n_keys) + i2[b] for a, b in _PAIRS] + [jnp.zeros((pad, tm), F32)]
        vs, es = _top_rows(jnp.concatenate(cv, axis=0), PEER_TOPK, jnp.concatenate(ce, axis=0))
        vs = jnp.concatenate(vs, axis=0)
        e = jnp.exp(vs - vs[0:1])
        row = pl.multiple_of(hd * PEER_TOPK, PEER_TOPK)
        sel_scr[0, pl.ds(row, PEER_TOPK), :] = e / jnp.sum(e, axis=0, keepdims=True)
        sel_scr[1, pl.ds(row, PEER_TOPK), :] = jnp.concatenate(es, axis=0)
        return carry

    lax.fori_loop(0, PEER_HEADS, head, 0)
    gate_ref[0] = sel_scr[0].T
    idx_ref[0] = sel_scr[1].T.astype(jnp.int32)


def _route(x, gain, sc, sh, wq, k1, k2):
    b, s, d = x.shape
    tm = min(ROUTE_TILE, s)
    nsel = PEER_HEADS * PEER_TOPK
    return pl.pallas_call(
        _route_kernel,
        grid=(b, s // tm),
        in_specs=[pl.BlockSpec((1, tm, d), lambda i, j: (i, j, 0)),
                  pl.BlockSpec((1, d), lambda i, j: (0, 0)),
                  pl.BlockSpec((1, 1, d), lambda i, j: (i, 0, 0)),
                  pl.BlockSpec((1, 1, d), lambda i, j: (i, 0, 0)),
                  pl.BlockSpec(wq.shape, lambda i, j: (0, 0)),
                  pl.BlockSpec(k1.shape, lambda i, j: (0, 0)),
                  pl.BlockSpec(k2.shape, lambda i, j: (0, 0))],
        out_specs=[pl.BlockSpec((1, tm, d), lambda i, j: (i, j, 0)),
                   pl.BlockSpec((1, tm, nsel), lambda i, j: (i, j, 0)),
                   pl.BlockSpec((1, tm, nsel), lambda i, j: (i, j, 0))],
        out_shape=[jax.ShapeDtypeStruct((b, s, d), F32),
                   jax.ShapeDtypeStruct((b, s, nsel), jnp.int32),
                   jax.ShapeDtypeStruct((b, s, nsel), F32)],
        scratch_shapes=[pltpu.VMEM((tm, wq.shape[1]), F32), pltpu.VMEM((2, nsel, tm), F32)],
        compiler_params=_cparams(("parallel", "parallel")),
        name="peer_route",
    )(x, gain, sc, sh, wq, k1, k2)


def _pack_table(t):
    n, d = t.shape
    chunks = d // LANES
    bits = lax.bitcast_convert_type(t.astype(BF16), jnp.uint16).astype(jnp.uint32)
    bits = bits.reshape(n * chunks // 2, 2, LANES)
    return lax.bitcast_convert_type(bits[:, 0, :] | (bits[:, 1, :] << 16), jnp.int32)


def _gather_rows(idx_ref, t, tbl_ref, wr):
    rows = [tbl_ref[pl.ds(pl.multiple_of(idx_ref[t, r], wr), wr), :] for r in range(idx_ref.shape[1])]
    return pltpu.bitcast(jnp.concatenate(rows, axis=0), BF16)


def _idx_copy(idx_vmem, half, smem, sem):
    sub = smem.shape[0]
    return pltpu.make_async_copy(idx_vmem.at[pl.ds(half * sub, sub)], smem, sem)


def _staged_halves(idx_cur, idx_next, smem_a, smem_b, sems, process):
    i = pl.program_id(0)
    sub = smem_a.shape[0]

    @pl.when(i == 0)
    def _():
        _idx_copy(idx_cur, 0, smem_a, sems.at[0]).start()

    _idx_copy(idx_cur, 0, smem_a, sems.at[0]).wait()
    _idx_copy(idx_cur, 1, smem_b, sems.at[1]).start()
    process(smem_a, 0)
    _idx_copy(idx_cur, 1, smem_b, sems.at[1]).wait()

    @pl.when(i + 1 < pl.num_programs(0))
    def _():
        _idx_copy(idx_next, 0, smem_a, sems.at[0]).start()

    process(smem_b, sub)


def _split_bf16(a):
    hi = a.astype(BF16).astype(F32)
    return jnp.concatenate([hi, a - hi], axis=0).astype(BF16)


def _chunk_mask(chunks, width):
    lane = lax.broadcasted_iota(jnp.int32, (chunks, width), 1)
    return lane % chunks == lax.broadcasted_iota(jnp.int32, (chunks, width), 0)


def _peer_blocks(t):
    sub = min(PEER_TB, t // 2)
    assert t % (2 * sub) == 0
    return sub, 2 * sub


def _idx_scratch(sub, nsel):
    return [pltpu.SMEM((sub, nsel), jnp.int32), pltpu.SMEM((sub, nsel), jnp.int32), pltpu.SemaphoreType.DMA((2,))]


def _pack_pairs(t):
    d = t.shape[1]
    bits = lax.bitcast_convert_type(t.astype(BF16), jnp.uint16).astype(jnp.uint32)
    return lax.bitcast_convert_type(bits[:, :d // 2] | (bits[:, d // 2:] << 16), jnp.int32)


def _sc_scores(tbl, idx, x):
    t, nsel = idx.shape
    w = tbl.shape[1]
    info = plsc.get_sparse_core_info()
    lanes = info.num_lanes
    workers = info.num_cores * info.num_subcores
    per = t // workers
    half = nsel // 2
    assert t % (workers * SC_TOKENS) == 0 and half % lanes == 0 and w % lanes == 0
    mesh = plsc.VectorSubcoreMesh(core_axis_name="c", subcore_axis_name="s")

    @functools.partial(
        pl.kernel, mesh=mesh, out_type=jax.ShapeDtypeStruct((t, nsel), F32),
        scratch_types=[pltpu.VMEM((SC_TOKENS, nsel), jnp.int32), pltpu.VMEM((SC_TOKENS, 2 * w), F32),
                       pltpu.VMEM((2, half, w), jnp.int32), pltpu.VMEM((SC_TOKENS, nsel), F32),
                       pltpu.SemaphoreType.DMA((2,))],
        compiler_params=pltpu.CompilerParams(needs_layout_passes=False),
        name="peer_scores")
    def scores(tbl_hbm, idx_hbm, x_hbm, out_hbm, idx_v, x_v, rows_v, s_v, sems):
        wid = lax.axis_index("s") * info.num_cores + lax.axis_index("c")
        lane = lax.iota(jnp.int32, lanes)

        def gather(tt, hb):
            return pltpu.make_async_copy(tbl_hbm.at[idx_v.at[tt, pl.ds(hb * half, half)]], rows_v.at[hb], sems.at[hb])

        def compute(tt, hb):
            @pl.loop(0, half // lanes)
            def _(rg):
                def chunk(c, accs):
                    xl = x_v[tt, pl.ds(c * lanes, lanes)]
                    xh = x_v[tt, pl.ds(w + c * lanes, lanes)]
                    out = []
                    for r in range(lanes):
                        wv = rows_v[hb, rg * lanes + r, pl.ds(c * lanes, lanes)]
                        lo = lax.bitcast_convert_type(wv << 16, F32)
                        hi = lax.bitcast_convert_type(wv & jnp.int32(-65536), F32)
                        out.append(accs[r] + lo * xl + hi * xh)
                    return tuple(out)

                accs = lax.fori_loop(0, w // lanes, chunk, tuple(jnp.zeros((lanes,), F32) for _ in range(lanes)))
                res = jnp.zeros((lanes,), F32)
                for r in range(lanes):
                    res = jnp.where(lane == r, jnp.sum(accs[r]), res)
                s_v[tt, pl.ds(hb * half + rg * lanes, lanes)] = res

        @pl.loop(0, per // SC_TOKENS)
        def _(bi):
            t0 = wid * per + bi * SC_TOKENS
            pltpu.sync_copy(idx_hbm.at[pl.ds(t0, SC_TOKENS)], idx_v)
            pltpu.sync_copy(x_hbm.at[pl.ds(t0, SC_TOKENS)], x_v)
            gather(0, 0).start()

            @pl.loop(0, SC_TOKENS)
            def _(tt):
                gather(tt, 1).start()
                gather(tt, 0).wait()
                compute(tt, 0)

                @pl.when(tt + 1 < SC_TOKENS)
                def _():
                    gather(tt + 1, 0).start()

                gather(tt, 1).wait()
                compute(tt, 1)

            pltpu.sync_copy(s_v, out_hbm.at[pl.ds(t0, SC_TOKENS)])

    return scores(tbl, idx, x)


def _peer_v_kernel(idx_cur, idx_next, s_ref, gts_ref, spread_ref, tbl_ref, x_ref, g_ref, o_ref, smem_a, smem_b, sems, o_scr,
                   act_ref):
    tb = x_ref.shape[0]
    chunks = x_ref.shape[1] // LANES
    width = act_ref.shape[1]
    mask = _chunk_mask(chunks, width)
    a = _split_bf16(jax.nn.gelu(s_ref[...]) * gts_ref[...])
    a = jnp.dot(a, spread_ref[...], preferred_element_type=F32)
    act_ref[...] = a[:tb] + a[tb:]

    def process(idx_ref, first):
        for t in range(idx_ref.shape[0]):
            gb = _gather_rows(idx_ref, t, tbl_ref, chunks // 2)
            a = jnp.where(mask, jnp.broadcast_to(act_ref[first + t:first + t + 1, :], (chunks, width)), 0.0)
            o = jnp.dot(_split_bf16(a), gb, preferred_element_type=F32)
            o_scr[(first + t) * chunks:(first + t + 1) * chunks, :] = o[:chunks] + o[chunks:]

    _staged_halves(idx_cur, idx_next, smem_a, smem_b, sems, process)
    for c in range(chunks):
        sl = slice(c * LANES, (c + 1) * LANES)
        o_ref[:, sl] = x_ref[:, sl] + g_ref[0][:, sl] * o_scr[pl.ds(c, tb, stride=chunks), :]


def _peer_v(idx, scores, gts, table, x, gate, tokens_per_batch):
    t, nsel = idx.shape
    d = x.shape[1]
    chunks = d // LANES
    width = nsel * chunks
    sub, tb = _peer_blocks(t)
    assert tokens_per_batch % tb == 0
    spread = jnp.kron(jnp.eye(nsel, dtype=F32), jnp.ones((1, chunks), F32)).astype(BF16)
    steps = t // tb
    return pl.pallas_call(
        _peer_v_kernel,
        grid=(steps,),
        in_specs=[pl.BlockSpec((tb, nsel), lambda i: (i, 0)),
                  pl.BlockSpec((tb, nsel), lambda i: (jnp.minimum(i + 1, steps - 1), 0)),
                  pl.BlockSpec((tb, nsel), lambda i: (i, 0)),
                  pl.BlockSpec((tb, nsel), lambda i: (i, 0)),
                  pl.BlockSpec(spread.shape, lambda i: (0, 0), pipeline_mode=pl.Buffered(1)),
                  pl.BlockSpec(table.shape, lambda i: (0, 0), pipeline_mode=pl.Buffered(1)),
                  pl.BlockSpec((tb, d), lambda i: (i, 0)),
                  pl.BlockSpec((1, 1, d), lambda i: (i * tb // tokens_per_batch, 0, 0))],
        out_specs=pl.BlockSpec((tb, d), lambda i: (i, 0)),
        out_shape=jax.ShapeDtypeStruct(x.shape, F32),
        scratch_shapes=_idx_scratch(sub, nsel) + [pltpu.VMEM((tb * chunks, LANES), F32), pltpu.VMEM((tb, width), F32)],
        compiler_params=_cparams(("arbitrary",)),
        name="peer_v",
    )(idx, idx, scores, gts, spread, table, x, gate)


def _peer_front(x, gain, sc, sh, wq, k1, k2, u_tbl):
    b, s, d = x.shape
    h, idx, gts = _route(x, gain, sc, sh, wq, k1, k2)
    nsel = idx.shape[2]
    idx = idx.reshape(b * s, nsel)
    return idx, gts.reshape(b * s, nsel), _sc_scores(u_tbl, idx, h.reshape(b * s, d))


def _peer_back(x, gate, idx, gts, scores, v_tbl):
    b, s, d = x.shape
    out = _peer_v(idx * (d // LANES // 2), scores, gts, v_tbl, x.reshape(b * s, d), gate, s)
    return out.reshape(b, s, d)


def _block_diag(w):
    h, i, j = w.shape
    return jnp.einsum('hij,hg->higj', w, jnp.eye(h, dtype=w.dtype)).reshape(h * i, h * j)


def _rope_tables(s):
    half = QK_ROPE // 2
    inv = 1.0 / (ROPE_THETA ** (jnp.arange(0, QK_ROPE, 2, dtype=F32) / QK_ROPE))
    ang = jnp.arange(s, dtype=F32)[:, None] * inv[None, :]
    cos, sin = jnp.cos(ang), jnp.sin(ang)
    z = jnp.zeros((s, QK_NOPE), F32)
    tail = jnp.zeros((s, HEAD_SLAB - QK_DIM), F32)
    zh = jnp.zeros((s, half), F32)
    rc = jnp.concatenate([z + 1.0, cos, cos, tail + 1.0], axis=1)
    r1 = jnp.concatenate([z, -sin, zh, tail], axis=1)
    r2 = jnp.concatenate([z, zh, sin, tail], axis=1)
    return rc, r1, r2


def _pad_last(a, n):
    return jnp.pad(a, [(0, 0)] * (a.ndim - 1) + [(0, n - a.shape[-1])])


def _mixer_ab(x, n1, sc1, sh1, g1, p, j):
    b, s, d = x.shape
    d_rnn = p['rg_conv_w'].shape[2]
    q_lora = p['mla_q_norm'].shape[1]
    kv_lora = p['mla_kv_norm'].shape[1]
    w_in = p['ab_w_in'][j]
    lat = 2 * d_rnn + q_lora + kv_lora
    w0 = jnp.concatenate([w_in[:, :lat], jnp.zeros((d, QK_NOPE), F32), w_in[:, lat:],
                          jnp.zeros((d, HEAD_SLAB - QK_DIM), F32)], axis=1).astype(BF16)
    z = _in_proj(x, n1, sc1, sh1, w0)
    wg = jnp.stack([jnp.concatenate([_block_diag(p['rg_wa'][j][k]), _block_diag(p['rg_wx'][j][k])], axis=1)
                    for k in range(2)]).astype(BF16)
    bg = jnp.concatenate([p['rg_ba'][j], p['rg_bx'][j]], axis=1)[:, None, :]
    cl = (-RG_C * jax.nn.softplus(-p['rg_lambda'][j]))[:, None, :]
    hf, hb = _rglru(z, p['rg_conv_w'][j], p['rg_conv_b'][j][None, :], wg, bg, cl)
    wuq = _pad_last(p['mla_w_uq'][j].reshape(q_lora, MLA_HEADS, QK_DIM), HEAD_SLAB)
    wuq = wuq.reshape(q_lora, MLA_HEADS * HEAD_SLAB).astype(BF16)
    wkv = p['mla_w_ukv'][j].reshape(kv_lora, MLA_HEADS, QK_NOPE + V_DIM)
    wk = _pad_last(wkv[:, :, :QK_NOPE], HEAD_SLAB).reshape(kv_lora, MLA_HEADS * HEAD_SLAB).astype(BF16)
    wv = wkv[:, :, QK_NOPE:].reshape(kv_lora, MLA_HEADS * V_DIM).astype(BF16)
    gq = _pad_last(p['mla_qn_q'][j][None, :], HEAD_SLAB)
    gk = _pad_last(p['mla_qn_k'][j][None, :], HEAD_SLAB)
    rc, r1, r2 = _rope_tables(s)
    width = q_lora + kv_lora + HEAD_SLAB
    assert (2 * d_rnn) % width == 0
    bound = 1.02 * QK_DIM ** 0.5 * jnp.max(jnp.abs(gq)) * jnp.max(jnp.abs(gk))
    spare = jnp.arange(HEAD_SLAB)[None, :] == HEAD_SLAB - 1
    qb = jnp.where(spare, 1.0, 0.0).astype(F32)
    kb = jnp.where(spare, -bound * LOG2_E, 0.0).astype(F32)
    q, k, v = _mla_proj(z, 2 * d_rnn // width, p['mla_q_norm'][j][None, :], p['mla_kv_norm'][j][None, :],
                        wuq, wk, wv, gq, gk, qb, kb, rc, r1, r2)
    attn = lax.cond(bound < MAX_SOFTMAX_BOUND, functools.partial(_attention, bounded=True),
                    functools.partial(_attention, bounded=False), q, k, v)
    return _ab_out(x, hf, hb, z, attn, p['ab_w_out'][j].astype(BF16), g1)


def _mixer_c(x, n1, sc1, sh1, g1, p, j):
    bgate, cx = _c_in(x, n1, sc1, sh1, p['c_w_in'][j].astype(BF16))
    return _c_out(x, bgate, cx, p['c_conv_w'][j], p['c_w_out'][j].astype(BF16), g1)


def _layer_front(x, mod_i, p, i):
    sh1, sc1, g1, sh2, sc2, g2 = [m[:, None, :] for m in jnp.split(mod_i, 6, axis=-1)]
    mixer = _mixer_ab if i % 2 == 0 else _mixer_c
    x = mixer(x, p['norm1_g'][i][None, :], sc1, sh1, g1, p, i // 2)
    front = _peer_front(x, p['norm2_g'][i][None, :], sc2, sh2, p['peer_wq'][i].astype(BF16),
                        p['peer_k1'][i].astype(BF16), p['peer_k2'][i].astype(BF16), _pack_pairs(p['peer_u'][i]))
    return (x, g2) + front


def _layer_back(state, p, i):
    x, g2, idx, gts, scores = state
    return _peer_back(x, g2, idx, gts, scores, _pack_table(p['peer_v'][i]))


def kernel(x_prompt, x_sample, c_prompt, c_sample, ada_w, ada_b, norm1_g, norm2_g, ab_w_in, rg_conv_w, rg_conv_b, rg_wa, rg_ba, rg_wx, rg_bx, rg_lambda, mla_q_norm, mla_w_uq, mla_kv_norm, mla_w_ukv, mla_qn_q, mla_qn_k, ab_w_out, c_w_in, c_conv_w, c_w_out, peer_wq, peer_k1, peer_k2, peer_u, peer_v):
    p = dict(ada_w=ada_w, norm1_g=norm1_g, norm2_g=norm2_g, ab_w_in=ab_w_in, rg_conv_w=rg_conv_w,
             rg_conv_b=rg_conv_b, rg_wa=rg_wa, rg_ba=rg_ba, rg_wx=rg_wx, rg_bx=rg_bx, rg_lambda=rg_lambda,
             mla_q_norm=mla_q_norm, mla_w_uq=mla_w_uq, mla_kv_norm=mla_kv_norm, mla_w_ukv=mla_w_ukv,
             mla_qn_q=mla_qn_q, mla_qn_k=mla_qn_k, ab_w_out=ab_w_out, c_w_in=c_w_in, c_conv_w=c_conv_w,
             c_w_out=c_w_out, peer_wq=peer_wq, peer_k1=peer_k1, peer_k2=peer_k2, peer_u=peer_u, peer_v=peer_v)
    bp, bs = c_prompt.shape[0], c_sample.shape[0]
    rows = -(-(bp + bs) // SUBLANES) * SUBLANES
    c_all = jnp.pad(jnp.concatenate([c_prompt, c_sample], axis=0), ((0, rows - bp - bs), (0, 0)))
    mod = _modulation(c_all, ada_w, ada_b)
    def pieces(x, m):
        b, s, _ = x.shape
        per = max(1, min(b, PIECE_TOKENS // s))
        while b % per:
            per -= 1
        return [(x[j:j + per], m[:, j:j + per]) for j in range(0, b, per)]

    parts = pieces(x_sample, mod[:, bp:bp + bs])
    n_sample = len(parts)
    parts += pieces(x_prompt, mod[:, :bp])
    xs = [x for x, _ in parts]
    for i in range(ada_w.shape[0]):
        fronts = [_layer_front(x, m[i], p, i) for x, (_, m) in zip(xs, parts)]
        xs = [_layer_back(f, p, i) for f in fronts]
    return (jnp.concatenate(xs[n_sample:], axis=0), jnp.concatenate(xs[:n_sample], axis=0))
```

```python
import functools

import jax
import jax.numpy as jnp
from jax import lax
from jax.experimental import pallas as pl
from jax.experimental.pallas import tpu as pltpu
from jax.experimental.pallas import tpu_sc as plsc

F32 = jnp.float32
BF16 = jnp.bfloat16
EPS = 1e-6
LOG2_E = 1.4426950408889634

RG_HEADS = 8
RG_CONV = 4
RG_C = 8.0
MLA_HEADS = 8
QK_NOPE = 64
QK_ROPE = 32
V_DIM = 64
QK_DIM = QK_NOPE + QK_ROPE
ROPE_THETA = 10000.0
PEER_HEADS = 8
PEER_TOPK = 16
HEAD_SLAB = 128
MAX_SOFTMAX_BOUND = 40.0

LANES = 128
SUBLANES = 8
VMEM_LIMIT = 56 * 1024 * 1024

ROW_TILE = 512
SCAN_CHUNK = 1024
ATTN_TQ = 1024
ATTN_TK = 1024
ROUTE_TILE = 512
PEER_TB = 32
SC_TOKENS = 8
PIECE_TOKENS = 16384


def _cparams(sem):
    return pltpu.CompilerParams(dimension_semantics=sem, vmem_limit_bytes=VMEM_LIMIT)


def _norm_mod(x, gain, sc, sh):
    ms = jnp.mean(x * x, axis=-1, keepdims=True)
    return x * lax.rsqrt(ms + EPS) * gain * (1.0 + sc) + sh


def _rms(x, gain, n):
    ms = jnp.sum(x * x, axis=-1, keepdims=True) * (1.0 / n)
    return x * lax.rsqrt(ms + EPS) * gain


def _bdot(a, b):
    return jnp.dot(a.astype(BF16), b, preferred_element_type=F32)


def _mod_kernel(c_ref, w_ref, b_ref, o_ref):
    c = c_ref[...]
    s = c * jax.nn.sigmoid(c)
    o_ref[0] = _bdot(s, w_ref[0].astype(BF16)) + b_ref[0]


def _modulation(c_all, ada_w, ada_b):
    depth, d, n = ada_w.shape
    rows = c_all.shape[0]
    tn = 1536
    return pl.pallas_call(
        _mod_kernel,
        grid=(depth, n // tn),
        in_specs=[
            pl.BlockSpec((rows, d), lambda i, j: (0, 0)),
            pl.BlockSpec((1, d, tn), lambda i, j: (i, 0, j)),
            pl.BlockSpec((1, 1, tn), lambda i, j: (i, 0, j)),
        ],
        out_specs=pl.BlockSpec((1, rows, tn), lambda i, j: (i, 0, j)),
        out_shape=jax.ShapeDtypeStruct((depth, rows, n), F32),
        compiler_params=_cparams(("parallel", "parallel")),
        name="modulation",
    )(c_all, ada_w, ada_b.reshape(depth, 1, n))


def _in_proj_kernel(x_ref, g_ref, sc_ref, sh_ref, w_ref, z_ref):
    h = _norm_mod(x_ref[0], g_ref[...], sc_ref[0], sh_ref[0])
    z_ref[0] = _bdot(h, w_ref[...])


def _in_proj(x, gain, sc, sh, w):
    b, s, d = x.shape
    n = w.shape[1]
    tm = min(ROW_TILE, s)
    return pl.pallas_call(
        _in_proj_kernel,
        grid=(b, s // tm),
        in_specs=[
            pl.BlockSpec((1, tm, d), lambda i, j: (i, j, 0)),
            pl.BlockSpec((1, d), lambda i, j: (0, 0)),
            pl.BlockSpec((1, 1, d), lambda i, j: (i, 0, 0)),
            pl.BlockSpec((1, 1, d), lambda i, j: (i, 0, 0)),
            pl.BlockSpec((d, n), lambda i, j: (0, 0)),
        ],
        out_specs=pl.BlockSpec((1, tm, n), lambda i, j: (i, j, 0)),
        out_shape=jax.ShapeDtypeStruct((b, s, n), F32),
        compiler_params=_cparams(("parallel", "parallel")),
        name="in_proj",
    )(x, gain, sc, sh, w)


def _rglru_kernel(xf_ref, xfp_ref, xfn_ref, xb_ref, xbp_ref, xbn_ref, cw_ref, cb_ref, wg_ref, bg_ref, cl_ref,
                  hf_ref, hb_ref, ext_ref, a_ref, b_ref, carry_ref):
    j = pl.program_id(1)
    nc = pl.num_programs(1)
    tc = xf_ref.shape[1]
    dr = xf_ref.shape[2]
    nt = tc // SUBLANES

    @pl.when(j == 0)
    def _():
        carry_ref[...] = jnp.zeros_like(carry_ref)

    def gates(x_ref, xp_ref, xn_ref, first, last, d):
        ext_ref[0:SUBLANES] = jnp.where(first, 0.0, xp_ref[0])
        ext_ref[SUBLANES:SUBLANES + tc] = x_ref[0]
        ext_ref[SUBLANES + tc:2 * SUBLANES + tc] = jnp.where(last, 0.0, xn_ref[0])
        xc = cb_ref[...]
        for k in range(RG_CONV):
            xc = xc + cw_ref[k:k + 1, :] * ext_ref[SUBLANES - 2 + k:SUBLANES - 2 + k + tc]
        g = _bdot(xc, wg_ref[d]) + bg_ref[d]
        r = jax.nn.sigmoid(g[:, :dr])
        i = jax.nn.sigmoid(g[:, dr:])
        log_a = r * cl_ref[d]
        a = jnp.exp(log_a)
        b = jnp.sqrt(-jnp.tanh(log_a) * (1.0 + a * a)) * (i * xc)
        a_ref[d] = a
        b_ref[d] = b

    rows = lax.broadcasted_iota(jnp.int32, (SUBLANES, dr), 0)

    def scan(d, reverse, out_ref):
        def body(it, carry):
            t = (nt - 1 - it) if reverse else it
            off = pl.multiple_of(t * SUBLANES, SUBLANES)
            a = a_ref[d, pl.ds(off, SUBLANES), :]
            b = b_ref[d, pl.ds(off, SUBLANES), :]
            for s in (1, 2, 4):
                if reverse:
                    a_s = pltpu.roll(a, SUBLANES - s, 0)
                    b_s = pltpu.roll(b, SUBLANES - s, 0)
                    m = rows < SUBLANES - s
                else:
                    a_s = pltpu.roll(a, s, 0)
                    b_s = pltpu.roll(b, s, 0)
                    m = rows >= s
                b = jnp.where(m, a * b_s + b, b)
                a = jnp.where(m, a * a_s, a)
            h = b + a * carry
            out_ref[0, pl.ds(off, SUBLANES), :] = h
            edge = h[0:1] if reverse else h[SUBLANES - 1:SUBLANES]
            return jnp.broadcast_to(edge, (SUBLANES, dr))

        carry_ref[d] = lax.fori_loop(0, nt, body, carry_ref[d])

    gates(xf_ref, xfp_ref, xfn_ref, j == 0, j == nc - 1, 0)
    scan(0, False, hf_ref)
    gates(xb_ref, xbp_ref, xbn_ref, j == nc - 1, j == 0, 1)
    scan(1, True, hb_ref)


def _rglru(z, conv_w, conv_b, wg, bg, cl):
    b, s, _ = z.shape
    dr = conv_w.shape[1]
    tc = min(SCAN_CHUNK, s)
    nc = s // tc
    nb8 = s // SUBLANES
    cb8 = tc // SUBLANES

    def main(rev):
        return pl.BlockSpec((1, tc, dr), (lambda i, j: (i, nc - 1 - j, 0)) if rev else (lambda i, j: (i, j, 0)))

    def prev(rev):
        def f(i, j):
            c = (nc - 1 - j) if rev else j
            return (i, jnp.maximum(c * cb8 - 1, 0), 0)
        return pl.BlockSpec((1, SUBLANES, dr), f)

    def nxt(rev):
        def f(i, j):
            c = (nc - 1 - j) if rev else j
            return (i, jnp.minimum((c + 1) * cb8, nb8 - 1), 0)
        return pl.BlockSpec((1, SUBLANES, dr), f)

    def const(shape):
        return pl.BlockSpec(shape, lambda i, j: (0,) * len(shape))

    out_sd = jax.ShapeDtypeStruct((b, s, dr), F32)
    return pl.pallas_call(
        _rglru_kernel,
        grid=(b, nc),
        in_specs=[main(False), prev(False), nxt(False), main(True), prev(True), nxt(True),
                  const((RG_CONV, dr)), const((1, dr)), const((2, dr, 2 * dr)), const((2, 1, 2 * dr)),
                  const((2, 1, dr))],
        out_specs=[pl.BlockSpec((1, tc, dr), lambda i, j: (i, j, 0)),
                   pl.BlockSpec((1, tc, dr), lambda i, j: (i, nc - 1 - j, 0))],
        out_shape=[out_sd, out_sd],
        scratch_shapes=[pltpu.VMEM((tc + 2 * SUBLANES, dr), F32), pltpu.VMEM((2, tc, dr), F32),
                        pltpu.VMEM((2, tc, dr), F32), pltpu.VMEM((2, SUBLANES, dr), F32)],
        compiler_params=_cparams(("parallel", "arbitrary")),
        name="rglru",
    )(z, z, z, z, z, z, conv_w, conv_b, wg, bg, cl)


def _mla_proj_kernel(z_ref, qn_ref, kvn_ref, wuq_ref, wk_ref, wv_ref, gq_ref, gk_ref, qb_ref, kb_ref,
                     rc_ref, r1_ref, r2_ref, q_ref, k_ref, v_ref):
    zz = z_ref[0]
    q_lora = qn_ref.shape[1]
    kv_lora = kvn_ref.shape[1]
    ql = zz[:, :q_lora]
    kvl = zz[:, q_lora:q_lora + kv_lora]
    kr = zz[:, q_lora + kv_lora:]
    q = _bdot(_rms(ql, qn_ref[...], q_lora), wuq_ref[...])
    kvn = _rms(kvl, kvn_ref[...], kv_lora)
    kk = _bdot(kvn, wk_ref[...])
    vv = _bdot(kvn, wv_ref[...])
    rc, r1, r2 = rc_ref[...], r1_ref[...], r2_ref[...]
    half = QK_ROPE // 2
    scale = QK_DIM ** -0.5 * LOG2_E

    def norm_rope(xh, g):
        xh = _rms(xh, g, QK_DIM)
        return xh * rc + pltpu.roll(xh, HEAD_SLAB - half, 1) * r1 + pltpu.roll(xh, half, 1) * r2

    for h in range(MLA_HEADS):
        sl = slice(h * HEAD_SLAB, (h + 1) * HEAD_SLAB)
        q_ref[0, h] = (norm_rope(q[:, sl], gq_ref[...]) * scale + qb_ref[...]).astype(BF16)
        k_ref[0, h] = (norm_rope(kk[:, sl] + kr, gk_ref[...]) + kb_ref[...]).astype(BF16)
    for p in range(MLA_HEADS // 2):
        v_ref[0, p] = vv[:, p * LANES:(p + 1) * LANES].astype(BF16)


def _mla_proj(z, col_block, qn, kvn, wuq, wk, wv, gq, gk, qb, kb, rc, r1, r2):
    b, s, _ = z.shape
    tm = min(ROW_TILE, s)
    width = qn.shape[1] + kvn.shape[1] + HEAD_SLAB

    def const(a):
        return pl.BlockSpec(a.shape, lambda i, j: (0,) * a.ndim)

    def rope(a):
        return pl.BlockSpec((tm, HEAD_SLAB), lambda i, j: (j, 0))

    hp = MLA_HEADS // 2
    return pl.pallas_call(
        _mla_proj_kernel,
        grid=(b, s // tm),
        in_specs=[pl.BlockSpec((1, tm, width), lambda i, j: (i, j, col_block)),
                  const(qn), const(kvn), const(wuq), const(wk), const(wv), const(gq), const(gk), const(qb), const(kb),
                  rope(rc), rope(r1), rope(r2)],
        out_specs=[pl.BlockSpec((1, MLA_HEADS, tm, HEAD_SLAB), lambda i, j: (i, 0, j, 0)),
                   pl.BlockSpec((1, MLA_HEADS, tm, HEAD_SLAB), lambda i, j: (i, 0, j, 0)),
                   pl.BlockSpec((1, hp, tm, LANES), lambda i, j: (i, 0, j, 0))],
        out_shape=[jax.ShapeDtypeStruct((b, MLA_HEADS, s, HEAD_SLAB), BF16),
                   jax.ShapeDtypeStruct((b, MLA_HEADS, s, HEAD_SLAB), BF16),
                   jax.ShapeDtypeStruct((b, hp, s, LANES), BF16)],
        compiler_params=_cparams(("parallel", "parallel")),
        name="mla_proj",
    )(z, qn, kvn, wuq, wk, wv, gq, gk, qb, kb, rc, r1, r2)


def _flash_kernel(q_ref, k_ref, v_ref, o_ref, m_ref, l_ref, acc_ref):
    ik = pl.program_id(3)

    @pl.when(ik == 0)
    def _():
        m_ref[...] = jnp.full_like(m_ref, -jnp.inf)
        l_ref[...] = jnp.zeros_like(l_ref)
        acc_ref[...] = jnp.zeros_like(acc_ref)

    v = v_ref[0, 0]
    for hh in range(2):
        s = lax.dot_general(q_ref[0, hh], k_ref[0, hh], (((1,), (1,)), ((), ())), preferred_element_type=F32)
        m_prev = m_ref[hh]
        m_new = jnp.maximum(m_prev, jnp.max(s, axis=-1, keepdims=True))
        alpha = jnp.exp2(m_prev - m_new)
        p = jnp.exp2(s - m_new[:, :1])
        l_ref[hh] = alpha * l_ref[hh] + jnp.sum(p, axis=-1, keepdims=True)
        acc_ref[hh] = alpha * acc_ref[hh] + jnp.dot(p.astype(BF16), v, preferred_element_type=F32)
        m_ref[hh] = m_new

    @pl.when(ik == pl.num_programs(3) - 1)
    def _():
        lane = lax.broadcasted_iota(jnp.int32, acc_ref.shape[1:], 1)
        o_ref[0] = jnp.where(lane < V_DIM, acc_ref[0] / l_ref[0], acc_ref[1] / l_ref[1])


def _flash_bounded_kernel(q_ref, k_ref, v_ref, o_ref, l_ref, acc_ref):
    ik = pl.program_id(3)

    @pl.when(ik == 0)
    def _():
        l_ref[...] = jnp.zeros_like(l_ref)
        acc_ref[...] = jnp.zeros_like(acc_ref)

    v = v_ref[0, 0]
    for hh in range(2):
        s = lax.dot_general(q_ref[0, hh], k_ref[0, hh], (((1,), (1,)), ((), ())), preferred_element_type=F32)
        p = jnp.exp2(s)
        l_ref[hh] = l_ref[hh] + jnp.sum(p, axis=-1, keepdims=True)
        acc_ref[hh] = acc_ref[hh] + jnp.dot(p.astype(BF16), v, preferred_element_type=F32)

    @pl.when(ik == pl.num_programs(3) - 1)
    def _():
        lane = lax.broadcasted_iota(jnp.int32, acc_ref.shape[1:], 1)
        o_ref[0] = jnp.where(lane < V_DIM, acc_ref[0] / l_ref[0], acc_ref[1] / l_ref[1])


def _attention(q, k, v, bounded):
    b, h, s, _ = q.shape
    tq = min(ATTN_TQ, s)
    tk = min(ATTN_TK, s)
    hp = h // 2
    return pl.pallas_call(
        _flash_bounded_kernel if bounded else _flash_kernel,
        grid=(b, hp, s // tq, s // tk),
        in_specs=[pl.BlockSpec((1, 2, tq, HEAD_SLAB), lambda i, p, a, c: (i, p, a, 0)),
                  pl.BlockSpec((1, 2, tk, HEAD_SLAB), lambda i, p, a, c: (i, p, c, 0)),
                  pl.BlockSpec((1, 1, tk, LANES), lambda i, p, a, c: (i, p, c, 0))],
        out_specs=pl.BlockSpec((1, tq, LANES), lambda i, p, a, c: (i, a, p)),
        out_shape=jax.ShapeDtypeStruct((b, s, hp * LANES), F32),
        scratch_shapes=[pltpu.VMEM((2, tq, LANES), F32)] * (2 if bounded else 3),
        compiler_params=_cparams(("parallel", "parallel", "parallel", "arbitrary")),
        name="attention_bounded" if bounded else "attention",
    )(q, k, v)


def _ab_out_kernel(x_ref, hf_ref, hb_ref, y_ref, at_ref, w_ref, g_ref, o_ref):
    rg = (hf_ref[0] + hb_ref[0]) * jax.nn.gelu(y_ref[0])
    cat = jnp.concatenate([rg.astype(BF16), at_ref[0].astype(BF16)], axis=-1)
    m = jnp.dot(cat, w_ref[...], preferred_element_type=F32)
    o_ref[0] = x_ref[0] + g_ref[0] * m


def _ab_out(x, hf, hb, z, attn, w, gate):
    b, s, d = x.shape
    dr = hf.shape[2]
    da = attn.shape[2]
    tm = min(ROW_TILE, s)
    return pl.pallas_call(
        _ab_out_kernel,
        grid=(b, s // tm),
        in_specs=[pl.BlockSpec((1, tm, d), lambda i, j: (i, j, 0)),
                  pl.BlockSpec((1, tm, dr), lambda i, j: (i, j, 0)),
                  pl.BlockSpec((1, tm, dr), lambda i, j: (i, j, 0)),
                  pl.BlockSpec((1, tm, dr), lambda i, j: (i, j, 1)),
                  pl.BlockSpec((1, tm, da), lambda i, j: (i, j, 0)),
                  pl.BlockSpec(w.shape, lambda i, j: (0, 0)),
                  pl.BlockSpec((1, 1, d), lambda i, j: (i, 0, 0))],
        out_specs=pl.BlockSpec((1, tm, d), lambda i, j: (i, j, 0)),
        out_shape=jax.ShapeDtypeStruct((b, s, d), F32),
        compiler_params=_cparams(("parallel", "parallel")),
        name="ab_out",
    )(x, hf, hb, z, attn, w, gate)


def _c_in_kernel(x_ref, g_ref, sc_ref, sh_ref, w_ref, bg_ref, cx_ref):
    h = _norm_mod(x_ref[0], g_ref[...], sc_ref[0], sh_ref[0])
    z = _bdot(h, w_ref[...])
    dc = bg_ref.shape[2]
    bg_ref[0] = z[:, :dc]
    cx_ref[0] = z[:, dc:2 * dc] * z[:, 2 * dc:]


def _c_in(x, gain, sc, sh, w):
    b, s, d = x.shape
    dc = w.shape[1] // 3
    tm = min(ROW_TILE, s)
    sd = jax.ShapeDtypeStruct((b, s, dc), F32)
    return pl.pallas_call(
        _c_in_kernel,
        grid=(b, s // tm),
        in_specs=[pl.BlockSpec((1, tm, d), lambda i, j: (i, j, 0)),
                  pl.BlockSpec((1, d), lambda i, j: (0, 0)),
                  pl.BlockSpec((1, 1, d), lambda i, j: (i, 0, 0)),
                  pl.BlockSpec((1, 1, d), lambda i, j: (i, 0, 0)),
                  pl.BlockSpec(w.shape, lambda i, j: (0, 0))],
        out_specs=[pl.BlockSpec((1, tm, dc), lambda i, j: (i, j, 0))] * 2,
        out_shape=[sd, sd],
        compiler_params=_cparams(("parallel", "parallel")),
        name="c_in",
    )(x, gain, sc, sh, w)


def _c_out_kernel(x_ref, bg_ref, cx_ref, cp_ref, cn_ref, cw_ref, w_ref, g_ref, o_ref, ext_ref):
    j = pl.program_id(1)
    tm = cx_ref.shape[1]
    ext_ref[0:SUBLANES] = jnp.where(j == 0, 0.0, cp_ref[0])
    ext_ref[SUBLANES:SUBLANES + tm] = cx_ref[0]
    ext_ref[SUBLANES + tm:2 * SUBLANES + tm] = jnp.where(j == pl.num_programs(1) - 1, 0.0, cn_ref[0])
    conv = cw_ref[0:1, :] * ext_ref[SUBLANES - 1:SUBLANES - 1 + tm]
    conv = conv + cw_ref[1:2, :] * ext_ref[SUBLANES:SUBLANES + tm]
    conv = conv + cw_ref[2:3, :] * ext_ref[SUBLANES + 1:SUBLANES + 1 + tm]
    m = _bdot(bg_ref[0] * conv, w_ref[...])
    o_ref[0] = x_ref[0] + g_ref[0] * m


def _c_out(x, bg, cx, conv_w, w, gate):
    b, s, d = x.shape
    dc = bg.shape[2]
    tm = min(ROW_TILE, s)
    nb8 = s // SUBLANES
    t8 = tm // SUBLANES
    return pl.pallas_call(
        _c_out_kernel,
        grid=(b, s // tm),
        in_specs=[pl.BlockSpec((1, tm, d), lambda i, j: (i, j, 0)),
                  pl.BlockSpec((1, tm, dc), lambda i, j: (i, j, 0)),
                  pl.BlockSpec((1, tm, dc), lambda i, j: (i, j, 0)),
                  pl.BlockSpec((1, SUBLANES, dc), lambda i, j: (i, jnp.maximum(j * t8 - 1, 0), 0)),
                  pl.BlockSpec((1, SUBLANES, dc), lambda i, j: (i, jnp.minimum((j + 1) * t8, nb8 - 1), 0)),
                  pl.BlockSpec(conv_w.shape, lambda i, j: (0, 0)),
                  pl.BlockSpec(w.shape, lambda i, j: (0, 0)),
                  pl.BlockSpec((1, 1, d), lambda i, j: (i, 0, 0))],
        out_specs=pl.BlockSpec((1, tm, d), lambda i, j: (i, j, 0)),
        out_shape=jax.ShapeDtypeStruct((b, s, d), F32),
        scratch_shapes=[pltpu.VMEM((tm + 2 * SUBLANES, dc), F32)],
        compiler_params=_cparams(("parallel", "parallel")),
        name="c_out",
    )(x, bg, cx, cx, cx, conv_w, w, gate)


def _top_rows(s, k, payload=None):
    n = s.shape[0]
    ridx = lax.broadcasted_iota(jnp.int32, s.shape, 0).astype(F32)
    vals, ids = [], []
    for _ in range(k):
        m = jnp.max(s, axis=0, keepdims=True)
        first = jnp.min(jnp.where(s == m, ridx, float(n)), axis=0, keepdims=True)
        sel = ridx == first
        ids.append(first if payload is None else jnp.sum(jnp.where(sel, payload, 0.0), axis=0, keepdims=True))
        vals.append(m)
        s = jnp.where(sel, -jnp.inf, s)
    return vals, ids


_PAIRS = [(a, b) for a in range(PEER_TOPK) for b in range(PEER_TOPK) if (a + 1) * (b + 1) <= PEER_TOPK]


def _route_kernel(x_ref, g_ref, sc_ref, sh_ref, wq_ref, k1_ref, k2_ref, h_ref, idx_ref, gate_ref, q_scr, sel_scr):
    h = _norm_mod(x_ref[0], g_ref[...], sc_ref[0], sh_ref[0])
    h_ref[0] = h
    q_scr[...] = _bdot(h, wq_ref[...])
    n_keys = k1_ref.shape[0]
    half = k1_ref.shape[1]
    tm = x_ref.shape[1]
    nt = (((1,), (1,)), ((), ()))
    pad = -len(_PAIRS) % SUBLANES

    def head(hd, carry):
        off = pl.multiple_of(hd * 2 * half, 2 * half)
        q1 = q_scr[:, pl.ds(off, half)].astype(BF16)
        q2 = q_scr[:, pl.ds(off + half, half)].astype(BF16)
        s1 = lax.dot_general(k1_ref[...], q1, nt, preferred_element_type=F32)
        s2 = lax.dot_general(k2_ref[...], q2, nt, preferred_element_type=F32)
        v1, i1 = _top_rows(s1, PEER_TOPK)
        v2, i2 = _top_rows(s2, PEER_TOPK)
        cv = [v1[a] + v2[b] for a, b in _PAIRS] + [jnp.full((pad, tm), -jnp.inf, F32)]
        ce = [i1[a] * float(n_keys) + i2[b] for a, b in _PAIRS] + [jnp.zeros((pad, tm), F32)]
        vs, es = _top_rows(jnp.concatenate(cv, axis=0), PEER_TOPK, jnp.concatenate(ce, axis=0))
        vs = jnp.concatenate(vs, axis=0)
        e = jnp.exp(vs - vs[0:1])
        row = pl.multiple_of(hd * PEER_TOPK, PEER_TOPK)
        sel_scr[0, pl.ds(row, PEER_TOPK), :] = e / jnp.sum(e, axis=0, keepdims=True)
        sel_scr[1, pl.ds(row, PEER_TOPK), :] = jnp.concatenate(es, axis=0)
        return carry

    lax.fori_loop(0, PEER_HEADS, head, 0)
    gate_ref[0] = sel_scr[0].T
    idx_ref[0] = sel_scr[1].T.astype(jnp.int32)


def _route(x, gain, sc, sh, wq, k1, k2):
    b, s, d = x.shape
    tm = min(ROUTE_TILE, s)
    nsel = PEER_HEADS * PEER_TOPK
    return pl.pallas_call(
        _route_kernel,
        grid=(b, s // tm),
        in_specs=[pl.BlockSpec((1, tm, d), lambda i, j: (i, j, 0)),
                  pl.BlockSpec((1, d), lambda i, j: (0, 0)),
                  pl.BlockSpec((1, 1, d), lambda i, j: (i, 0, 0)),
                  pl.BlockSpec((1, 1, d), lambda i, j: (i, 0, 0)),
                  pl.BlockSpec(wq.shape, lambda i, j: (0, 0)),
                  pl.BlockSpec(k1.shape, lambda i, j: (0, 0)),
                  pl.BlockSpec(k2.shape, lambda i, j: (0, 0))],
        out_specs=[pl.BlockSpec((1, tm, d), lambda i, j: (i, j, 0)),
                   pl.BlockSpec((1, tm, nsel), lambda i, j: (i, j, 0)),
                   pl.BlockSpec((1, tm, nsel), lambda i, j: (i, j, 0))],
        out_shape=[jax.ShapeDtypeStruct((b, s, d), F32),
                   jax.ShapeDtypeStruct((b, s, nsel), jnp.int32),
                   jax.ShapeDtypeStruct((b, s, nsel), F32)],
        scratch_shapes=[pltpu.VMEM((tm, wq.shape[1]), F32), pltpu.VMEM((2, nsel, tm), F32)],
        compiler_params=_cparams(("parallel", "parallel")),
        name="peer_route",
    )(x, gain, sc, sh, wq, k1, k2)


def _pack_table(t):
    n, d = t.shape
    chunks = d // LANES
    bits = lax.bitcast_convert_type(t.astype(BF16), jnp.uint16).astype(jnp.uint32)
    bits = bits.reshape(n * chunks // 2, 2, LANES)
    return lax.bitcast_convert_type(bits[:, 0, :] | (bits[:, 1, :] << 16), jnp.int32)


def _gather_rows(idx_ref, t, tbl_ref, wr):
    rows = [tbl_ref[pl.ds(pl.multiple_of(idx_ref[t, r], wr), wr), :] for r in range(idx_ref.shape[1])]
    return pltpu.bitcast(jnp.concatenate(rows, axis=0), BF16)


def _idx_copy(idx_vmem, half, smem, sem):
    sub = smem.shape[0]
    return pltpu.make_async_copy(idx_vmem.at[pl.ds(half * sub, sub)], smem, sem)


def _staged_halves(idx_cur, idx_next, smem_a, smem_b, sems, process):
    i = pl.program_id(0)
    sub = smem_a.shape[0]

    @pl.when(i == 0)
    def _():
        _idx_copy(idx_cur, 0, smem_a, sems.at[0]).start()

    _idx_copy(idx_cur, 0, smem_a, sems.at[0]).wait()
    _idx_copy(idx_cur, 1, smem_b, sems.at[1]).start()
    process(smem_a, 0)
    _idx_copy(idx_cur, 1, smem_b, sems.at[1]).wait()

    @pl.when(i + 1 < pl.num_programs(0))
    def _():
        _idx_copy(idx_next, 0, smem_a, sems.at[0]).start()

    process(smem_b, sub)


def _split_bf16(a):
    hi = a.astype(BF16).astype(F32)
    return jnp.concatenate([hi, a - hi], axis=0).astype(BF16)


def _chunk_mask(chunks, width):
    lane = lax.broadcasted_iota(jnp.int32, (chunks, width), 1)
    return lane % chunks == lax.broadcasted_iota(jnp.int32, (chunks, width), 0)


def _peer_blocks(t):
    sub = min(PEER_TB, t // 2)
    assert t % (2 * sub) == 0
    return sub, 2 * sub


def _idx_scratch(sub, nsel):
    return [pltpu.SMEM((sub, nsel), jnp.int32), pltpu.SMEM((sub, nsel), jnp.int32), pltpu.SemaphoreType.DMA((2,))]


def _pack_pairs(t):
    d = t.shape[1]
    bits = lax.bitcast_convert_type(t.astype(BF16), jnp.uint16).astype(jnp.uint32)
    return lax.bitcast_convert_type(bits[:, :d // 2] | (bits[:, d // 2:] << 16), jnp.int32)


def _sc_scores(tbl, idx, x):
    t, nsel = idx.shape
    w = tbl.shape[1]
    info = plsc.get_sparse_core_info()
    lanes = info.num_lanes
    workers = info.num_cores * info.num_subcores
    per = t // workers
    half = nsel // 2
    assert t % (workers * SC_TOKENS) == 0 and half % lanes == 0 and w % lanes == 0
    mesh = plsc.VectorSubcoreMesh(core_axis_name="c", subcore_axis_name="s")

    @functools.partial(
        pl.kernel, mesh=mesh, out_type=jax.ShapeDtypeStruct((t, nsel), F32),
        scratch_types=[pltpu.VMEM((SC_TOKENS, nsel), jnp.int32), pltpu.VMEM((SC_TOKENS, 2 * w), F32),
                       pltpu.VMEM((2, half, w), jnp.int32), pltpu.VMEM((SC_TOKENS, nsel), F32),
                       pltpu.SemaphoreType.DMA((2,))],
        compiler_params=pltpu.CompilerParams(needs_layout_passes=False),
        name="peer_scores")
    def scores(tbl_hbm, idx_hbm, x_hbm, out_hbm, idx_v, x_v, rows_v, s_v, sems):
        wid = lax.axis_index("s") * info.num_cores + lax.axis_index("c")
        lane = lax.iota(jnp.int32, lanes)

        def gather(tt, hb):
            return pltpu.make_async_copy(tbl_hbm.at[idx_v.at[tt, pl.ds(hb * half, half)]], rows_v.at[hb], sems.at[hb])

        def compute(tt, hb):
            @pl.loop(0, half // lanes)
            def _(rg):
                def chunk(c, accs):
                    xl = x_v[tt, pl.ds(c * lanes, lanes)]
                    xh = x_v[tt, pl.ds(w + c * lanes, lanes)]
                    out = []
                    for r in range(lanes):
                        wv = rows_v[hb, rg * lanes + r, pl.ds(c * lanes, lanes)]
                        lo = lax.bitcast_convert_type(wv << 16, F32)
                        hi = lax.bitcast_convert_type(wv & jnp.int32(-65536), F32)
                        out.append(accs[r] + lo * xl + hi * xh)
                    return tuple(out)

                accs = lax.fori_loop(0, w // lanes, chunk, tuple(jnp.zeros((lanes,), F32) for _ in range(lanes)))
                res = jnp.zeros((lanes,), F32)
                for r in range(lanes):
                    res = jnp.where(lane == r, jnp.sum(accs[r]), res)
                s_v[tt, pl.ds(hb * half + rg * lanes, lanes)] = res

        @pl.loop(0, per // SC_TOKENS)
        def _(bi):
            t0 = wid * per + bi * SC_TOKENS
            pltpu.sync_copy(idx_hbm.at[pl.ds(t0, SC_TOKENS)], idx_v)
            pltpu.sync_copy(x_hbm.at[pl.ds(t0, SC_TOKENS)], x_v)
            gather(0, 0).start()

            @pl.loop(0, SC_TOKENS)
            def _(tt):
                gather(tt, 1).start()
                gather(tt, 0).wait()
                compute(tt, 0)

                @pl.when(tt + 1 < SC_TOKENS)
                def _():
                    gather(tt + 1, 0).start()

                gather(tt, 1).wait()
                compute(tt, 1)

            pltpu.sync_copy(s_v, out_hbm.at[pl.ds(t0, SC_TOKENS)])

    return scores(tbl, idx, x)


def _peer_v_kernel(idx_cur, idx_next, s_ref, gts_ref, spread_ref, tbl_ref, x_ref, g_ref, o_ref, smem_a, smem_b, sems, o_scr,
                   act_ref):
    tb = x_ref.shape[0]
    chunks = x_ref.shape[1] // LANES
    width = act_ref.shape[1]
    mask = _chunk_mask(chunks, width)
    a = _split_bf16(jax.nn.gelu(s_ref[...]) * gts_ref[...])
    a = jnp.dot(a, spread_ref[...], preferred_element_type=F32)
    act_ref[...] = a[:tb] + a[tb:]

    def process(idx_ref, first):
        for t in range(idx_ref.shape[0]):
            gb = _gather_rows(idx_ref, t, tbl_ref, chunks // 2)
            a = jnp.where(mask, jnp.broadcast_to(act_ref[first + t:first + t + 1, :], (chunks, width)), 0.0)
            o = jnp.dot(_split_bf16(a), gb, preferred_element_type=F32)
            o_scr[(first + t) * chunks:(first + t + 1) * chunks, :] = o[:chunks] + o[chunks:]

    _staged_halves(idx_cur, idx_next, smem_a, smem_b, sems, process)
    for c in range(chunks):
        sl = slice(c * LANES, (c + 1) * LANES)
        o_ref[:, sl] = x_ref[:, sl] + g_ref[0][:, sl] * o_scr[pl.ds(c, tb, stride=chunks), :]


def _peer_v(idx, scores, gts, table, x, gate, tokens_per_batch):
    t, nsel = idx.shape
    d = x.shape[1]
    chunks = d // LANES
    width = nsel * chunks
    sub, tb = _peer_blocks(t)
    assert tokens_per_batch % tb == 0
    spread = jnp.kron(jnp.eye(nsel, dtype=F32), jnp.ones((1, chunks), F32)).astype(BF16)
    steps = t // tb
    return pl.pallas_call(
        _peer_v_kernel,
        grid=(steps,),
        in_specs=[pl.BlockSpec((tb, nsel), lambda i: (i, 0)),
                  pl.BlockSpec((tb, nsel), lambda i: (jnp.minimum(i + 1, steps - 1), 0)),
                  pl.BlockSpec((tb, nsel), lambda i: (i, 0)),
                  pl.BlockSpec((tb, nsel), lambda i: (i, 0)),
                  pl.BlockSpec(spread.shape, lambda i: (0, 0), pipeline_mode=pl.Buffered(1)),
                  pl.BlockSpec(table.shape, lambda i: (0, 0), pipeline_mode=pl.Buffered(1)),
                  pl.BlockSpec((tb, d), lambda i: (i, 0)),
                  pl.BlockSpec((1, 1, d), lambda i: (i * tb // tokens_per_batch, 0, 0))],
        out_specs=pl.BlockSpec((tb, d), lambda i: (i, 0)),
        out_shape=jax.ShapeDtypeStruct(x.shape, F32),
        scratch_shapes=_idx_scratch(sub, nsel) + [pltpu.VMEM((tb * chunks, LANES), F32), pltpu.VMEM((tb, width), F32)],
        compiler_params=_cparams(("arbitrary",)),
        name="peer_v",
    )(idx, idx, scores, gts, spread, table, x, gate)


def _peer_front(x, gain, sc, sh, wq, k1, k2, u_tbl):
    b, s, d = x.shape
    h, idx, gts = _route(x, gain, sc, sh, wq, k1, k2)
    nsel = idx.shape[2]
    idx = idx.reshape(b * s, nsel)
    return idx, gts.reshape(b * s, nsel), _sc_scores(u_tbl, idx, h.reshape(b * s, d))


def _peer_back(x, gate, idx, gts, scores, v_tbl):
    b, s, d = x.shape
    out = _peer_v(idx * (d // LANES // 2), scores, gts, v_tbl, x.reshape(b * s, d), gate, s)
    return out.reshape(b, s, d)


def _block_diag(w):
    h, i, j = w.shape
    return jnp.einsum('hij,hg->higj', w, jnp.eye(h, dtype=w.dtype)).reshape(h * i, h * j)


def _rope_tables(s):
    half = QK_ROPE // 2
    inv = 1.0 / (ROPE_THETA ** (jnp.arange(0, QK_ROPE, 2, dtype=F32) / QK_ROPE))
    ang = jnp.arange(s, dtype=F32)[:, None] * inv[None, :]
    cos, sin = jnp.cos(ang), jnp.sin(ang)
    z = jnp.zeros((s, QK_NOPE), F32)
    tail = jnp.zeros((s, HEAD_SLAB - QK_DIM), F32)
    zh = jnp.zeros((s, half), F32)
    rc = jnp.concatenate([z + 1.0, cos, cos, tail + 1.0], axis=1)
    r1 = jnp.concatenate([z, -sin, zh, tail], axis=1)
    r2 = jnp.concatenate([z, zh, sin, tail], axis=1)
    return rc, r1, r2


def _pad_last(a, n):
    return jnp.pad(a, [(0, 0)] * (a.ndim - 1) + [(0, n - a.shape[-1])])


def _mixer_ab(x, n1, sc1, sh1, g1, p, j):
    b, s, d = x.shape
    d_rnn = p['rg_conv_w'].shape[2]
    q_lora = p['mla_q_norm'].shape[1]
    kv_lora = p['mla_kv_norm'].shape[1]
    w_in = p['ab_w_in'][j]
    lat = 2 * d_rnn + q_lora + kv_lora
    w0 = jnp.concatenate([w_in[:, :lat], jnp.zeros((d, QK_NOPE), F32), w_in[:, lat:],
                          jnp.zeros((d, HEAD_SLAB - QK_DIM), F32)], axis=1).astype(BF16)
    z = _in_proj(x, n1, sc1, sh1, w0)
    wg = jnp.stack([jnp.concatenate([_block_diag(p['rg_wa'][j][k]), _block_diag(p['rg_wx'][j][k])], axis=1)
                    for k in range(2)]).astype(BF16)
    bg = jnp.concatenate([p['rg_ba'][j], p['rg_bx'][j]], axis=1)[:, None, :]
    cl = (-RG_C * jax.nn.softplus(-p['rg_lambda'][j]))[:, None, :]
    hf, hb = _rglru(z, p['rg_conv_w'][j], p['rg_conv_b'][j][None, :], wg, bg, cl)
    wuq = _pad_last(p['mla_w_uq'][j].reshape(q_lora, MLA_HEADS, QK_DIM), HEAD_SLAB)
    wuq = wuq.reshape(q_lora, MLA_HEADS * HEAD_SLAB).astype(BF16)
    wkv = p['mla_w_ukv'][j].reshape(kv_lora, MLA_HEADS, QK_NOPE + V_DIM)
    wk = _pad_last(wkv[:, :, :QK_NOPE], HEAD_SLAB).reshape(kv_lora, MLA_HEADS * HEAD_SLAB).astype(BF16)
    wv = wkv[:, :, QK_NOPE:].reshape(kv_lora, MLA_HEADS * V_DIM).astype(BF16)
    gq = _pad_last(p['mla_qn_q'][j][None, :], HEAD_SLAB)
    gk = _pad_last(p['mla_qn_k'][j][None, :], HEAD_SLAB)
    rc, r1, r2 = _rope_tables(s)
    width = q_lora + kv_lora + HEAD_SLAB
    assert (2 * d_rnn) % width == 0
    bound = 1.02 * QK_DIM ** 0.5 * jnp.max(jnp.abs(gq)) * jnp.max(jnp.abs(gk))
    spare = jnp.arange(HEAD_SLAB)[None, :] == HEAD_SLAB - 1
    qb = jnp.where(spare, 1.0, 0.0).astype(F32)
    kb = jnp.where(spare, -bound * LOG2_E, 0.0).astype(F32)
    q, k, v = _mla_proj(z, 2 * d_rnn // width, p['mla_q_norm'][j][None, :], p['mla_kv_norm'][j][None, :],
                        wuq, wk, wv, gq, gk, qb, kb, rc, r1, r2)
    attn = lax.cond(bound < MAX_SOFTMAX_BOUND, functools.partial(_attention, bounded=True),
                    functools.partial(_attention, bounded=False), q, k, v)
    return _ab_out(x, hf, hb, z, attn, p['ab_w_out'][j].astype(BF16), g1)


def _mixer_c(x, n1, sc1, sh1, g1, p, j):
    bgate, cx = _c_in(x, n1, sc1, sh1, p['c_w_in'][j].astype(BF16))
    return _c_out(x, bgate, cx, p['c_conv_w'][j], p['c_w_out'][j].astype(BF16), g1)


def _layer_front(x, mod_i, p, i):
    sh1, sc1, g1, sh2, sc2, g2 = [m[:, None, :] for m in jnp.split(mod_i, 6, axis=-1)]
    mixer = _mixer_ab if i % 2 == 0 else _mixer_c
    x = mixer(x, p['norm1_g'][i][None, :], sc1, sh1, g1, p, i // 2)
    front = _peer_front(x, p['norm2_g'][i][None, :], sc2, sh2, p['peer_wq'][i].astype(BF16),
                        p['peer_k1'][i].astype(BF16), p['peer_k2'][i].astype(BF16), _pack_pairs(p['peer_u'][i]))
    return (x, g2) + front


def _layer_back(state, p, i):
    x, g2, idx, gts, scores = state
    return _peer_back(x, g2, idx, gts, scores, _pack_table(p['peer_v'][i]))


def kernel(x_prompt, x_sample, c_prompt, c_sample, ada_w, ada_b, norm1_g, norm2_g, ab_w_in, rg_conv_w, rg_conv_b, rg_wa, rg_ba, rg_wx, rg_bx, rg_lambda, mla_q_norm, mla_w_uq, mla_kv_norm, mla_w_ukv, mla_qn_q, mla_qn_k, ab_w_out, c_w_in, c_conv_w, c_w_out, peer_wq, peer_k1, peer_k2, peer_u, peer_v):
    p = dict(ada_w=ada_w, norm1_g=norm1_g, norm2_g=norm2_g, ab_w_in=ab_w_in, rg_conv_w=rg_conv_w,
             rg_conv_b=rg_conv_b, rg_wa=rg_wa, rg_ba=rg_ba, rg_wx=rg_wx, rg_bx=rg_bx, rg_lambda=rg_lambda,
             mla_q_norm=mla_q_norm, mla_w_uq=mla_w_uq, mla_kv_norm=mla_kv_norm, mla_w_ukv=mla_w_ukv,
             mla_qn_q=mla_qn_q, mla_qn_k=mla_qn_k, ab_w_out=ab_w_out, c_w_in=c_w_in, c_conv_w=c_conv_w,
             c_w_out=c_w_out, peer_wq=peer_wq, peer_k1=peer_k1, peer_k2=peer_k2, peer_u=peer_u, peer_v=peer_v)
    bp, bs = c_prompt.shape[0], c_sample.shape[0]
    rows = -(-(bp + bs) // SUBLANES) * SUBLANES
    c_all = jnp.pad(jnp.concatenate([c_prompt, c_sample], axis=0), ((0, rows - bp - bs), (0, 0)))
    mod = _modulation(c_all, ada_w, ada_b)
    def pieces(x, m):
        b, s, _ = x.shape
        per = max(1, min(b, PIECE_TOKENS // s))
        while b % per:
            per -= 1
        return [(x[j:j + per], m[:, j:j + per]) for j in range(0, b, per)]

    parts = pieces(x_sample, mod[:, bp:bp + bs])
    n_sample = len(parts)
    parts += pieces(x_prompt, mod[:, :bp])
    xs = [x for x, _ in parts]
    for i in range(ada_w.shape[0]):
        fronts = [_layer_front(x, m[i], p, i) for x, (_, m) in zip(xs, parts)]
        xs = [_layer_back(f, p, i) for f in fronts]
    return (jnp.concatenate(xs[n_sample:], axis=0), jnp.concatenate(xs[:n_sample], axis=0))
```

```python
import functools

import jax
import jax.numpy as jnp
from jax import lax
from jax.experimental import pallas as pl
from jax.experimental.pallas import tpu as pltpu
from jax.experimental.pallas import tpu_sc as plsc

F32 = jnp.float32
BF16 = jnp.bfloat16
EPS = 1e-6
LOG2_E = 1.4426950408889634

RG_HEADS = 8
RG_CONV = 4
RG_C = 8.0
MLA_HEADS = 8
QK_NOPE = 64
QK_ROPE = 32
V_DIM = 64
QK_DIM = QK_NOPE + QK_ROPE
ROPE_THETA = 10000.0
PEER_HEADS = 8
PEER_TOPK = 16
HEAD_SLAB = 128
MAX_SOFTMAX_BOUND = 40.0

LANES = 128
SUBLANES = 8
VMEM_LIMIT = 56 * 1024 * 1024

ROW_TILE = 512
SCAN_CHUNK = 1024
ATTN_TQ = 1024
ATTN_TK = 2048
ROUTE_TILE = 512
PEER_TB = 32
SC_TOKENS = 8
PIECE_TOKENS = 16384


def _cparams(sem):
    return pltpu.CompilerParams(dimension_semantics=sem, vmem_limit_bytes=VMEM_LIMIT)


def _norm_mod(x, gain, sc, sh):
    ms = jnp.mean(x * x, axis=-1, keepdims=True)
    return x * lax.rsqrt(ms + EPS) * gain * (1.0 + sc) + sh


def _rms(x, gain, n):
    ms = jnp.sum(x * x, axis=-1, keepdims=True) * (1.0 / n)
    return x * lax.rsqrt(ms + EPS) * gain


def _bdot(a, b):
    return jnp.dot(a.astype(BF16), b, preferred_element_type=F32)


def _mod_kernel(c_ref, w_ref, b_ref, o_ref):
    c = c_ref[...]
    s = c * jax.nn.sigmoid(c)
    o_ref[0] = _bdot(s, w_ref[0].astype(BF16)) + b_ref[0]


def _modulation(c_all, ada_w, ada_b):
    depth, d, n = ada_w.shape
    rows = c_all.shape[0]
    tn = 1536
    return pl.pallas_call(
        _mod_kernel,
        grid=(depth, n // tn),
        in_specs=[
            pl.BlockSpec((rows, d), lambda i, j: (0, 0)),
            pl.BlockSpec((1, d, tn), lambda i, j: (i, 0, j)),
            pl.BlockSpec((1, 1, tn), lambda i, j: (i, 0, j)),
        ],
        out_specs=pl.BlockSpec((1, rows, tn), lambda i, j: (i, 0, j)),
        out_shape=jax.ShapeDtypeStruct((depth, rows, n), F32),
        compiler_params=_cparams(("parallel", "parallel")),
        name="modulation",
    )(c_all, ada_w, ada_b.reshape(depth, 1, n))


def _in_proj_kernel(x_ref, g_ref, sc_ref, sh_ref, w_ref, z_ref):
    h = _norm_mod(x_ref[0], g_ref[...], sc_ref[0], sh_ref[0])
    z_ref[0] = _bdot(h, w_ref[...])


def _in_proj(x, gain, sc, sh, w):
    b, s, d = x.shape
    n = w.shape[1]
    tm = min(ROW_TILE, s)
    return pl.pallas_call(
        _in_proj_kernel,
        grid=(b, s // tm),
        in_specs=[
            pl.BlockSpec((1, tm, d), lambda i, j: (i, j, 0)),
            pl.BlockSpec((1, d), lambda i, j: (0, 0)),
            pl.BlockSpec((1, 1, d), lambda i, j: (i, 0, 0)),
            pl.BlockSpec((1, 1, d), lambda i, j: (i, 0, 0)),
            pl.BlockSpec((d, n), lambda i, j: (0, 0)),
        ],
        out_specs=pl.BlockSpec((1, tm, n), lambda i, j: (i, j, 0)),
        out_shape=jax.ShapeDtypeStruct((b, s, n), F32),
        compiler_params=_cparams(("parallel", "parallel")),
        name="in_proj",
    )(x, gain, sc, sh, w)


def _rglru_kernel(xf_ref, xfp_ref, xfn_ref, xb_ref, xbp_ref, xbn_ref, cw_ref, cb_ref, wg_ref, bg_ref, cl_ref,
                  hf_ref, hb_ref, ext_ref, a_ref, b_ref, carry_ref):
    j = pl.program_id(1)
    nc = pl.num_programs(1)
    tc = xf_ref.shape[1]
    dr = xf_ref.shape[2]
    nt = tc // SUBLANES

    @pl.when(j == 0)
    def _():
        carry_ref[...] = jnp.zeros_like(carry_ref)

    def gates(x_ref, xp_ref, xn_ref, first, last, d):
        ext_ref[0:SUBLANES] = jnp.where(first, 0.0, xp_ref[0])
        ext_ref[SUBLANES:SUBLANES + tc] = x_ref[0]
        ext_ref[SUBLANES + tc:2 * SUBLANES + tc] = jnp.where(last, 0.0, xn_ref[0])
        xc = cb_ref[...]
        for k in range(RG_CONV):
            xc = xc + cw_ref[k:k + 1, :] * ext_ref[SUBLANES - 2 + k:SUBLANES - 2 + k + tc]
        g = _bdot(xc, wg_ref[d]) + bg_ref[d]
        r = jax.nn.sigmoid(g[:, :dr])
        i = jax.nn.sigmoid(g[:, dr:])
        log_a = r * cl_ref[d]
        a = jnp.exp(log_a)
        b = jnp.sqrt(-jnp.tanh(log_a) * (1.0 + a * a)) * (i * xc)
        a_ref[d] = a
        b_ref[d] = b

    rows = lax.broadcasted_iota(jnp.int32, (SUBLANES, dr), 0)

    def scan(d, reverse, out_ref):
        def body(it, carry):
            t = (nt - 1 - it) if reverse else it
            off = pl.multiple_of(t * SUBLANES, SUBLANES)
            a = a_ref[d, pl.ds(off, SUBLANES), :]
            b = b_ref[d, pl.ds(off, SUBLANES), :]
            for s in (1, 2, 4):
                if reverse:
                    a_s = pltpu.roll(a, SUBLANES - s, 0)
                    b_s = pltpu.roll(b, SUBLANES - s, 0)
                    m = rows < SUBLANES - s
                else:
                    a_s = pltpu.roll(a, s, 0)
                    b_s = pltpu.roll(b, s, 0)
                    m = rows >= s
                b = jnp.where(m, a * b_s + b, b)
                a = jnp.where(m, a * a_s, a)
            h = b + a * carry
            out_ref[0, pl.ds(off, SUBLANES), :] = h
            edge = h[0:1] if reverse else h[SUBLANES - 1:SUBLANES]
            return jnp.broadcast_to(edge, (SUBLANES, dr))

        carry_ref[d] = lax.fori_loop(0, nt, body, carry_ref[d])

    gates(xf_ref, xfp_ref, xfn_ref, j == 0, j == nc - 1, 0)
    scan(0, False, hf_ref)
    gates(xb_ref, xbp_ref, xbn_ref, j == nc - 1, j == 0, 1)
    scan(1, True, hb_ref)


def _rglru(z, conv_w, conv_b, wg, bg, cl):
    b, s, _ = z.shape
    dr = conv_w.shape[1]
    tc = min(SCAN_CHUNK, s)
    nc = s // tc
    nb8 = s // SUBLANES
    cb8 = tc // SUBLANES

    def main(rev):
        return pl.BlockSpec((1, tc, dr), (lambda i, j: (i, nc - 1 - j, 0)) if rev else (lambda i, j: (i, j, 0)))

    def prev(rev):
        def f(i, j):
            c = (nc - 1 - j) if rev else j
            return (i, jnp.maximum(c * cb8 - 1, 0), 0)
        return pl.BlockSpec((1, SUBLANES, dr), f)

    def nxt(rev):
        def f(i, j):
            c = (nc - 1 - j) if rev else j
            return (i, jnp.minimum((c + 1) * cb8, nb8 - 1), 0)
        return pl.BlockSpec((1, SUBLANES, dr), f)

    def const(shape):
        return pl.BlockSpec(shape, lambda i, j: (0,) * len(shape))

    out_sd = jax.ShapeDtypeStruct((b, s, dr), F32)
    return pl.pallas_call(
        _rglru_kernel,
        grid=(b, nc),
        in_specs=[main(False), prev(False), nxt(False), main(True), prev(True), nxt(True),
                  const((RG_CONV, dr)), const((1, dr)), const((2, dr, 2 * dr)), const((2, 1, 2 * dr)),
                  const((2, 1, dr))],
        out_specs=[pl.BlockSpec((1, tc, dr), lambda i, j: (i, j, 0)),
                   pl.BlockSpec((1, tc, dr), lambda i, j: (i, nc - 1 - j, 0))],
        out_shape=[out_sd, out_sd],
        scratch_shapes=[pltpu.VMEM((tc + 2 * SUBLANES, dr), F32), pltpu.VMEM((2, tc, dr), F32),
                        pltpu.VMEM((2, tc, dr), F32), pltpu.VMEM((2, SUBLANES, dr), F32)],
        compiler_params=_cparams(("parallel", "arbitrary")),
        name="rglru",
    )(z, z, z, z, z, z, conv_w, conv_b, wg, bg, cl)


def _mla_proj_kernel(z_ref, qn_ref, kvn_ref, wuq_ref, wk_ref, wv_ref, gq_ref, gk_ref, qb_ref, kb_ref,
                     rc_ref, r1_ref, r2_ref, q_ref, k_ref, v_ref):
    zz = z_ref[0]
    q_lora = qn_ref.shape[1]
    kv_lora = kvn_ref.shape[1]
    ql = zz[:, :q_lora]
    kvl = zz[:, q_lora:q_lora + kv_lora]
    kr = zz[:, q_lora + kv_lora:]
    q = _bdot(_rms(ql, qn_ref[...], q_lora), wuq_ref[...])
    kvn = _rms(kvl, kvn_ref[...], kv_lora)
    kk = _bdot(kvn, wk_ref[...])
    vv = _bdot(kvn, wv_ref[...])
    rc, r1, r2 = rc_ref[...], r1_ref[...], r2_ref[...]
    half = QK_ROPE // 2
    scale = QK_DIM ** -0.5 * LOG2_E

    def norm_rope(xh, g):
        xh = _rms(xh, g, QK_DIM)
        return xh * rc + pltpu.roll(xh, HEAD_SLAB - half, 1) * r1 + pltpu.roll(xh, half, 1) * r2

    for h in range(MLA_HEADS):
        sl = slice(h * HEAD_SLAB, (h + 1) * HEAD_SLAB)
        q_ref[0, h] = (norm_rope(q[:, sl], gq_ref[...]) * scale + qb_ref[...]).astype(BF16)
        k_ref[0, h] = (norm_rope(kk[:, sl] + kr, gk_ref[...]) + kb_ref[...]).astype(BF16)
    for p in range(MLA_HEADS // 2):
        v_ref[0, p] = vv[:, p * LANES:(p + 1) * LANES].astype(BF16)


def _mla_proj(z, col_block, qn, kvn, wuq, wk, wv, gq, gk, qb, kb, rc, r1, r2):
    b, s, _ = z.shape
    tm = min(ROW_TILE, s)
    width = qn.shape[1] + kvn.shape[1] + HEAD_SLAB

    def const(a):
        return pl.BlockSpec(a.shape, lambda i, j: (0,) * a.ndim)

    def rope(a):
        return pl.BlockSpec((tm, HEAD_SLAB), lambda i, j: (j, 0))

    hp = MLA_HEADS // 2
    return pl.pallas_call(
        _mla_proj_kernel,
        grid=(b, s // tm),
        in_specs=[pl.BlockSpec((1, tm, width), lambda i, j: (i, j, col_block)),
                  const(qn), const(kvn), const(wuq), const(wk), const(wv), const(gq), const(gk), const(qb), const(kb),
                  rope(rc), rope(r1), rope(r2)],
        out_specs=[pl.BlockSpec((1, MLA_HEADS, tm, HEAD_SLAB), lambda i, j: (i, 0, j, 0)),
                   pl.BlockSpec((1, MLA_HEADS, tm, HEAD_SLAB), lambda i, j: (i, 0, j, 0)),
                   pl.BlockSpec((1, hp, tm, LANES), lambda i, j: (i, 0, j, 0))],
        out_shape=[jax.ShapeDtypeStruct((b, MLA_HEADS, s, HEAD_SLAB), BF16),
                   jax.ShapeDtypeStruct((b, MLA_HEADS, s, HEAD_SLAB), BF16),
                   jax.ShapeDtypeStruct((b, hp, s, LANES), BF16)],
        compiler_params=_cparams(("parallel", "parallel")),
        name="mla_proj",
    )(z, qn, kvn, wuq, wk, wv, gq, gk, qb, kb, rc, r1, r2)


def _flash_kernel(q_ref, k_ref, v_ref, o_ref, m_ref, l_ref, acc_ref):
    ik = pl.program_id(3)

    @pl.when(ik == 0)
    def _():
        m_ref[...] = jnp.full_like(m_ref, -jnp.inf)
        l_ref[...] = jnp.zeros_like(l_ref)
        acc_ref[...] = jnp.zeros_like(acc_ref)

    v = v_ref[0, 0]
    for hh in range(2):
        s = lax.dot_general(q_ref[0, hh], k_ref[0, hh], (((1,), (1,)), ((), ())), preferred_element_type=F32)
        m_prev = m_ref[hh]
        m_new = jnp.maximum(m_prev, jnp.max(s, axis=-1, keepdims=True))
        alpha = jnp.exp2(m_prev - m_new)
        p = jnp.exp2(s - m_new[:, :1])
        l_ref[hh] = alpha * l_ref[hh] + jnp.sum(p, axis=-1, keepdims=True)
        acc_ref[hh] = alpha * acc_ref[hh] + jnp.dot(p.astype(BF16), v, preferred_element_type=F32)
        m_ref[hh] = m_new

    @pl.when(ik == pl.num_programs(3) - 1)
    def _():
        lane = lax.broadcasted_iota(jnp.int32, acc_ref.shape[1:], 1)
        o_ref[0] = jnp.where(lane < V_DIM, acc_ref[0] / l_ref[0], acc_ref[1] / l_ref[1])


def _flash_bounded_kernel(q_ref, k_ref, v_ref, o_ref, l_ref, acc_ref):
    ik = pl.program_id(3)

    @pl.when(ik == 0)
    def _():
        l_ref[...] = jnp.zeros_like(l_ref)
        acc_ref[...] = jnp.zeros_like(acc_ref)

    v = v_ref[0, 0]
    for hh in range(2):
        s = lax.dot_general(q_ref[0, hh], k_ref[0, hh], (((1,), (1,)), ((), ())), preferred_element_type=F32)
        p = jnp.exp2(s)
        l_ref[hh] = l_ref[hh] + jnp.sum(p, axis=-1, keepdims=True)
        acc_ref[hh] = acc_ref[hh] + jnp.dot(p.astype(BF16), v, preferred_element_type=F32)

    @pl.when(ik == pl.num_programs(3) - 1)
    def _():
        lane = lax.broadcasted_iota(jnp.int32, acc_ref.shape[1:], 1)
        o_ref[0] = jnp.where(lane < V_DIM, acc_ref[0] / l_ref[0], acc_ref[1] / l_ref[1])


def _attention(q, k, v, bounded):
    b, h, s, _ = q.shape
    tq = min(ATTN_TQ, s)
    tk = min(ATTN_TK, s)
    hp = h // 2
    return pl.pallas_call(
        _flash_bounded_kernel if bounded else _flash_kernel,
        grid=(b, hp, s // tq, s // tk),
        in_specs=[pl.BlockSpec((1, 2, tq, HEAD_SLAB), lambda i, p, a, c: (i, p, a, 0)),
                  pl.BlockSpec((1, 2, tk, HEAD_SLAB), lambda i, p, a, c: (i, p, c, 0)),
                  pl.BlockSpec((1, 1, tk, LANES), lambda i, p, a, c: (i, p, c, 0))],
        out_specs=pl.BlockSpec((1, tq, LANES), lambda i, p, a, c: (i, a, p)),
        out_shape=jax.ShapeDtypeStruct((b, s, hp * LANES), F32),
        scratch_shapes=[pltpu.VMEM((2, tq, LANES), F32)] * (2 if bounded else 3),
        compiler_params=_cparams(("parallel", "parallel", "parallel", "arbitrary")),
        name="attention_bounded" if bounded else "attention",
    )(q, k, v)


def _ab_out_kernel(x_ref, hf_ref, hb_ref, y_ref, at_ref, w_ref, g_ref, o_ref):
    rg = (hf_ref[0] + hb_ref[0]) * jax.nn.gelu(y_ref[0])
    cat = jnp.concatenate([rg.astype(BF16), at_ref[0].astype(BF16)], axis=-1)
    m = jnp.dot(cat, w_ref[...], preferred_element_type=F32)
    o_ref[0] = x_ref[0] + g_ref[0] * m


def _ab_out(x, hf, hb, z, attn, w, gate):
    b, s, d = x.shape
    dr = hf.shape[2]
    da = attn.shape[2]
    tm = min(ROW_TILE, s)
    return pl.pallas_call(
        _ab_out_kernel,
        grid=(b, s // tm),
        in_specs=[pl.BlockSpec((1, tm, d), lambda i, j: (i, j, 0)),
                  pl.BlockSpec((1, tm, dr), lambda i, j: (i, j, 0)),
                  pl.BlockSpec((1, tm, dr), lambda i, j: (i, j, 0)),
                  pl.BlockSpec((1, tm, dr), lambda i, j: (i, j, 1)),
                  pl.BlockSpec((1, tm, da), lambda i, j: (i, j, 0)),
                  pl.BlockSpec(w.shape, lambda i, j: (0, 0)),
                  pl.BlockSpec((1, 1, d), lambda i, j: (i, 0, 0))],
        out_specs=pl.BlockSpec((1, tm, d), lambda i, j: (i, j, 0)),
        out_shape=jax.ShapeDtypeStruct((b, s, d), F32),
        compiler_params=_cparams(("parallel", "parallel")),
        name="ab_out",
    )(x, hf, hb, z, attn, w, gate)


def _c_in_kernel(x_ref, g_ref, sc_ref, sh_ref, w_ref, bg_ref, cx_ref):
    h = _norm_mod(x_ref[0], g_ref[...], sc_ref[0], sh_ref[0])
    z = _bdot(h, w_ref[...])
    dc = bg_ref.shape[2]
    bg_ref[0] = z[:, :dc]
    cx_ref[0] = z[:, dc:2 * dc] * z[:, 2 * dc:]


def _c_in(x, gain, sc, sh, w):
    b, s, d = x.shape
    dc = w.shape[1] // 3
    tm = min(ROW_TILE, s)
    sd = jax.ShapeDtypeStruct((b, s, dc), F32)
    return pl.pallas_call(
        _c_in_kernel,
        grid=(b, s // tm),
        in_specs=[pl.BlockSpec((1, tm, d), lambda i, j: (i, j, 0)),
                  pl.BlockSpec((1, d), lambda i, j: (0, 0)),
                  pl.BlockSpec((1, 1, d), lambda i, j: (i, 0, 0)),
                  pl.BlockSpec((1, 1, d), lambda i, j: (i, 0, 0)),
                  pl.BlockSpec(w.shape, lambda i, j: (0, 0))],
        out_specs=[pl.BlockSpec((1, tm, dc), lambda i, j: (i, j, 0))] * 2,
        out_shape=[sd, sd],
        compiler_params=_cparams(("parallel", "parallel")),
        name="c_in",
    )(x, gain, sc, sh, w)


def _c_out_kernel(x_ref, bg_ref, cx_ref, cp_ref, cn_ref, cw_ref, w_ref, g_ref, o_ref, ext_ref):
    j = pl.program_id(1)
    tm = cx_ref.shape[1]
    ext_ref[0:SUBLANES] = jnp.where(j == 0, 0.0, cp_ref[0])
    ext_ref[SUBLANES:SUBLANES + tm] = cx_ref[0]
    ext_ref[SUBLANES + tm:2 * SUBLANES + tm] = jnp.where(j == pl.num_programs(1) - 1, 0.0, cn_ref[0])
    conv = cw_ref[0:1, :] * ext_ref[SUBLANES - 1:SUBLANES - 1 + tm]
    conv = conv + cw_ref[1:2, :] * ext_ref[SUBLANES:SUBLANES + tm]
    conv = conv + cw_ref[2:3, :] * ext_ref[SUBLANES + 1:SUBLANES + 1 + tm]
    m = _bdot(bg_ref[0] * conv, w_ref[...])
    o_ref[0] = x_ref[0] + g_ref[0] * m


def _c_out(x, bg, cx, conv_w, w, gate):
    b, s, d = x.shape
    dc = bg.shape[2]
    tm = min(ROW_TILE, s)
    nb8 = s // SUBLANES
    t8 = tm // SUBLANES
    return pl.pallas_call(
        _c_out_kernel,
        grid=(b, s // tm),
        in_specs=[pl.BlockSpec((1, tm, d), lambda i, j: (i, j, 0)),
                  pl.BlockSpec((1, tm, dc), lambda i, j: (i, j, 0)),
                  pl.BlockSpec((1, tm, dc), lambda i, j: (i, j, 0)),
                  pl.BlockSpec((1, SUBLANES, dc), lambda i, j: (i, jnp.maximum(j * t8 - 1, 0), 0)),
                  pl.BlockSpec((1, SUBLANES, dc), lambda i, j: (i, jnp.minimum((j + 1) * t8, nb8 - 1), 0)),
                  pl.BlockSpec(conv_w.shape, lambda i, j: (0, 0)),
                  pl.BlockSpec(w.shape, lambda i, j: (0, 0)),
                  pl.BlockSpec((1, 1, d), lambda i, j: (i, 0, 0))],
        out_specs=pl.BlockSpec((1, tm, d), lambda i, j: (i, j, 0)),
        out_shape=jax.ShapeDtypeStruct((b, s, d), F32),
        scratch_shapes=[pltpu.VMEM((tm + 2 * SUBLANES, dc), F32)],
        compiler_params=_cparams(("parallel", "parallel")),
        name="c_out",
    )(x, bg, cx, cx, cx, conv_w, w, gate)


def _top_rows(s, k, payload=None):
    n = s.shape[0]
    ridx = lax.broadcasted_iota(jnp.int32, s.shape, 0).astype(F32)
    vals, ids = [], []
    for _ in range(k):
        m = jnp.max(s, axis=0, keepdims=True)
        first = jnp.min(jnp.where(s == m, ridx, float(n)), axis=0, keepdims=True)
        sel = ridx == first
        ids.append(first if payload is None else jnp.sum(jnp.where(sel, payload, 0.0), axis=0, keepdims=True))
        vals.append(m)
        s = jnp.where(sel, -jnp.inf, s)
    return vals, ids


_PAIRS = [(a, b) for a in range(PEER_TOPK) for b in range(PEER_TOPK) if (a + 1) * (b + 1) <= PEER_TOPK]


def _route_kernel(x_ref, g_ref, sc_ref, sh_ref, wq_ref, k1_ref, k2_ref, h_ref, idx_ref, gate_ref, q_scr, sel_scr):
    h = _norm_mod(x_ref[0], g_ref[...], sc_ref[0], sh_ref[0])
    h_ref[0] = h
    q_scr[...] = _bdot(h, wq_ref[...])
    n_keys = k1_ref.shape[0]
    half = k1_ref.shape[1]
    tm = x_ref.shape[1]
    nt = (((1,), (1,)), ((), ()))
    pad = -len(_PAIRS) % SUBLANES

    def head(hd, carry):
        off = pl.multiple_of(hd * 2 * half, 2 * half)
        q1 = q_scr[:, pl.ds(off, half)].astype(BF16)
        q2 = q_scr[:, pl.ds(off + half, half)].astype(BF16)
        s1 = lax.dot_general(k1_ref[...], q1, nt, preferred_element_type=F32)
        s2 = lax.dot_general(k2_ref[...], q2, nt, preferred_element_type=F32)
        v1, i1 = _top_rows(s1, PEER_TOPK)
        v2, i2 = _top_rows(s2, PEER_TOPK)
        cv = [v1[a] + v2[b] for a, b in _PAIRS] + [jnp.full((pad, tm), -jnp.inf, F32)]
        ce = [i1[a] * float(n_keys) + i2[b] for a, b in _PAIRS] + [jnp.zeros((pad, tm), F32)]
        vs, es = _top_rows(jnp.concatenate(cv, axis=0), PEER_TOPK, jnp.concatenate(ce, axis=0))
        vs = jnp.concatenate(vs, axis=0)
        e = jnp.exp(vs - vs[0:1])
        row = pl.multiple_of(hd * PEER_TOPK, PEER_TOPK)
        sel_scr[0, pl.ds(row, PEER_TOPK), :] = e / jnp.sum(e, axis=0, keepdims=True)
        sel_scr[1, pl.ds(row, PEER_TOPK), :] = jnp.concatenate(es, axis=0)
        return carry

    lax.fori_loop(0, PEER_HEADS, head, 0)
    gate_ref[0] = sel_scr[0].T
    idx_ref[0] = sel_scr[1].T.astype(jnp.int32)


def _route(x, gain, sc, sh, wq, k1, k2):
    b, s, d = x.shape
    tm = min(ROUTE_TILE, s)
    nsel = PEER_HEADS * PEER_TOPK
    return pl.pallas_call(
        _route_kernel,
        grid=(b, s // tm),
        in_specs=[pl.BlockSpec((1, tm, d), lambda i, j: (i, j, 0)),
                  pl.BlockSpec((1, d), lambda i, j: (0, 0)),
                  pl.BlockSpec((1, 1, d), lambda i, j: (i, 0, 0)),
                  pl.BlockSpec((1, 1, d), lambda i, j: (i, 0, 0)),
                  pl.BlockSpec(wq.shape, lambda i, j: (0, 0)),
                  pl.BlockSpec(k1.shape, lambda i, j: (0, 0)),
                  pl.BlockSpec(k2.shape, lambda i, j: (0, 0))],
        out_specs=[pl.BlockSpec((1, tm, d), lambda i, j: (i, j, 0)),
                   pl.BlockSpec((1, tm, nsel), lambda i, j: (i, j, 0)),
                   pl.BlockSpec((1, tm, nsel), lambda i, j: (i, j, 0))],
        out_shape=[jax.ShapeDtypeStruct((b, s, d), F32),
                   jax.ShapeDtypeStruct((b, s, nsel), jnp.int32),
                   jax.ShapeDtypeStruct((b, s, nsel), F32)],
        scratch_shapes=[pltpu.VMEM((tm, wq.shape[1]), F32), pltpu.VMEM((2, nsel, tm), F32)],
        compiler_params=_cparams(("parallel", "parallel")),
        name="peer_route",
    )(x, gain, sc, sh, wq, k1, k2)


def _pack_table(t):
    n, d = t.shape
    chunks = d // LANES
    bits = lax.bitcast_convert_type(t.astype(BF16), jnp.uint16).astype(jnp.uint32)
    bits = bits.reshape(n * chunks // 2, 2, LANES)
    return lax.bitcast_convert_type(bits[:, 0, :] | (bits[:, 1, :] << 16), jnp.int32)


def _gather_rows(idx_ref, t, tbl_ref, wr):
    rows = [tbl_ref[pl.ds(pl.multiple_of(idx_ref[t, r], wr), wr), :] for r in range(idx_ref.shape[1])]
    return pltpu.bitcast(jnp.concatenate(rows, axis=0), BF16)


def _idx_copy(idx_vmem, half, smem, sem):
    sub = smem.shape[0]
    return pltpu.make_async_copy(idx_vmem.at[pl.ds(half * sub, sub)], smem, sem)


def _staged_halves(idx_cur, idx_next, smem_a, smem_b, sems, process):
    i = pl.program_id(0)
    sub = smem_a.shape[0]

    @pl.when(i == 0)
    def _():
        _idx_copy(idx_cur, 0, smem_a, sems.at[0]).start()

    _idx_copy(idx_cur, 0, smem_a, sems.at[0]).wait()
    _idx_copy(idx_cur, 1, smem_b, sems.at[1]).start()
    process(smem_a, 0)
    _idx_copy(idx_cur, 1, smem_b, sems.at[1]).wait()

    @pl.when(i + 1 < pl.num_programs(0))
    def _():
        _idx_copy(idx_next, 0, smem_a, sems.at[0]).start()

    process(smem_b, sub)


def _split_bf16(a):
    hi = a.astype(BF16).astype(F32)
    return jnp.concatenate([hi, a - hi], axis=0).astype(BF16)


def _chunk_mask(chunks, width):
    lane = lax.broadcasted_iota(jnp.int32, (chunks, width), 1)
    return lane % chunks == lax.broadcasted_iota(jnp.int32, (chunks, width), 0)


def _peer_blocks(t):
    sub = min(PEER_TB, t // 2)
    assert t % (2 * sub) == 0
    return sub, 2 * sub


def _idx_scratch(sub, nsel):
    return [pltpu.SMEM((sub, nsel), jnp.int32), pltpu.SMEM((sub, nsel), jnp.int32), pltpu.SemaphoreType.DMA((2,))]


def _pack_pairs(t):
    d = t.shape[1]
    bits = lax.bitcast_convert_type(t.astype(BF16), jnp.uint16).astype(jnp.uint32)
    return lax.bitcast_convert_type(bits[:, :d // 2] | (bits[:, d // 2:] << 16), jnp.int32)


def _sc_scores(tbl, idx, x):
    t, nsel = idx.shape
    w = tbl.shape[1]
    info = plsc.get_sparse_core_info()
    lanes = info.num_lanes
    workers = info.num_cores * info.num_subcores
    per = t // workers
    half = nsel // 2
    assert t % (workers * SC_TOKENS) == 0 and half % lanes == 0 and w % lanes == 0
    mesh = plsc.VectorSubcoreMesh(core_axis_name="c", subcore_axis_name="s")

    @functools.partial(
        pl.kernel, mesh=mesh, out_type=jax.ShapeDtypeStruct((t, nsel), F32),
        scratch_types=[pltpu.VMEM((SC_TOKENS, nsel), jnp.int32), pltpu.VMEM((SC_TOKENS, 2 * w), F32),
                       pltpu.VMEM((2, half, w), jnp.int32), pltpu.VMEM((SC_TOKENS, nsel), F32),
                       pltpu.SemaphoreType.DMA((2,))],
        compiler_params=pltpu.CompilerParams(needs_layout_passes=False),
        name="peer_scores")
    def scores(tbl_hbm, idx_hbm, x_hbm, out_hbm, idx_v, x_v, rows_v, s_v, sems):
        wid = lax.axis_index("s") * info.num_cores + lax.axis_index("c")
        lane = lax.iota(jnp.int32, lanes)

        def gather(tt, hb):
            return pltpu.make_async_copy(tbl_hbm.at[idx_v.at[tt, pl.ds(hb * half, half)]], rows_v.at[hb], sems.at[hb])

        def compute(tt, hb):
            @pl.loop(0, half // lanes)
            def _(rg):
                def chunk(c, accs):
                    xl = x_v[tt, pl.ds(c * lanes, lanes)]
                    xh = x_v[tt, pl.ds(w + c * lanes, lanes)]
                    out = []
                    for r in range(lanes):
                        wv = rows_v[hb, rg * lanes + r, pl.ds(c * lanes, lanes)]
                        lo = lax.bitcast_convert_type(wv << 16, F32)
                        hi = lax.bitcast_convert_type(wv & jnp.int32(-65536), F32)
                        out.append(accs[r] + lo * xl + hi * xh)
                    return tuple(out)

                accs = lax.fori_loop(0, w // lanes, chunk, tuple(jnp.zeros((lanes,), F32) for _ in range(lanes)))
                res = jnp.zeros((lanes,), F32)
                for r in range(lanes):
                    res = jnp.where(lane == r, jnp.sum(accs[r]), res)
                s_v[tt, pl.ds(hb * half + rg * lanes, lanes)] = res

        @pl.loop(0, per // SC_TOKENS)
        def _(bi):
            t0 = wid * per + bi * SC_TOKENS
            pltpu.sync_copy(idx_hbm.at[pl.ds(t0, SC_TOKENS)], idx_v)
            pltpu.sync_copy(x_hbm.at[pl.ds(t0, SC_TOKENS)], x_v)
            gather(0, 0).start()

            @pl.loop(0, SC_TOKENS)
            def _(tt):
                gather(tt, 1).start()
                gather(tt, 0).wait()
                compute(tt, 0)

                @pl.when(tt + 1 < SC_TOKENS)
                def _():
                    gather(tt + 1, 0).start()

                gather(tt, 1).wait()
                compute(tt, 1)

            pltpu.sync_copy(s_v, out_hbm.at[pl.ds(t0, SC_TOKENS)])

    return scores(tbl, idx, x)


def _peer_v_kernel(idx_cur, idx_next, s_ref, gts_ref, spread_ref, tbl_ref, x_ref, g_ref, o_ref, smem_a, smem_b, sems, o_scr,
                   act_ref):
    tb = x_ref.shape[0]
    chunks = x_ref.shape[1] // LANES
    width = act_ref.shape[1]
    mask = _chunk_mask(chunks, width)
    a = _split_bf16(jax.nn.gelu(s_ref[...]) * gts_ref[...])
    a = jnp.dot(a, spread_ref[...], preferred_element_type=F32)
    act_ref[...] = a[:tb] + a[tb:]

    def process(idx_ref, first):
        for t in range(idx_ref.shape[0]):
            gb = _gather_rows(idx_ref, t, tbl_ref, chunks // 2)
            a = jnp.where(mask, jnp.broadcast_to(act_ref[first + t:first + t + 1, :], (chunks, width)), 0.0)
            o = jnp.dot(_split_bf16(a), gb, preferred_element_type=F32)
            o_scr[(first + t) * chunks:(first + t + 1) * chunks, :] = o[:chunks] + o[chunks:]

    _staged_halves(idx_cur, idx_next, smem_a, smem_b, sems, process)
    for c in range(chunks):
        sl = slice(c * LANES, (c + 1) * LANES)
        o_ref[:, sl] = x_ref[:, sl] + g_ref[0][:, sl] * o_scr[pl.ds(c, tb, stride=chunks), :]


def _peer_v(idx, scores, gts, table, x, gate, tokens_per_batch):
    t, nsel = idx.shape
    d = x.shape[1]
    chunks = d // LANES
    width = nsel * chunks
    sub, tb = _peer_blocks(t)
    assert tokens_per_batch % tb == 0
    spread = jnp.kron(jnp.eye(nsel, dtype=F32), jnp.ones((1, chunks), F32)).astype(BF16)
    steps = t // tb
    return pl.pallas_call(
        _peer_v_kernel,
        grid=(steps,),
        in_specs=[pl.BlockSpec((tb, nsel), lambda i: (i, 0)),
                  pl.BlockSpec((tb, nsel), lambda i: (jnp.minimum(i + 1, steps - 1), 0)),
                  pl.BlockSpec((tb, nsel), lambda i: (i, 0)),
                  pl.BlockSpec((tb, nsel), lambda i: (i, 0)),
                  pl.BlockSpec(spread.shape, lambda i: (0, 0), pipeline_mode=pl.Buffered(1)),
                  pl.BlockSpec(table.shape, lambda i: (0, 0), pipeline_mode=pl.Buffered(1)),
                  pl.BlockSpec((tb, d), lambda i: (i, 0)),
                  pl.BlockSpec((1, 1, d), lambda i: (i * tb // tokens_per_batch, 0, 0))],
        out_specs=pl.BlockSpec((tb, d), lambda i: (i, 0)),
        out_shape=jax.ShapeDtypeStruct(x.shape, F32),
        scratch_shapes=_idx_scratch(sub, nsel) + [pltpu.VMEM((tb * chunks, LANES), F32), pltpu.VMEM((tb, width), F32)],
        compiler_params=_cparams(("arbitrary",)),
        name="peer_v",
    )(idx, idx, scores, gts, spread, table, x, gate)


def _peer_front(x, gain, sc, sh, wq, k1, k2, u_tbl):
    b, s, d = x.shape
    h, idx, gts = _route(x, gain, sc, sh, wq, k1, k2)
    nsel = idx.shape[2]
    idx = idx.reshape(b * s, nsel)
    return idx, gts.reshape(b * s, nsel), _sc_scores(u_tbl, idx, h.reshape(b * s, d))


def _peer_back(x, gate, idx, gts, scores, v_tbl):
    b, s, d = x.shape
    out = _peer_v(idx * (d // LANES // 2), scores, gts, v_tbl, x.reshape(b * s, d), gate, s)
    return out.reshape(b, s, d)


def _block_diag(w):
    h, i, j = w.shape
    return jnp.einsum('hij,hg->higj', w, jnp.eye(h, dtype=w.dtype)).reshape(h * i, h * j)


def _rope_tables(s):
    half = QK_ROPE // 2
    inv = 1.0 / (ROPE_THETA ** (jnp.arange(0, QK_ROPE, 2, dtype=F32) / QK_ROPE))
    ang = jnp.arange(s, dtype=F32)[:, None] * inv[None, :]
    cos, sin = jnp.cos(ang), jnp.sin(ang)
    z = jnp.zeros((s, QK_NOPE), F32)
    tail = jnp.zeros((s, HEAD_SLAB - QK_DIM), F32)
    zh = jnp.zeros((s, half), F32)
    rc = jnp.concatenate([z + 1.0, cos, cos, tail + 1.0], axis=1)
    r1 = jnp.concatenate([z, -sin, zh, tail], axis=1)
    r2 = jnp.concatenate([z, zh, sin, tail], axis=1)
    return rc, r1, r2


def _pad_last(a, n):
    return jnp.pad(a, [(0, 0)] * (a.ndim - 1) + [(0, n - a.shape[-1])])


def _mixer_ab(x, n1, sc1, sh1, g1, p, j):
    b, s, d = x.shape
    d_rnn = p['rg_conv_w'].shape[2]
    q_lora = p['mla_q_norm'].shape[1]
    kv_lora = p['mla_kv_norm'].shape[1]
    w_in = p['ab_w_in'][j]
    lat = 2 * d_rnn + q_lora + kv_lora
    w0 = jnp.concatenate([w_in[:, :lat], jnp.zeros((d, QK_NOPE), F32), w_in[:, lat:],
                          jnp.zeros((d, HEAD_SLAB - QK_DIM), F32)], axis=1).astype(BF16)
    z = _in_proj(x, n1, sc1, sh1, w0)
    wg = jnp.stack([jnp.concatenate([_block_diag(p['rg_wa'][j][k]), _block_diag(p['rg_wx'][j][k])], axis=1)
                    for k in range(2)]).astype(BF16)
    bg = jnp.concatenate([p['rg_ba'][j], p['rg_bx'][j]], axis=1)[:, None, :]
    cl = (-RG_C * jax.nn.softplus(-p['rg_lambda'][j]))[:, None, :]
    hf, hb = _rglru(z, p['rg_conv_w'][j], p['rg_conv_b'][j][None, :], wg, bg, cl)
    wuq = _pad_last(p['mla_w_uq'][j].reshape(q_lora, MLA_HEADS, QK_DIM), HEAD_SLAB)
    wuq = wuq.reshape(q_lora, MLA_HEADS * HEAD_SLAB).astype(BF16)
    wkv = p['mla_w_ukv'][j].reshape(kv_lora, MLA_HEADS, QK_NOPE + V_DIM)
    wk = _pad_last(wkv[:, :, :QK_NOPE], HEAD_SLAB).reshape(kv_lora, MLA_HEADS * HEAD_SLAB).astype(BF16)
    wv = wkv[:, :, QK_NOPE:].reshape(kv_lora, MLA_HEADS * V_DIM).astype(BF16)
    gq = _pad_last(p['mla_qn_q'][j][None, :], HEAD_SLAB)
    gk = _pad_last(p['mla_qn_k'][j][None, :], HEAD_SLAB)
    rc, r1, r2 = _rope_tables(s)
    width = q_lora + kv_lora + HEAD_SLAB
    assert (2 * d_rnn) % width == 0
    bound = 1.02 * QK_DIM ** 0.5 * jnp.max(jnp.abs(gq)) * jnp.max(jnp.abs(gk))
    spare = jnp.arange(HEAD_SLAB)[None, :] == HEAD_SLAB - 1
    qb = jnp.where(spare, 1.0, 0.0).astype(F32)
    kb = jnp.where(spare, -bound * LOG2_E, 0.0).astype(F32)
    q, k, v = _mla_proj(z, 2 * d_rnn // width, p['mla_q_norm'][j][None, :], p['mla_kv_norm'][j][None, :],
                        wuq, wk, wv, gq, gk, qb, kb, rc, r1, r2)
    attn = lax.cond(bound < MAX_SOFTMAX_BOUND, functools.partial(_attention, bounded=True),
                    functools.partial(_attention, bounded=False), q, k, v)
    return _ab_out(x, hf, hb, z, attn, p['ab_w_out'][j].astype(BF16), g1)


def _mixer_c(x, n1, sc1, sh1, g1, p, j):
    bgate, cx = _c_in(x, n1, sc1, sh1, p['c_w_in'][j].astype(BF16))
    return _c_out(x, bgate, cx, p['c_conv_w'][j], p['c_w_out'][j].astype(BF16), g1)


def _layer_front(x, mod_i, p, i):
    sh1, sc1, g1, sh2, sc2, g2 = [m[:, None, :] for m in jnp.split(mod_i, 6, axis=-1)]
    mixer = _mixer_ab if i % 2 == 0 else _mixer_c
    x = mixer(x, p['norm1_g'][i][None, :], sc1, sh1, g1, p, i // 2)
    front = _peer_front(x, p['norm2_g'][i][None, :], sc2, sh2, p['peer_wq'][i].astype(BF16),
                        p['peer_k1'][i].astype(BF16), p['peer_k2'][i].astype(BF16), _pack_pairs(p['peer_u'][i]))
    return (x, g2) + front


def _layer_back(state, p, i):
    x, g2, idx, gts, scores = state
    return _peer_back(x, g2, idx, gts, scores, _pack_table(p['peer_v'][i]))


def kernel(x_prompt, x_sample, c_prompt, c_sample, ada_w, ada_b, norm1_g, norm2_g, ab_w_in, rg_conv_w, rg_conv_b, rg_wa, rg_ba, rg_wx, rg_bx, rg_lambda, mla_q_norm, mla_w_uq, mla_kv_norm, mla_w_ukv, mla_qn_q, mla_qn_k, ab_w_out, c_w_in, c_conv_w, c_w_out, peer_wq, peer_k1, peer_k2, peer_u, peer_v):
    p = dict(ada_w=ada_w, norm1_g=norm1_g, norm2_g=norm2_g, ab_w_in=ab_w_in, rg_conv_w=rg_conv_w,
             rg_conv_b=rg_conv_b, rg_wa=rg_wa, rg_ba=rg_ba, rg_wx=rg_wx, rg_bx=rg_bx, rg_lambda=rg_lambda,
             mla_q_norm=mla_q_norm, mla_w_uq=mla_w_uq, mla_kv_norm=mla_kv_norm, mla_w_ukv=mla_w_ukv,
             mla_qn_q=mla_qn_q, mla_qn_k=mla_qn_k, ab_w_out=ab_w_out, c_w_in=c_w_in, c_conv_w=c_conv_w,
             c_w_out=c_w_out, peer_wq=peer_wq, peer_k1=peer_k1, peer_k2=peer_k2, peer_u=peer_u, peer_v=peer_v)
    bp, bs = c_prompt.shape[0], c_sample.shape[0]
    rows = -(-(bp + bs) // SUBLANES) * SUBLANES
    c_all = jnp.pad(jnp.concatenate([c_prompt, c_sample], axis=0), ((0, rows - bp - bs), (0, 0)))
    mod = _modulation(c_all, ada_w, ada_b)
    def pieces(x, m):
        b, s, _ = x.shape
        per = max(1, min(b, PIECE_TOKENS // s))
        while b % per:
            per -= 1
        return [(x[j:j + per], m[:, j:j + per]) for j in range(0, b, per)]

    parts = pieces(x_sample, mod[:, bp:bp + bs])
    n_sample = len(parts)
    parts += pieces(x_prompt, mod[:, :bp])
    xs = [x for x, _ in parts]
    for i in range(ada_w.shape[0]):
        fronts = [_layer_front(x, m[i], p, i) for x, (_, m) in zip(xs, parts)]
        xs = [_layer_back(f, p, i) for f in fronts]
    return (jnp.concatenate(xs[n_sample:], axis=0), jnp.concatenate(xs[:n_sample], axis=0))
```
